```python
import jax, jax.numpy as jnp
from jax import lax
import numpy as np

D_MODEL = 1024
BATCH = 16
SEQ = 2048
DEPTH = 4

RWKV_HEADS = 8
RWKV_HEAD_DIM = 64
RWKV_DIM = RWKV_HEADS * RWKV_HEAD_DIM
DECAY_LORA = 64
ICLR_LORA = 64
GATE_LORA = 128
RWKV_GN_EPS = 64e-5
FOX_HEADS = 8
FOX_HEAD_DIM = 64
FOX_DIM = FOX_HEADS * FOX_HEAD_DIM
Q_BLOCK = 128
SSM_HEADS = 16
SSM_HEAD_DIM = 64
SSM_DIM = SSM_HEADS * SSM_HEAD_DIM
SSM_GROUPS = 2
SSM_STATE = 128
CONV_WIDTH = 4
SSD_CHUNK = 128
SSM_CONV_DIM = SSM_DIM + 2 * SSM_GROUPS * SSM_STATE
NORM_EPS = 1e-5
N_BRANCHES = 3
RWKV_COLS = 3 * RWKV_DIM + DECAY_LORA + ICLR_LORA + GATE_LORA
FOX_COLS = 3 * FOX_DIM + FOX_HEADS
SSM_COLS = SSM_DIM + SSM_CONV_DIM + SSM_HEADS
GATE_COLS = N_BRANCHES * D_MODEL
OFF_FOX = RWKV_COLS
OFF_SSM = OFF_FOX + FOX_COLS
OFF_GATE = OFF_SSM + SSM_COLS
IN_COLS = OFF_GATE + GATE_COLS
N_EXPERTS = 32
N_EXPERT_GROUPS = 4
EXPERTS_PER_GROUP = N_EXPERTS // N_EXPERT_GROUPS
TOP_K = 2
D_EXPERT = 512
DISPATCH_BLOCK = 128
DEEPNORM_ALPHA = (2 * DEPTH) ** 0.25
DEEPNORM_BETA = (8 * DEPTH) ** -0.25
LN_EPS = 1e-5

kernel_name = "hybrid_rwkv7_fox_mamba2_groupmoe_deepnorm"


def layer_norm(x, g, b):
    xf = x.astype(jnp.float32)
    mu = jnp.mean(xf, axis=-1, keepdims=True)
    var = jnp.mean(jnp.square(xf - mu), axis=-1, keepdims=True)
    y = (xf - mu) * lax.rsqrt(var + LN_EPS) * g.astype(jnp.float32) + b.astype(jnp.float32)
    return y.astype(x.dtype)


def token_shift(p):
    return jnp.pad(p[:, :-1], ((0, 0), (1, 0), (0, 0)))


def rwkv7_recurrence(r, w, k, v, a, b):
    Bsz, S, H, N = r.shape

    def step(state, inp):
        r_t, w_t, k_t, v_t, a_t, b_t = inp
        sa = jnp.einsum('bhij,bhj->bhi', state, a_t)
        state = (state * w_t[:, :, None, :] + sa[..., None] * b_t[:, :, None, :]
                 + v_t[..., None] * k_t[:, :, None, :])
        return state, jnp.einsum('bhij,bhj->bhi', state, r_t)

    xs = tuple(t.transpose(1, 0, 2, 3) for t in (r, w, k, v, a, b))
    _, y = lax.scan(step, jnp.zeros((Bsz, H, N, N), jnp.float32), xs)
    return y.transpose(1, 0, 2, 3)


def rwkv7_mixer(p, mu, w0, w_up, a0, a_up, g_up, k_k, k_a, r_k, gn_g, gn_b):
    Bsz, S, _ = p.shape
    f32 = jnp.float32
    p = p + mu * (token_shift(p) - p)
    R = RWKV_DIM
    r, k, v, wd, ad, gd = jnp.split(p, [R, 2 * R, 3 * R, 3 * R + DECAY_LORA,
                                        3 * R + DECAY_LORA + ICLR_LORA], axis=-1)
    w_raw = -jax.nn.softplus(-(w0 + jnp.tanh(wd) @ w_up)) - 0.5
    decay = jnp.exp(-jnp.exp(w_raw.astype(f32)))
    a = jax.nn.sigmoid(a0 + ad @ a_up)
    g = jax.nn.sigmoid(gd) @ g_up
    hs = lambda t: t.astype(f32).reshape(Bsz, S, RWKV_HEADS, RWKV_HEAD_DIM)
    hv = lambda t: t.astype(f32).reshape(RWKV_HEADS, RWKV_HEAD_DIM)
    r, k, v, a, decay = hs(r), hs(k), hs(v), hs(a), hs(decay)
    kk = k * hv(k_k)
    kk = kk / jnp.maximum(jnp.sqrt(jnp.sum(kk * kk, axis=-1, keepdims=True)), 1e-12)
    k = k * (1.0 + (a - 1.0) * hv(k_a))
    y = rwkv7_recurrence(r, decay, k, v, -kk, kk * a)
    mean = jnp.mean(y, axis=-1, keepdims=True)
    var = jnp.mean(jnp.square(y - mean), axis=-1, keepdims=True)
    y = (y - mean) * lax.rsqrt(var + RWKV_GN_EPS)
    y = y * hv(gn_g) + hv(gn_b)
    y = y + jnp.sum(r * k * hv(r_k), axis=-1, keepdims=True) * v
    y = y.reshape(Bsz, S, RWKV_DIM) * g.astype(f32)
    return y.astype(p.dtype)


def fox_mixer(p, f_bias):
    Bsz, S, _ = p.shape
    f32 = jnp.float32
    q, k, v, f = jnp.split(p, [FOX_DIM, 2 * FOX_DIM, 3 * FOX_DIM], axis=-1)
    hs = lambda t: t.astype(f32).reshape(Bsz, S, FOX_HEADS, FOX_HEAD_DIM)
    q, k, v = hs(q), hs(k), hs(v)
    log_f = jax.nn.log_sigmoid(f.astype(f32) + f_bias.astype(f32))
    c = jnp.cumsum(log_f, axis=1)
    c_key = c.transpose(0, 2, 1)
    nb = S // Q_BLOCK
    qb = q.reshape(Bsz, nb, Q_BLOCK, FOX_HEADS, FOX_HEAD_DIM).transpose(1, 0, 2, 3, 4)
    cb = c.reshape(Bsz, nb, Q_BLOCK, FOX_HEADS).transpose(1, 0, 3, 2)
    kpos = jnp.arange(S)
    scale = FOX_HEAD_DIM ** -0.5

    def block(args):
        qi, ci, i = args
        s = jnp.einsum('bqhe,bkhe->bhqk', qi, k) * scale
        s = s + ci[..., None] - c_key[:, :, None, :]
        qpos = i * Q_BLOCK + jnp.arange(Q_BLOCK)
        s = jnp.where(kpos[None, :] <= qpos[:, None], s, -jnp.inf)
        return jnp.einsum('bhqk,bkhe->bqhe', jax.nn.softmax(s, axis=-1), v)

    o = lax.map(block, (qb, cb, jnp.arange(nb)))
    return o.transpose(1, 0, 2, 3, 4).reshape(Bsz, S, FOX_DIM).astype(p.dtype)


def causal_depthwise_conv(u, w, b):
    out = lax.conv_general_dilated(u, w[:, None, :], window_strides=(1,),
                                   padding=[(CONV_WIDTH - 1, 0)],
                                   dimension_numbers=('NWC', 'WIO', 'NWC'),
                                   feature_group_count=u.shape[-1])
    return out + b


def segsum(a):
    T = a.shape[-1]
    cs = jnp.cumsum(a, axis=-1)
    diff = cs[..., :, None] - cs[..., None, :]
    return jnp.where(jnp.tril(jnp.ones((T, T), bool)), diff, -jnp.inf)


def ssd(x, dt, A, Bm, Cm):
    Bsz, S, H, P = x.shape
    G, N = Bm.shape[2], Bm.shape[3]
    E = H // G
    L = SSD_CHUNK
    nc = S // L
    X = (x * dt[..., None]).reshape(Bsz, nc, L, G, E, P)
    dA = (dt * A).reshape(Bsz, nc, L, G, E).transpose(0, 3, 4, 1, 2)
    Bc = Bm.reshape(Bsz, nc, L, G, N)
    Cc = Cm.reshape(Bsz, nc, L, G, N)
    A_cs = jnp.cumsum(dA, axis=-1)
    Lmat = jnp.exp(segsum(dA))
    CB = jnp.einsum('bclgn,bcsgn->bcgls', Cc, Bc)
    y_diag = jnp.einsum('bcgls,bgecls,bcsgep->bclgep', CB, Lmat, X)
    decay_states = jnp.exp(A_cs[..., -1:] - A_cs)
    states = jnp.einsum('bclgn,bgecl,bclgep->bcgepn', Bc, decay_states, X)
    states = jnp.concatenate([jnp.zeros_like(states[:, :1]), states], axis=1)
    chunk_decay = jnp.exp(segsum(jnp.pad(A_cs[..., -1], ((0, 0), (0, 0), (0, 0), (1, 0)))))
    states = jnp.einsum('bgezc,bcgepn->bzgepn', chunk_decay, states)[:, :-1]
    y_off = jnp.einsum('bclgn,bcgepn,bgecl->bclgep', Cc, states, jnp.exp(A_cs))
    return (y_diag + y_off).reshape(Bsz, S, H, P)


def mamba2_mixer(p, conv_w, conv_b, dt_bias, a_log, d_skip, norm_g):
    Bsz, S, _ = p.shape
    f32 = jnp.float32
    z, xbc, dt = jnp.split(p, [SSM_DIM, SSM_DIM + SSM_CONV_DIM], axis=-1)
    xbc = jax.nn.silu(causal_depthwise_conv(xbc, conv_w, conv_b)).astype(f32)
    xs, Bm, Cm = jnp.split(xbc, [SSM_DIM, SSM_DIM + SSM_GROUPS * SSM_STATE], axis=-1)
    xs = xs.reshape(Bsz, S, SSM_HEADS, SSM_HEAD_DIM)
    Bm = Bm.reshape(Bsz, S, SSM_GROUPS, SSM_STATE)
    Cm = Cm.reshape(Bsz, S, SSM_GROUPS, SSM_STATE)
    dt = jax.nn.softplus(dt.astype(f32) + dt_bias.astype(f32))
    A = -jnp.exp(a_log.astype(f32))
    y = ssd(xs, dt, A, Bm, Cm) + d_skip.astype(f32)[:, None] * xs
    y = y.reshape(Bsz, S, SSM_DIM) * jax.nn.silu(z.astype(f32))
    y = y.reshape(Bsz, S, SSM_GROUPS, SSM_DIM // SSM_GROUPS)
    y = y * lax.rsqrt(jnp.mean(y * y, axis=-1, keepdims=True) + NORM_EPS)
    y = y.reshape(Bsz, S, SSM_DIM) * norm_g.astype(f32)
    return y.astype(p.dtype)


def hybrid_mixer(x, w_in, rwkv_mu, rwkv_w0, rwkv_w_up, rwkv_a0, rwkv_a_up, rwkv_g_up,
                 rwkv_k_k, rwkv_k_a, rwkv_r_k, rwkv_gn_g, rwkv_gn_b, fox_f_bias,
                 ssm_conv_w, ssm_conv_b, ssm_dt_bias, ssm_a_log, ssm_d, ssm_norm_g,
                 gate_bias, w_br_rwkv, w_br_fox, w_br_ssm, w_out):
    Bsz, S, D = x.shape
    proj = x @ w_in
    p_rwkv = proj[..., :OFF_FOX]
    p_fox = proj[..., OFF_FOX:OFF_SSM]
    p_ssm = proj[..., OFF_SSM:OFF_GATE]
    p_gate = proj[..., OFF_GATE:].reshape(Bsz, S, N_BRANCHES, D)
    o_r = rwkv7_mixer(p_rwkv, rwkv_mu, rwkv_w0, rwkv_w_up, rwkv_a0, rwkv_a_up, rwkv_g_up,
                      rwkv_k_k, rwkv_k_a, rwkv_r_k, rwkv_gn_g, rwkv_gn_b)
    o_f = fox_mixer(p_fox, fox_f_bias)
    o_m = mamba2_mixer(p_ssm, ssm_conv_w, ssm_conv_b, ssm_dt_bias, ssm_a_log, ssm_d, ssm_norm_g)
    gates = jax.nn.sigmoid(p_gate + gate_bias)
    merged = (gates[:, :, 0] * (o_r @ w_br_rwkv) + gates[:, :, 1] * (o_f @ w_br_fox)
              + gates[:, :, 2] * (o_m @ w_br_ssm))
    return merged @ w_out


def moe_ffn(x, router_w, router_bias, w_gate, w_up, w_down):
    Bsz, S, D = x.shape
    N = Bsz * S
    x2 = x.reshape(N, D)
    f32 = jnp.float32
    aff = jax.nn.sigmoid(x2.astype(f32) @ router_w.astype(f32))
    sel = aff + router_bias.astype(f32)
    grp_score = lax.top_k(sel.reshape(N, N_EXPERT_GROUPS, EXPERTS_PER_GROUP), TOP_K)[0].sum(-1)
    top_grp = jnp.argmax(grp_score, axis=-1)
    in_grp = (jnp.arange(N_EXPERTS) // EXPERTS_PER_GROUP)[None, :] == top_grp[:, None]
    _, idx = lax.top_k(jnp.where(in_grp, sel, -jnp.inf), TOP_K)
    w_sel = jnp.take_along_axis(aff, idx, axis=-1)
    gate = w_sel / jnp.sum(w_sel, axis=-1, keepdims=True)
    M = N * TOP_K
    BLK = DISPATCH_BLOCK
    flat_e = idx.reshape(M)
    flat_tok = jnp.broadcast_to(jnp.arange(N, dtype=jnp.int32)[:, None], (N, TOP_K)).reshape(M)
    flat_gate = gate.reshape(M)
    order = jnp.argsort(flat_e)
    e_sorted, tok_sorted, gate_sorted = flat_e[order], flat_tok[order], flat_gate[order]
    counts = jnp.bincount(flat_e, length=N_EXPERTS)
    padded = (counts + BLK - 1) // BLK * BLK
    pad_end = jnp.cumsum(padded)
    pad_start = pad_end - padded
    start = jnp.cumsum(counts) - counts
    dest = pad_start[e_sorted] + jnp.arange(M) - start[e_sorted]
    n_blocks = -(-M // BLK) + N_EXPERTS
    buf_tok = jnp.zeros((n_blocks * BLK,), jnp.int32).at[dest].set(tok_sorted)
    block_expert = jnp.minimum(jnp.searchsorted(pad_end, jnp.arange(n_blocks) * BLK, side='right'),
                               N_EXPERTS - 1)
    xb = x2[buf_tok].reshape(n_blocks, BLK, D)

    def expert_block(args):
        xblk, e = args
        h = jax.nn.silu(xblk @ w_gate[e]) * (xblk @ w_up[e])
        return h @ w_down[e]

    yb = lax.map(expert_block, (xb, block_expert)).reshape(n_blocks * BLK, D)
    y = jnp.zeros((N, D), x.dtype).at[tok_sorted].add(yb[dest] * gate_sorted[:, None].astype(x.dtype))
    return y.reshape(Bsz, S, D)


def setup_inputs(seed: int = 0) -> dict:
    key = jax.random.key(seed)
    keys = list(jax.random.split(key, 48))
    cnt = [0]
    f32 = jnp.float32

    def nk():
        cnt[0] += 1
        return keys[cnt[0] - 1]

    def normal(shape, scale):
        return jax.random.normal(nk(), shape, f32) * scale

    def uniform(shape, lo, hi):
        return jax.random.uniform(nk(), shape, f32, lo, hi)

    L, D = DEPTH, D_MODEL
    x = normal((BATCH, SEQ, D), 1.0)
    w_in = normal((L, D, IN_COLS), D ** -0.5)
    rwkv_mu = uniform((L, RWKV_COLS), 0.0, 1.0)
    rwkv_w0 = uniform((L, RWKV_DIM), -6.5, -1.5)
    rwkv_w_up = normal((L, DECAY_LORA, RWKV_DIM), 0.1 * DECAY_LORA ** -0.5)
    rwkv_a0 = normal((L, RWKV_DIM), 0.1)
    rwkv_a_up = normal((L, ICLR_LORA, RWKV_DIM), ICLR_LORA ** -0.5)
    rwkv_g_up = normal((L, GATE_LORA, RWKV_DIM), GATE_LORA ** -0.5)
    rwkv_k_k = 0.85 + normal((L, RWKV_DIM), 0.05)
    rwkv_k_a = 1.0 + normal((L, RWKV_DIM), 0.05)
    rwkv_r_k = normal((L, RWKV_DIM), 0.1)
    rwkv_gn_g = 1.0 + normal((L, RWKV_DIM), 0.05)
    rwkv_gn_b = normal((L, RWKV_DIM), 0.02)
    fox_f_bias = uniform((L, FOX_HEADS), 1.0, 4.0)
    ssm_conv_w = normal((L, CONV_WIDTH, SSM_CONV_DIM), CONV_WIDTH ** -0.5)
    ssm_conv_b = normal((L, SSM_CONV_DIM), 0.02)
    dt0 = jnp.exp(uniform((L, SSM_HEADS), float(np.log(1e-3)), float(np.log(1e-1))))
    ssm_dt_bias = dt0 + jnp.log(-jnp.expm1(-dt0))
    ssm_a_log = jnp.log(uniform((L, SSM_HEADS), 1.0, 16.0))
    ssm_d = 1.0 + normal((L, SSM_HEADS), 0.1)
    ssm_norm_g = 1.0 + normal((L, SSM_DIM), 0.05)
    gate_bias = normal((L, N_BRANCHES, D), 0.02)
    w_br_rwkv = normal((L, RWKV_DIM, D), RWKV_DIM ** -0.5)
    w_br_fox = normal((L, FOX_DIM, D), FOX_DIM ** -0.5)
    w_br_ssm = normal((L, SSM_DIM, D), SSM_DIM ** -0.5)
    w_out = normal((L, D, D), DEEPNORM_BETA * D ** -0.5)
    ln1_g = 1.0 + normal((L, D), 0.05)
    ln1_b = normal((L, D), 0.02)
    router_w = normal((D, N_EXPERTS), D ** -0.5)
    router_bias = normal((N_EXPERTS,), 0.01)
    exp_w_gate = normal((L, N_EXPERTS, D, D_EXPERT), D ** -0.5)
    exp_w_up = normal((L, N_EXPERTS, D, D_EXPERT), D ** -0.5)
    exp_w_down = normal((L, N_EXPERTS, D_EXPERT, D), DEEPNORM_BETA * D_EXPERT ** -0.5)
    ln2_g = 1.0 + normal((L, D), 0.05)
    ln2_b = normal((L, D), 0.02)
    return {"x": x, "w_in": w_in, "rwkv_mu": rwkv_mu, "rwkv_w0": rwkv_w0,
            "rwkv_w_up": rwkv_w_up, "rwkv_a0": rwkv_a0, "rwkv_a_up": rwkv_a_up,
            "rwkv_g_up": rwkv_g_up, "rwkv_k_k": rwkv_k_k, "rwkv_k_a": rwkv_k_a,
            "rwkv_r_k": rwkv_r_k, "rwkv_gn_g": rwkv_gn_g, "rwkv_gn_b": rwkv_gn_b,
            "fox_f_bias": fox_f_bias, "ssm_conv_w": ssm_conv_w, "ssm_conv_b": ssm_conv_b,
            "ssm_dt_bias": ssm_dt_bias, "ssm_a_log": ssm_a_log, "ssm_d": ssm_d,
            "ssm_norm_g": ssm_norm_g, "gate_bias": gate_bias, "w_br_rwkv": w_br_rwkv,
            "w_br_fox": w_br_fox, "w_br_ssm": w_br_ssm, "w_out": w_out,
            "ln1_g": ln1_g, "ln1_b": ln1_b, "router_w": router_w, "router_bias": router_bias,
            "exp_w_gate": exp_w_gate, "exp_w_up": exp_w_up, "exp_w_down": exp_w_down,
            "ln2_g": ln2_g, "ln2_b": ln2_b}


def reference(x, w_in, rwkv_mu, rwkv_w0, rwkv_w_up, rwkv_a0, rwkv_a_up, rwkv_g_up,
              rwkv_k_k, rwkv_k_a, rwkv_r_k, rwkv_gn_g, rwkv_gn_b, fox_f_bias,
              ssm_conv_w, ssm_conv_b, ssm_dt_bias, ssm_a_log, ssm_d, ssm_norm_g,
              gate_bias, w_br_rwkv, w_br_fox, w_br_ssm, w_out, ln1_g, ln1_b,
              router_w, router_bias, exp_w_gate, exp_w_up, exp_w_down, ln2_g, ln2_b):
    for l in range(DEPTH):
        mix = hybrid_mixer(x, w_in[l], rwkv_mu[l], rwkv_w0[l], rwkv_w_up[l], rwkv_a0[l],
                           rwkv_a_up[l], rwkv_g_up[l], rwkv_k_k[l], rwkv_k_a[l], rwkv_r_k[l],
                           rwkv_gn_g[l], rwkv_gn_b[l], fox_f_bias[l], ssm_conv_w[l],
                           ssm_conv_b[l], ssm_dt_bias[l], ssm_a_log[l], ssm_d[l],
                           ssm_norm_g[l], gate_bias[l], w_br_rwkv[l], w_br_fox[l],
                           w_br_ssm[l], w_out[l])
        x = layer_norm(DEEPNORM_ALPHA * x + mix, ln1_g[l], ln1_b[l])
        ffn = moe_ffn(x, router_w, router_bias, exp_w_gate[l], exp_w_up[l], exp_w_down[l])
        x = layer_norm(DEEPNORM_ALPHA * x + ffn, ln2_g[l], ln2_b[l])
    return x
```

```python
import functools
import math

import jax
import jax.numpy as jnp
from jax import lax
from jax.experimental import pallas as pl
from jax.experimental.pallas import tpu as pltpu

F32 = jnp.float32
BF16 = jnp.bfloat16

LANES = 128
SUBLANES = 8
VMEM_LIMIT = 48 * 1024 * 1024

D_MODEL = 1024
HEAD = 64
PAIR = LANES // HEAD
RWKV_DIM = 512
DECAY_LORA = 64
ICLR_LORA = 64
GATE_LORA = 128
RWKV_COLS = 3 * RWKV_DIM + DECAY_LORA + ICLR_LORA + GATE_LORA
RWKV_GN_EPS = 64e-5
FOX_DIM = 512
FOX_HEADS = 8
SSM_DIM = 1024
SSM_HEADS = 16
SSM_GROUPS = 2
SSM_STATE = 128
CONV_WIDTH = 4
SSM_CONV_DIM = SSM_DIM + 2 * SSM_GROUPS * SSM_STATE
NORM_EPS = 1e-5
N_EXPERTS = 32
N_EXPERT_GROUPS = 4
EXPERTS_PER_GROUP = N_EXPERTS // N_EXPERT_GROUPS
TOP_K = 2
D_EXPERT = 512
LN_EPS = 1e-5
OFF_FOX = RWKV_COLS
OFF_SSM = OFF_FOX + 3 * FOX_DIM + FOX_HEADS
OFF_GATE = OFF_SSM + SSM_DIM + SSM_CONV_DIM + SSM_HEADS

SMALL_ROWS = 32
DT_OFF = 8
RWKV_CHUNK = 64
SSD_CHUNK = 128
MOE_BLOCK = 256


def _bf(x):
    return x.astype(BF16)


def _dot(a, b):
    return jnp.dot(a, b, preferred_element_type=F32)


def _dot_nt(a, b):
    return lax.dot_general(a, b, (((1,), (1,)), ((), ())), preferred_element_type=F32)


def _dot_tn(a, b):
    return lax.dot_general(a, b, (((0,), (0,)), ((), ())), preferred_element_type=F32)


def _split(x, parts):
    out = []
    rem = x
    for _ in range(parts):
        hi = _bf(rem)
        out.append(hi)
        rem = rem - hi.astype(F32)
    return out


def _dot_sel_rhs(m01, x, parts=3):
    acc = None
    for part in _split(x, parts):
        t = _dot(m01, part)
        acc = t if acc is None else acc + t
    return acc


def _dot_sel_lhs(x, m01, parts=3):
    acc = None
    for part in _split(x, parts):
        t = _dot(part, m01)
        acc = t if acc is None else acc + t
    return acc


def _sigmoid(x):
    return 1.0 / (1.0 + jnp.exp(-x))


def _softplus(x):
    return jnp.maximum(x, 0.0) + jnp.log(1.0 + jnp.exp(-jnp.abs(x)))


def _log_sigmoid(x):
    return jnp.minimum(x, 0.0) - jnp.log(1.0 + jnp.exp(-jnp.abs(x)))


def _iota2(shape, dim):
    return lax.broadcasted_iota(jnp.int32, shape, dim)


def _params(*sem):
    return pltpu.CompilerParams(dimension_semantics=sem, vmem_limit_bytes=VMEM_LIMIT)


def _mm_kernel(x_ref, w_ref, o_ref):
    o_ref[...] = _dot(x_ref[...], w_ref[...]).astype(o_ref.dtype)


def _matmul(x, w, tm, tn, out_dtype=F32):
    n, k = x.shape
    nc = w.shape[1]
    return pl.pallas_call(
        _mm_kernel,
        grid=(n // tm, nc // tn),
        in_specs=[pl.BlockSpec((tm, k), lambda i, j: (i, 0)),
                  pl.BlockSpec((k, tn), lambda i, j: (0, j))],
        out_specs=pl.BlockSpec((tm, tn), lambda i, j: (i, j)),
        out_shape=jax.ShapeDtypeStruct((n, nc), out_dtype),
        compiler_params=_params("parallel", "parallel"),
        name="in_proj",
    )(x, w)


def _mm_nt_kernel(wt_ref, x_ref, o_ref):
    o_ref[...] = _dot_nt(wt_ref[...], x_ref[...])


def _matmul_nt(wt, x, tm):
    r, k = wt.shape
    n = x.shape[0]
    return pl.pallas_call(
        _mm_nt_kernel,
        grid=(n // tm,),
        in_specs=[pl.BlockSpec((r, k), lambda i: (0, 0)),
                  pl.BlockSpec((tm, k), lambda i: (i, 0))],
        out_specs=pl.BlockSpec((r, tm), lambda i: (0, i)),
        out_shape=jax.ShapeDtypeStruct((r, n), F32),
        compiler_params=_params("parallel"),
        name="in_proj_small_t",
    )(wt, x)


def _rwkv_prep_kernel(p_ref, mu_ref, w0_ref, wup_ref, a0_ref, aup_ref, gup_ref, kk_ref, ka_ref,
                      bd_ref, r_o, lw_o, k_o, v_o, kn_o, b_o, g_o, carry_ref):
    @pl.when(pl.program_id(1) == 0)
    def _():
        carry_ref[...] = jnp.zeros_like(carry_ref)

    p = p_ref[...]
    ts = p.shape[0]
    row = _iota2((ts, 1), 0)
    prev = jnp.where(row == 0, carry_ref[...], pltpu.roll(p, 1, axis=0))
    carry_ref[...] = p[ts - 1:ts, :]
    p = p + mu_ref[...] * (prev - p)

    R = RWKV_DIM
    r = p[:, 0:R]
    k = p[:, R:2 * R]
    v = p[:, 2 * R:3 * R]
    lora = p[:, 3 * R:3 * R + DECAY_LORA + ICLR_LORA]
    gd = p[:, 3 * R + DECAY_LORA + ICLR_LORA:]
    w_pre = w0_ref[...] + _dot(_bf(jnp.tanh(lora)), wup_ref[...])
    lw = -jnp.exp(_log_sigmoid(w_pre) - 0.5)
    a = _sigmoid(a0_ref[...] + _dot(_bf(lora), aup_ref[...]))
    g = _dot(_bf(_sigmoid(gd)), gup_ref[...])
    kk = k * kk_ref[...]
    ss = _dot_sel_lhs(kk * kk, bd_ref[...], parts=2)
    kn = kk / jnp.maximum(jnp.sqrt(ss), 1e-12)
    r_o[...] = r
    lw_o[...] = lw
    k_o[...] = k * (1.0 + (a - 1.0) * ka_ref[...])
    v_o[...] = v
    kn_o[...] = kn
    b_o[...] = kn * a
    g_o[...] = g


def _rwkv_prep(p, B, S, ts, mu, w0, wup, a0, aup, gup, k_k, k_a, bd):
    n = B * S
    ns = S // ts
    row_spec = lambda w: pl.BlockSpec((ts, w), lambda b, s: (b * ns + s, 0))
    full = lambda a: pl.BlockSpec(a.shape, lambda b, s: (0,) * a.ndim)
    out = jax.ShapeDtypeStruct((n, RWKV_DIM), F32)
    return pl.pallas_call(
        _rwkv_prep_kernel,
        grid=(B, ns),
        in_specs=[row_spec(RWKV_COLS)] + [full(a) for a in (mu, w0, wup, a0, aup, gup, k_k, k_a, bd)],
        out_specs=[row_spec(RWKV_DIM)] * 7,
        out_shape=[out] * 7,
        scratch_shapes=[pltpu.VMEM((1, RWKV_COLS), F32)],
        compiler_params=_params("parallel", "arbitrary"),
        name="rwkv_prep",
    )(p, mu, w0, wup, a0, aup, gup, k_k, k_a, bd)


def _rwkv_chunk_kernel(r_ref, lw_ref, k_ref, v_ref, kn_ref, b_ref, g_ref, gng_ref, gnb_ref, rk_ref,
                       o_ref, s_ref, *, n_chunks):
    C = RWKV_CHUNK

    @pl.when(pl.program_id(2) == 0)
    def _():
        s_ref[...] = jnp.zeros_like(s_ref)

    lane = _iota2((1, LANES), 1)
    m0 = (lane < HEAD).astype(F32)
    m1 = 1.0 - m0
    ri = _iota2((LANES, LANES), 0)
    ci = _iota2((LANES, LANES), 1)
    same = (ri >= HEAD) == (ci >= HEAD)
    smask = jnp.where(same & (ri > ci), 1.0, 0.0)
    imask = jnp.where(same & (ri >= ci), 1.0, 0.0)
    eye = jnp.where(ri == ci, 1.0, 0.0)
    bd_mean = _bf(jnp.where(same, 1.0 / HEAD, 0.0))
    bd_one = _bf(jnp.where(same, 1.0, 0.0))
    tri = _bf(jnp.where(_iota2((C, C), 0) >= _iota2((C, C), 1), 1.0, 0.0))

    def stack(x):
        return jnp.concatenate([x * m0, x * m1], axis=0)

    gng = gng_ref[...]
    gnb = gnb_ref[...]
    rk = rk_ref[...]

    for c in range(n_chunks):
        rows = slice(c * C, (c + 1) * C)
        r = r_ref[rows, :]
        lw = lw_ref[rows, :]
        k = k_ref[rows, :]
        v = v_ref[rows, :]
        kn = kn_ref[rows, :]
        b = b_ref[rows, :]

        cs = _dot_sel_rhs(tri, lw)
        sh = cs[C // 2 - 1:C // 2, :]
        cs_end = cs[C - 1:C, :]
        e_pos = jnp.exp(cs - sh)
        e_neg = jnp.exp(sh - cs)
        e_prev = jnp.exp(cs - lw - sh)
        e_end = jnp.exp(cs_end - cs)

        Rs = _bf(stack(r * e_pos))
        As = _bf(stack(-kn * e_prev))
        Ks = _bf(stack(k * e_neg))
        Bs = _bf(stack(b * e_neg))
        Rt = stack(r * jnp.exp(cs))
        At = _bf(stack(-kn * jnp.exp(cs - lw)))
        Be = _bf(stack(b * e_end))
        Ke = _bf(stack(k * e_end))
        Vs = _bf(stack(v))

        Aab = _dot_nt(As, Bs) * smask
        Aak = _dot_nt(As, Ks) * smask
        RK = _dot_nt(Rs, Ks) * imask
        RB = _dot_nt(Rs, Bs) * imask

        T = eye + Aab
        P = Aab
        for _ in range(int(math.log2(C)) - 1):
            Pb = _bf(P)
            P = _dot(Pb, Pb)
            T = T + _dot(_bf(P), _bf(T))
        Tb = _bf(T)

        W1 = _dot(_bf(Aak), Vs)
        TA = _dot(Tb, At)
        U0 = _dot(Tb, _bf(W1))
        TAb = _bf(TA)
        U0b = _bf(U0)
        RBb = _bf(RB)
        Q = Rt + _dot(RBb, TAb)
        Y0 = _dot(_bf(RK), Vs) + _dot(RBb, U0b)
        Mt = _dot_tn(TAb, Be)
        Nt = _dot_tn(Vs, Ke) + _dot_tn(U0b, Be)

        S0 = s_ref[...]
        S0b = _bf(S0)
        Y = _dot_nt(_bf(Q), S0b) + Y0
        s_ref[...] = S0 * jnp.exp(cs_end) + _dot(S0b, _bf(Mt)) + Nt

        y = Y[:C, :] + Y[C:, :]
        mean = _dot_sel_lhs(y, bd_mean, parts=2)
        yc = y - mean
        var = _dot_sel_lhs(yc * yc, bd_mean, parts=2)
        yn = yc * lax.rsqrt(var + RWKV_GN_EPS) * gng + gnb
        bonus = _dot_sel_lhs(r * k * rk, bd_one, parts=2) * v
        o_ref[rows, :] = (yn + bonus) * g_ref[rows, :]


def _rwkv_chunk(r, lw, k, v, kn, b, g, gn_g, gn_b, r_k, B, S, n_chunks):
    n = B * S
    rows = n_chunks * RWKV_CHUNK
    ns = S // rows
    npair = RWKV_DIM // LANES
    blk = pl.BlockSpec((rows, LANES), lambda bb, p, s: (bb * ns + s, p))
    par = pl.BlockSpec((1, LANES), lambda bb, p, s: (0, p))
    return pl.pallas_call(
        functools.partial(_rwkv_chunk_kernel, n_chunks=n_chunks),
        grid=(B, npair, ns),
        in_specs=[blk] * 7 + [par] * 3,
        out_specs=blk,
        out_shape=jax.ShapeDtypeStruct((n, RWKV_DIM), F32),
        scratch_shapes=[pltpu.VMEM((LANES, LANES), F32)],
        compiler_params=_params("parallel", "parallel", "arbitrary"),
        name="rwkv_chunk",
    )(r, lw, k, v, kn, b, g, gn_g, gn_b, r_k)


def _fox_c_kernel(ps_ref, pst_ref, bl_ref, bc_ref, cc_ref, cr_ref, *, seq):
    L = LANES
    ri = _iota2((L, L), 0)
    ci = _iota2((L, L), 1)
    tril = _bf(jnp.where(ri >= ci, 1.0, 0.0))
    triu = _bf(jnp.where(ri <= ci, 1.0, 0.0))
    carry = jnp.zeros((1, L), F32)
    carry_c = jnp.zeros((SMALL_ROWS, 1), F32)
    for i in range(seq // L):
        rows = slice(i * L, (i + 1) * L)
        cs = _dot_sel_rhs(tril, _log_sigmoid(ps_ref[rows, :] + bl_ref[...])) + carry
        cc_ref[rows, :] = cs
        carry = cs[L - 1:L, :]
        cst = _dot_sel_lhs(_log_sigmoid(pst_ref[:, rows] + bc_ref[...]), triu) + carry_c
        cr_ref[:, rows] = cst
        carry_c = cst[:, L - 1:L]


def _fox_c(ps, pst, bias_lane, bias_col, B, S):
    n = B * S
    return pl.pallas_call(
        functools.partial(_fox_c_kernel, seq=S),
        grid=(B,),
        in_specs=[pl.BlockSpec((S, LANES), lambda b: (b, 0)),
                  pl.BlockSpec((SMALL_ROWS, S), lambda b: (0, b)),
                  pl.BlockSpec((1, LANES), lambda b: (0, 0)),
                  pl.BlockSpec((SMALL_ROWS, 1), lambda b: (0, 0))],
        out_specs=[pl.BlockSpec((S, LANES), lambda b: (b, 0)),
                   pl.BlockSpec((SMALL_ROWS, S), lambda b: (0, b))],
        out_shape=[jax.ShapeDtypeStruct((n, LANES), F32),
                   jax.ShapeDtypeStruct((SMALL_ROWS, n), F32)],
        compiler_params=_params("parallel"),
        name="fox_cumgate",
    )(ps, pst, bias_lane, bias_col)


def _fox_attn_kernel(q_ref, k_ref, v_ref, cc_ref, cr_ref, o_ref, *, tq):
    pair = pl.program_id(1)
    qi = pl.program_id(2)
    lane = _iota2((1, LANES), 1)
    in0 = lane < HEAD
    q = q_ref[...] * (HEAD ** -0.5)
    qs = (_bf(jnp.where(in0, q, 0.0)), _bf(jnp.where(in0, 0.0, q)))
    ri = _iota2((LANES, LANES), 0)
    ci = _iota2((LANES, LANES), 1)
    sel = _bf(jnp.where(ri == PAIR * pair + (ci >= HEAD).astype(jnp.int32), 1.0, 0.0))
    cq_full = _dot_sel_lhs(cc_ref[...], sel)
    cqs = (cq_full[:, 0:1], cq_full[:, HEAD:HEAD + 1])
    causal = _iota2((tq, tq), 1) <= _iota2((tq, tq), 0)

    def step(kj, carry, diagonal):
        start = pl.multiple_of(kj * tq, tq)
        kb = _bf(k_ref[pl.ds(start, tq), :])
        vb = _bf(v_ref[pl.ds(start, tq), :])
        new = []
        for h in range(PAIR):
            m, l, acc = carry[h]
            ck = cr_ref[pl.ds(PAIR * pair + h, 1), pl.ds(start, tq)]
            s = _dot_nt(qs[h], kb) + cqs[h] - ck
            if diagonal:
                s = jnp.where(causal, s, -jnp.inf)
            m_new = jnp.maximum(m, jnp.max(s, axis=-1, keepdims=True))
            p = jnp.exp(s - m_new)
            alpha = jnp.exp(m - m_new)
            l = alpha * l + jnp.sum(p, axis=-1, keepdims=True)
            acc = alpha * acc + _dot(_bf(p), vb)
            new.append((m_new, l, acc))
        return tuple(new)

    init = tuple((jnp.full((tq, 1), -jnp.inf, F32), jnp.zeros((tq, 1), F32),
                  jnp.zeros((tq, LANES), F32)) for _ in range(PAIR))
    carry = lax.fori_loop(0, qi, lambda kj, cr: step(kj, cr, False), init)
    carry = step(qi, carry, True)
    (_, l0, a0), (_, l1, a1) = carry
    o_ref[...] = jnp.where(in0, a0 / l0, a1 / l1)


def _fox_attn(qkv, cc, cr, B, S, tq):
    n = B * S
    nq = S // tq
    npair = FOX_DIM // LANES
    return pl.pallas_call(
        functools.partial(_fox_attn_kernel, tq=tq),
        grid=(B, npair, nq),
        in_specs=[pl.BlockSpec((tq, LANES), lambda b, p, i: (b * nq + i, p)),
                  pl.BlockSpec((S, LANES), lambda b, p, i: (b, npair + p)),
                  pl.BlockSpec((S, LANES), lambda b, p, i: (b, 2 * npair + p)),
                  pl.BlockSpec((tq, LANES), lambda b, p, i: (b * nq + i, 0)),
                  pl.BlockSpec((SMALL_ROWS, S), lambda b, p, i: (0, b))],
        out_specs=pl.BlockSpec((tq, LANES), lambda b, p, i: (b * nq + i, p)),
        out_shape=jax.ShapeDtypeStruct((n, FOX_DIM), F32),
        compiler_params=_params("parallel", "parallel", "parallel"),
        name="fox_attn",
    )(qkv, qkv, qkv, cc, cr)


def _mamba_kernel(z_ref, xbc_ref, ps_ref, pst_ref, cw_ref, cb_ref, dtbl_ref, dtbc_ref, al_ref, ac_ref,
                  dl_ref, ng_ref, ex_ref, o_ref, halo_ref, st_ref):
    L = SSD_CHUNK

    @pl.when(pl.program_id(1) == 0)
    def _():
        halo_ref[...] = jnp.zeros_like(halo_ref)
        st_ref[...] = jnp.zeros_like(st_ref)

    xbc = xbc_ref[...]
    halo = halo_ref[...]
    row8 = _iota2((SUBLANES, 1), 0)
    acc = cb_ref[...] + cw_ref[CONV_WIDTH - 1:CONV_WIDTH, :] * xbc
    for sft in range(1, CONV_WIDTH):
        rolled = pltpu.roll(xbc, sft, axis=0)
        top = jnp.where(row8 < sft, pltpu.roll(halo, sft, axis=0), rolled[:SUBLANES, :])
        shifted = jnp.concatenate([top, rolled[SUBLANES:, :]], axis=0)
        acc = acc + cw_ref[CONV_WIDTH - 1 - sft:CONV_WIDTH - sft, :] * shifted
    halo_ref[...] = xbc[L - SUBLANES:, :]
    act = acc * _sigmoid(acc)
    xs = act[:, :SSM_DIM]
    Bm = act[:, SSM_DIM:SSM_DIM + SSM_GROUPS * SSM_STATE]
    Cm = act[:, SSM_DIM + SSM_GROUPS * SSM_STATE:]

    ri = _iota2((L, L), 0)
    ci = _iota2((L, L), 1)
    lower = ri >= ci
    tril = _bf(jnp.where(lower, 1.0, 0.0))
    triu = _bf(jnp.where(ri <= ci, 1.0, 0.0))
    lane = _iota2((1, LANES), 1)
    in0 = lane < HEAD

    dt_full = _dot_sel_lhs(_softplus(ps_ref[...] + dtbl_ref[...]), ex_ref[...])
    cs = _dot_sel_rhs(tril, dt_full * al_ref[...])
    cs_end = cs[L - 1:L, :]
    ecs = jnp.exp(cs)
    X = xs * dt_full
    Xe = X * jnp.exp(cs_end - cs)
    csr = _dot_sel_lhs(_softplus(pst_ref[...] + dtbc_ref[...]) * ac_ref[...], triu)

    heads_per_group = SSM_HEADS // SSM_GROUPS
    pairs_per_group = heads_per_group // PAIR
    ys = []
    for g in range(SSM_GROUPS):
        Bg = Bm[:, g * SSM_STATE:(g + 1) * SSM_STATE]
        Cg = _bf(Cm[:, g * SSM_STATE:(g + 1) * SSM_STATE])
        CB = _dot_nt(Cg, _bf(Bg))
        BgT = _bf(Bg.T)
        for pp in range(pairs_per_group):
            p = g * pairs_per_group + pp
            cols = slice(p * LANES, (p + 1) * LANES)
            Xp = X[:, cols]
            ydiag = None
            for hh in range(PAIR):
                h = PAIR * p + hh
                decay = jnp.exp(cs[:, h * HEAD:h * HEAD + 1] - csr[DT_OFF + h:DT_OFF + h + 1, :])
                Mh = _bf(CB * jnp.where(lower, decay, 0.0))
                Xh = _bf(jnp.where(in0, Xp, 0.0) if hh == 0 else jnp.where(in0, 0.0, Xp))
                t = _dot(Mh, Xh)
                ydiag = t if ydiag is None else ydiag + t
            st = st_ref[p]
            yoff = _dot(Cg, _bf(st)) * ecs[:, cols]
            st_ref[p] = st * ecs[L - 1:L, cols] + _dot(BgT, _bf(Xe[:, cols]))
            ys.append(ydiag + yoff + dl_ref[:, cols] * xs[:, cols])
    z = z_ref[...]
    y = jnp.concatenate(ys, axis=1) * (z * _sigmoid(z))
    gw = SSM_DIM // SSM_GROUPS
    outs = []
    for g in range(SSM_GROUPS):
        yg = y[:, g * gw:(g + 1) * gw]
        ms = jnp.mean(yg * yg, axis=-1, keepdims=True)
        outs.append(yg * lax.rsqrt(ms + NORM_EPS))
    o_ref[...] = jnp.concatenate(outs, axis=1) * ng_ref[...]


def _mamba(z, xbc, ps, pst, cw, cb, dtbl, dtbc, al, ac, dl, ng, ex, B, S):
    n = B * S
    L = SSD_CHUNK
    nc = S // L
    row = lambda w: pl.BlockSpec((L, w), lambda b, c: (b * nc + c, 0))
    full = lambda a: pl.BlockSpec(a.shape, lambda b, c: (0,) * a.ndim)
    return pl.pallas_call(
        _mamba_kernel,
        grid=(B, nc),
        in_specs=[row(SSM_DIM), row(SSM_CONV_DIM), row(LANES),
                  pl.BlockSpec((SMALL_ROWS, L), lambda b, c: (0, b * nc + c))]
                 + [full(a) for a in (cw, cb, dtbl, dtbc, al, ac, dl, ng, ex)],
        out_specs=row(SSM_DIM),
        out_shape=jax.ShapeDtypeStruct((n, SSM_DIM), F32),
        scratch_shapes=[pltpu.VMEM((SUBLANES, SSM_CONV_DIM), F32),
                        pltpu.VMEM((SSM_HEADS // PAIR, SSM_STATE, LANES), F32)],
        compiler_params=_params("parallel", "arbitrary"),
        name="mamba_ssd",
    )(z, xbc, ps, pst, cw, cb, dtbl, dtbc, al, ac, dl, ng, ex)


def _layer_norm(h, g, b):
    mu = jnp.mean(h, axis=-1, keepdims=True)
    hc = h - mu
    var = jnp.mean(hc * hc, axis=-1, keepdims=True)
    return hc * lax.rsqrt(var + LN_EPS) * g + b


def _merge_kernel(or_ref, of_ref, om_ref, pg_ref, x_ref, wr_ref, wf_ref, wm_ref, wo_ref, gb_ref,
                  lg_ref, lb_ref, rwh_ref, rwl_ref, x1_ref, x1b_ref, lgt_ref, *, alpha):
    D = D_MODEL
    gate = lambda i: _sigmoid(pg_ref[:, i * D:(i + 1) * D] + gb_ref[:, i * D:(i + 1) * D])
    merged = (gate(0) * _dot(_bf(or_ref[...]), wr_ref[...])
              + gate(1) * _dot(_bf(of_ref[...]), wf_ref[...])
              + gate(2) * _dot(_bf(om_ref[...]), wm_ref[...]))
    mix = _dot(_bf(merged), wo_ref[...])
    x1 = _layer_norm(alpha * x_ref[...] + mix, lg_ref[...], lb_ref[...])
    x1_ref[...] = x1
    xh = _bf(x1)
    x1b_ref[...] = xh
    xl = _bf(x1 - xh.astype(F32))
    lgt_ref[...] = _dot(xh, rwh_ref[...]) + _dot(xl, rwh_ref[...]) + _dot(xh, rwl_ref[...])


def _merge(o_r, o_f, o_m, pg, x, wr, wf, wm, wo, gb, lg, lb, rwh, rwl, tm, alpha):
    n = x.shape[0]
    row = lambda w: pl.BlockSpec((tm, w), lambda i: (i, 0))
    full = lambda a: pl.BlockSpec(a.shape, lambda i: (0,) * a.ndim)
    return pl.pallas_call(
        functools.partial(_merge_kernel, alpha=alpha),
        grid=(n // tm,),
        in_specs=[row(RWKV_DIM), row(FOX_DIM), row(SSM_DIM), row(3 * D_MODEL), row(D_MODEL)]
                 + [full(a) for a in (wr, wf, wm, wo, gb, lg, lb, rwh, rwl)],
        out_specs=[row(D_MODEL), row(D_MODEL), row(LANES)],
        out_shape=[jax.ShapeDtypeStruct((n, D_MODEL), F32),
                   jax.ShapeDtypeStruct((n, D_MODEL), BF16),
                   jax.ShapeDtypeStruct((n, LANES), F32)],
        compiler_params=_params("parallel"),
        name="merge_ln1_router",
    )(o_r, o_f, o_m, pg, x, wr, wf, wm, wo, gb, lg, lb, rwh, rwl)


def _moe_kernel(be_ref, nu_ref, xb_ref, wg_ref, wu_ref, wd_ref, yb_ref, wgb, wub, wdb):
    i = pl.program_id(0)
    new_expert = jnp.logical_or(i == 0, be_ref[i] != be_ref[jnp.maximum(i - 1, 0)])

    @pl.when(new_expert)
    def _():
        wgb[...] = _bf(wg_ref[0])
        wub[...] = _bf(wu_ref[0])
        wdb[...] = _bf(wd_ref[0])

    @pl.when(i < nu_ref[0])
    def _():
        x = xb_ref[...]
        gt = _dot(x, wgb[...])
        h = gt * _sigmoid(gt) * _dot(x, wub[...])
        yb_ref[...] = _dot(_bf(h), wdb[...])

    @pl.when(i >= nu_ref[0])
    def _():
        yb_ref[...] = jnp.zeros_like(yb_ref)


def _moe(block_expert, n_used, xb, wg, wu, wd):
    rows, d = xb.shape
    blk = MOE_BLOCK
    nb = rows // blk
    grid_spec = pltpu.PrefetchScalarGridSpec(
        num_scalar_prefetch=2,
        grid=(nb,),
        in_specs=[pl.BlockSpec((blk, d), lambda i, be, nu: (i, 0)),
                  pl.BlockSpec((1, d, D_EXPERT), lambda i, be, nu: (be[i], 0, 0)),
                  pl.BlockSpec((1, d, D_EXPERT), lambda i, be, nu: (be[i], 0, 0)),
                  pl.BlockSpec((1, D_EXPERT, d), lambda i, be, nu: (be[i], 0, 0))],
        out_specs=pl.BlockSpec((blk, d), lambda i, be, nu: (i, 0)),
        scratch_shapes=[pltpu.VMEM((d, D_EXPERT), BF16), pltpu.VMEM((d, D_EXPERT), BF16),
                        pltpu.VMEM((D_EXPERT, d), BF16)])
    return pl.pallas_call(
        _moe_kernel,
        grid_spec=grid_spec,
        out_shape=jax.ShapeDtypeStruct((rows, d), F32),
        compiler_params=_params("arbitrary"),
        name="moe_experts",
    )(block_expert, n_used, xb, wg, wu, wd)


def _ln2_kernel(x_ref, f_ref, g_ref, b_ref, o_ref, ob_ref, *, alpha):
    y = _layer_norm(alpha * x_ref[...] + f_ref[...], g_ref[...], b_ref[...])
    o_ref[...] = y
    ob_ref[...] = _bf(y)


def _ln2(x, ffn, g, b, tm, alpha):
    n, d = x.shape
    row = pl.BlockSpec((tm, d), lambda i: (i, 0))
    par = pl.BlockSpec((1, d), lambda i: (0, 0))
    return pl.pallas_call(
        functools.partial(_ln2_kernel, alpha=alpha),
        grid=(n // tm,),
        in_specs=[row, row, par, par],
        out_specs=[row, row],
        out_shape=[jax.ShapeDtypeStruct((n, d), F32), jax.ShapeDtypeStruct((n, d), BF16)],
        compiler_params=_params("parallel"),
        name="ln2",
    )(x, ffn, g, b)


def _route(logits, router_bias):
    n = logits.shape[0]
    aff = jax.nn.sigmoid(logits[:, :N_EXPERTS])
    sel = aff + router_bias.astype(F32)
    grp_score = lax.top_k(sel.reshape(n, N_EXPERT_GROUPS, EXPERTS_PER_GROUP), TOP_K)[0].sum(-1)
    top_grp = jnp.argmax(grp_score, axis=-1)
    in_grp = (jnp.arange(N_EXPERTS) // EXPERTS_PER_GROUP)[None, :] == top_grp[:, None]
    _, idx = lax.top_k(jnp.where(in_grp, sel, -jnp.inf), TOP_K)
    w_sel = jnp.take_along_axis(aff, idx, axis=-1)
    gate = w_sel / jnp.sum(w_sel, axis=-1, keepdims=True)

    m = n * TOP_K
    blk = MOE_BLOCK
    flat_e = idx.reshape(m)
    flat_tok = jnp.broadcast_to(jnp.arange(n, dtype=jnp.int32)[:, None], (n, TOP_K)).reshape(m)
    order = jnp.argsort(flat_e)
    e_sorted, tok_sorted = flat_e[order], flat_tok[order]
    counts = jnp.bincount(flat_e, length=N_EXPERTS)
    padded = (counts + blk - 1) // blk * blk
    pad_end = jnp.cumsum(padded)
    pad_start = pad_end - padded
    start = jnp.cumsum(counts) - counts
    dest_sorted = (pad_start[e_sorted] + jnp.arange(m) - start[e_sorted]).astype(jnp.int32)
    n_blocks = -(-m // blk) + N_EXPERTS
    buf_tok = jnp.zeros((n_blocks * blk,), jnp.int32).at[dest_sorted].set(tok_sorted)
    block_expert = jnp.minimum(
        jnp.searchsorted(pad_end, jnp.arange(n_blocks) * blk, side='right'), N_EXPERTS - 1).astype(jnp.int32)
    dest = jnp.zeros((m,), jnp.int32).at[order].set(dest_sorted).reshape(n, TOP_K)
    n_used = (pad_end[-1] // blk).astype(jnp.int32).reshape(1)
    return gate, buf_tok, block_expert, dest, n_used


def _tiles(B, S):
    n = B * S
    return dict(
        tm_proj=min(1024, n),
        tn_proj=256,
        ts_prep=min(512, S),
        rwkv_chunks=min(2, S // RWKV_CHUNK),
        tq=min(256, S),
        tm_merge=min(512, n),
        tm_ln=min(1024, n),
    )


def _pad_rows(a, before, total):
    return jnp.zeros((total, a.shape[1]), a.dtype).at[before:before + a.shape[0]].set(a)


def _block_diag_ones(dim):
    i = jnp.arange(dim) // HEAD
    return (i[:, None] == i[None, :]).astype(BF16)


def kernel(x, w_in, rwkv_mu, rwkv_w0, rwkv_w_up, rwkv_a0, rwkv_a_up, rwkv_g_up, rwkv_k_k, rwkv_k_a,
           rwkv_r_k, rwkv_gn_g, rwkv_gn_b, fox_f_bias, ssm_conv_w, ssm_conv_b, ssm_dt_bias, ssm_a_log,
           ssm_d, ssm_norm_g, gate_bias, w_br_rwkv, w_br_fox, w_br_ssm, w_out, ln1_g, ln1_b, router_w,
           router_bias, exp_w_gate, exp_w_up, exp_w_down, ln2_g, ln2_b):
    B, S, D = x.shape
    depth = w_in.shape[0]
    n = B * S
    t = _tiles(B, S)
    alpha = (2 * depth) ** 0.25

    row1 = lambda a: a.reshape(1, -1).astype(F32)
    bd512 = _block_diag_ones(RWKV_DIM)
    expand = (jnp.arange(LANES)[:, None] == DT_OFF + jnp.arange(SSM_DIM)[None, :] // HEAD).astype(BF16)
    rw = jnp.zeros((D, LANES), F32).at[:, :N_EXPERTS].set(router_w.astype(F32))
    rwh = rw.astype(BF16)
    rwl = (rw - rwh.astype(F32)).astype(BF16)

    xf = x.reshape(n, D).astype(F32)
    xb16 = xf.astype(BF16)
    for l in range(depth):
        w = w_in[l]
        w_small = jnp.concatenate([w[:, OFF_SSM - FOX_HEADS:OFF_SSM], w[:, OFF_GATE - SSM_HEADS:OFF_GATE]], axis=1)
        w_small_l = jnp.zeros((D, LANES), F32).at[:, :w_small.shape[1]].set(w_small).astype(BF16)
        w_small_t = jnp.zeros((SMALL_ROWS, D), F32).at[:w_small.shape[1]].set(w_small.T).astype(BF16)
        mm = lambda lo, hi: _matmul(xb16, w[:, lo:hi].astype(BF16), t["tm_proj"], t["tn_proj"])
        p_rwkv = mm(0, RWKV_COLS)
        p_fox = mm(OFF_FOX, OFF_FOX + 3 * FOX_DIM)
        p_z = mm(OFF_SSM, OFF_SSM + SSM_DIM)
        p_xbc = mm(OFF_SSM + SSM_DIM, OFF_SSM + SSM_DIM + SSM_CONV_DIM)
        p_gate = mm(OFF_GATE, OFF_GATE + 3 * D)
        ps = _matmul(xb16, w_small_l, t["tm_proj"], LANES)
        pst = _matmul_nt(w_small_t, xb16, t["tm_proj"])

        wup = _pad_rows(rwkv_w_up[l], 0, DECAY_LORA + ICLR_LORA).astype(BF16)
        aup = _pad_rows(rwkv_a_up[l], DECAY_LORA, DECAY_LORA + ICLR_LORA).astype(BF16)
        r, lw, k, v, kn, b, g = _rwkv_prep(
            p_rwkv, B, S, t["ts_prep"], row1(rwkv_mu[l]), row1(rwkv_w0[l]), wup, row1(rwkv_a0[l]), aup,
            rwkv_g_up[l].astype(BF16), row1(rwkv_k_k[l]), row1(rwkv_k_a[l]), bd512)
        o_r = _rwkv_chunk(r, lw, k, v, kn, b, g, row1(rwkv_gn_g[l]), row1(rwkv_gn_b[l]),
                          row1(rwkv_r_k[l]), B, S, t["rwkv_chunks"])

        fb_lane = jnp.zeros((1, LANES), F32).at[0, :FOX_HEADS].set(fox_f_bias[l].astype(F32))
        fb_col = jnp.zeros((SMALL_ROWS, 1), F32).at[:FOX_HEADS, 0].set(fox_f_bias[l].astype(F32))
        cc, cr = _fox_c(ps, pst, fb_lane, fb_col, B, S)
        o_f = _fox_attn(p_fox, cc, cr, B, S, t["tq"])

        dtb = ssm_dt_bias[l].astype(F32)
        a_neg = -jnp.exp(ssm_a_log[l].astype(F32))
        dtbl = jnp.zeros((1, LANES), F32).at[0, DT_OFF:DT_OFF + SSM_HEADS].set(dtb)
        dtbc = jnp.zeros((SMALL_ROWS, 1), F32).at[DT_OFF:DT_OFF + SSM_HEADS, 0].set(dtb)
        ac = jnp.zeros((SMALL_ROWS, 1), F32).at[DT_OFF:DT_OFF + SSM_HEADS, 0].set(a_neg)
        al = jnp.repeat(a_neg, HEAD).reshape(1, SSM_DIM)
        dl = jnp.repeat(ssm_d[l].astype(F32), HEAD).reshape(1, SSM_DIM)
        o_m = _mamba(p_z, p_xbc, ps, pst, ssm_conv_w[l].astype(F32), row1(ssm_conv_b[l]), dtbl, dtbc,
                     al, ac, dl, row1(ssm_norm_g[l]), expand, B, S)

        x1, x1b, logits = _merge(
            o_r, o_f, o_m, p_gate, xf, w_br_rwkv[l].astype(BF16), w_br_fox[l].astype(BF16),
            w_br_ssm[l].astype(BF16), w_out[l].astype(BF16), row1(gate_bias[l]), row1(ln1_g[l]),
            row1(ln1_b[l]), rwh, rwl, t["tm_merge"], alpha)

        gate, buf_tok, block_expert, dest, n_used = _route(logits, router_bias)
        yb = _moe(block_expert, n_used, x1b[buf_tok], exp_w_gate[l], exp_w_up[l], exp_w_down[l])
        ffn = jnp.sum(yb[dest] * gate[..., None], axis=1)
        xf, xb16 = _ln2(x1, ffn, row1(ln2_g[l]), row1(ln2_b[l]), t["tm_ln"], alpha)
    return xf.reshape(B, S, D).astype(x.dtype)
```

```python
import functools
import math

import jax
import jax.numpy as jnp
from jax import lax
from jax.experimental import pallas as pl
from jax.experimental.pallas import tpu as pltpu

F32 = jnp.float32
BF16 = jnp.bfloat16

LANES = 128
SUBLANES = 8
VMEM_LIMIT = 48 * 1024 * 1024

D_MODEL = 1024
HEAD = 64
PAIR = LANES // HEAD
RWKV_DIM = 512
DECAY_LORA = 64
ICLR_LORA = 64
GATE_LORA = 128
RWKV_COLS = 3 * RWKV_DIM + DECAY_LORA + ICLR_LORA + GATE_LORA
RWKV_GN_EPS = 64e-5
FOX_DIM = 512
FOX_HEADS = 8
SSM_DIM = 1024
SSM_HEADS = 16
SSM_GROUPS = 2
SSM_STATE = 128
CONV_WIDTH = 4
SSM_CONV_DIM = SSM_DIM + 2 * SSM_GROUPS * SSM_STATE
NORM_EPS = 1e-5
N_EXPERTS = 32
N_EXPERT_GROUPS = 4
EXPERTS_PER_GROUP = N_EXPERTS // N_EXPERT_GROUPS
TOP_K = 2
D_EXPERT = 512
LN_EPS = 1e-5
OFF_FOX = RWKV_COLS
OFF_SSM = OFF_FOX + 3 * FOX_DIM + FOX_HEADS
OFF_GATE = OFF_SSM + SSM_DIM + SSM_CONV_DIM + SSM_HEADS

SMALL_ROWS = 32
DT_OFF = 8
RWKV_CHUNK = 64
RWKV_GROUP = 8
SSD_CHUNK = 128
MOE_BLOCK = 256


def _bf(x):
    return x.astype(BF16)


def _dot(a, b):
    return jnp.dot(a, b, preferred_element_type=F32)


def _dot_nt(a, b):
    return lax.dot_general(a, b, (((1,), (1,)), ((), ())), preferred_element_type=F32)


def _dot_tn(a, b):
    return lax.dot_general(a, b, (((0,), (0,)), ((), ())), preferred_element_type=F32)


def _split(x, parts):
    out = []
    rem = x
    for _ in range(parts):
        hi = _bf(rem)
        out.append(hi)
        rem = rem - hi.astype(F32)
    return out


def _dot_sel_rhs(m01, x, parts=3):
    acc = None
    for part in _split(x, parts):
        t = _dot(m01, part)
        acc = t if acc is None else acc + t
    return acc


def _dot_sel_lhs(x, m01, parts=3):
    acc = None
    for part in _split(x, parts):
        t = _dot(part, m01)
        acc = t if acc is None else acc + t
    return acc


def _sigmoid(x):
    return 1.0 / (1.0 + jnp.exp(-x))


def _softplus(x):
    return jnp.maximum(x, 0.0) + jnp.log(1.0 + jnp.exp(-jnp.abs(x)))


def _log_sigmoid(x):
    return jnp.minimum(x, 0.0) - jnp.log(1.0 + jnp.exp(-jnp.abs(x)))


def _iota2(shape, dim):
    return lax.broadcasted_iota(jnp.int32, shape, dim)


def _params(*sem):
    return pltpu.CompilerParams(dimension_semantics=sem, vmem_limit_bytes=VMEM_LIMIT)


def _mm_kernel(x_ref, w_ref, o_ref):
    o_ref[...] = _dot(x_ref[...], w_ref[...]).astype(o_ref.dtype)


def _matmul(x, w, tm, tn, out_dtype=F32):
    n, k = x.shape
    nc = w.shape[1]
    return pl.pallas_call(
        _mm_kernel,
        grid=(n // tm, nc // tn),
        in_specs=[pl.BlockSpec((tm, k), lambda i, j: (i, 0)),
                  pl.BlockSpec((k, tn), lambda i, j: (0, j))],
        out_specs=pl.BlockSpec((tm, tn), lambda i, j: (i, j)),
        out_shape=jax.ShapeDtypeStruct((n, nc), out_dtype),
        compiler_params=_params("parallel", "parallel"),
        name="in_proj",
    )(x, w)


def _mm_nt_kernel(wt_ref, x_ref, o_ref):
    o_ref[...] = _dot_nt(wt_ref[...], x_ref[...])


def _matmul_nt(wt, x, tm):
    r, k = wt.shape
    n = x.shape[0]
    return pl.pallas_call(
        _mm_nt_kernel,
        grid=(n // tm,),
        in_specs=[pl.BlockSpec((r, k), lambda i: (0, 0)),
                  pl.BlockSpec((tm, k), lambda i: (i, 0))],
        out_specs=pl.BlockSpec((r, tm), lambda i: (0, i)),
        out_shape=jax.ShapeDtypeStruct((r, n), F32),
        compiler_params=_params("parallel"),
        name="in_proj_small_t",
    )(wt, x)


def _rwkv_prep_kernel(p_ref, mu_ref, w0_ref, wup_ref, a0_ref, aup_ref, gup_ref, kk_ref, ka_ref,
                      bd_ref, r_o, lw_o, k_o, v_o, kn_o, b_o, g_o, carry_ref):
    @pl.when(pl.program_id(1) == 0)
    def _():
        carry_ref[...] = jnp.zeros_like(carry_ref)

    p = p_ref[...].astype(F32)
    ts = p.shape[0]
    row = _iota2((ts, 1), 0)
    prev = jnp.where(row == 0, carry_ref[...], pltpu.roll(p, 1, axis=0))
    carry_ref[...] = p[ts - 1:ts, :]
    p = p + mu_ref[...] * (prev - p)

    R = RWKV_DIM
    r = p[:, 0:R]
    k = p[:, R:2 * R]
    v = p[:, 2 * R:3 * R]
    lora = p[:, 3 * R:3 * R + DECAY_LORA + ICLR_LORA]
    gd = p[:, 3 * R + DECAY_LORA + ICLR_LORA:]
    w_pre = w0_ref[...] + _dot(_bf(jnp.tanh(lora)), wup_ref[...])
    lw = -jnp.exp(_log_sigmoid(w_pre) - 0.5)
    a = _sigmoid(a0_ref[...] + _dot(_bf(lora), aup_ref[...]))
    g = _dot(_bf(_sigmoid(gd)), gup_ref[...])
    kk = k * kk_ref[...]
    ss = _dot_sel_lhs(kk * kk, bd_ref[...], parts=2)
    kn = kk / jnp.maximum(jnp.sqrt(ss), 1e-12)
    r_o[...] = _bf(r)
    lw_o[...] = lw
    k_o[...] = _bf(k * (1.0 + (a - 1.0) * ka_ref[...]))
    v_o[...] = _bf(v)
    kn_o[...] = _bf(kn)
    b_o[...] = _bf(kn * a)
    g_o[...] = _bf(g)


def _rwkv_prep(p, B, S, ts, mu, w0, wup, a0, aup, gup, k_k, k_a, bd):
    n = B * S
    ns = S // ts
    row_spec = lambda w: pl.BlockSpec((ts, w), lambda b, s: (b * ns + s, 0))
    full = lambda a: pl.BlockSpec(a.shape, lambda b, s: (0,) * a.ndim)
    out = lambda dt: jax.ShapeDtypeStruct((n, RWKV_DIM), dt)
    return pl.pallas_call(
        _rwkv_prep_kernel,
        grid=(B, ns),
        in_specs=[row_spec(RWKV_COLS)] + [full(a) for a in (mu, w0, wup, a0, aup, gup, k_k, k_a, bd)],
        out_specs=[row_spec(RWKV_DIM)] * 7,
        out_shape=[out(BF16), out(F32)] + [out(BF16)] * 5,
        scratch_shapes=[pltpu.VMEM((1, RWKV_COLS), F32)],
        compiler_params=_params("parallel", "arbitrary"),
        name="rwkv_prep",
    )(p, mu, w0, wup, a0, aup, gup, k_k, k_a, bd)


def _rwkv_chunk_kernel(r_ref, lw_ref, k_ref, v_ref, kn_ref, b_ref, g_ref, gng_ref, gnb_ref, rk_ref,
                       o_ref, s_ref, *, n_chunks):
    C = RWKV_CHUNK

    @pl.when(pl.program_id(2) == 0)
    def _():
        s_ref[...] = jnp.zeros_like(s_ref)

    lane = _iota2((1, LANES), 1)
    m0 = (lane < HEAD).astype(F32)
    m1 = 1.0 - m0
    ri = _iota2((LANES, LANES), 0)
    ci = _iota2((LANES, LANES), 1)
    same = (ri >= HEAD) == (ci >= HEAD)
    smask = jnp.where(same & (ri > ci), 1.0, 0.0)
    imask = jnp.where(same & (ri >= ci), 1.0, 0.0)
    eye = jnp.where(ri == ci, 1.0, 0.0)
    bd_mean = _bf(jnp.where(same, 1.0 / HEAD, 0.0))
    bd_one = _bf(jnp.where(same, 1.0, 0.0))
    tri = _bf(jnp.where(_iota2((C, C), 0) >= _iota2((C, C), 1), 1.0, 0.0))

    def stack(x):
        return jnp.concatenate([x * m0, x * m1], axis=0)

    gng = gng_ref[...]
    gnb = gnb_ref[...]
    rk = rk_ref[...]

    def group(chunks):
        ld = lambda ref, c: ref[c * C:(c + 1) * C, :]
        r = [ld(r_ref, c).astype(F32) for c in chunks]
        lw = [ld(lw_ref, c) for c in chunks]
        k = [ld(k_ref, c).astype(F32) for c in chunks]
        v = [ld(v_ref, c).astype(F32) for c in chunks]
        kn = [ld(kn_ref, c).astype(F32) for c in chunks]
        b = [ld(b_ref, c).astype(F32) for c in chunks]
        idx = range(len(chunks))

        cs = [_dot_sel_rhs(tri, lw[i]) for i in idx]
        Rs, As, Ks, Bs, Rt, At, Be, Ke, Vs, g_end = ([] for _ in range(10))
        for i in idx:
            sh = cs[i][C // 2 - 1:C // 2, :]
            cs_end = cs[i][C - 1:C, :]
            e_pos = jnp.exp(cs[i] - sh)
            e_neg = jnp.exp(sh - cs[i])
            e_prev = jnp.exp(cs[i] - lw[i] - sh)
            e_end = jnp.exp(cs_end - cs[i])
            Rs.append(_bf(stack(r[i] * e_pos)))
            As.append(_bf(stack(-kn[i] * e_prev)))
            Ks.append(_bf(stack(k[i] * e_neg)))
            Bs.append(_bf(stack(b[i] * e_neg)))
            Rt.append(stack(r[i] * jnp.exp(cs[i])))
            At.append(_bf(stack(-kn[i] * jnp.exp(cs[i] - lw[i]))))
            Be.append(_bf(stack(b[i] * e_end)))
            Ke.append(_bf(stack(k[i] * e_end)))
            Vs.append(_bf(stack(v[i])))
            g_end.append(jnp.exp(cs_end))

        Aab = [_dot_nt(As[i], Bs[i]) * smask for i in idx]
        Aak = [_dot_nt(As[i], Ks[i]) * smask for i in idx]
        RK = [_dot_nt(Rs[i], Ks[i]) * imask for i in idx]
        RBb = [_bf(_dot_nt(Rs[i], Bs[i]) * imask) for i in idx]

        T = [eye + Aab[i] for i in idx]
        P = Aab
        for _ in range(int(math.log2(C)) - 1):
            Pb = [_bf(P[i]) for i in idx]
            P = [_dot(Pb[i], Pb[i]) for i in idx]
            T = [T[i] + _dot(_bf(P[i]), _bf(T[i])) for i in idx]
        Tb = [_bf(T[i]) for i in idx]

        W1 = [_dot(_bf(Aak[i]), Vs[i]) for i in idx]
        TAb = [_bf(_dot(Tb[i], At[i])) for i in idx]
        U0b = [_bf(_dot(Tb[i], _bf(W1[i]))) for i in idx]
        Q = [_bf(Rt[i] + _dot(RBb[i], TAb[i])) for i in idx]
        Y0 = [_dot(_bf(RK[i]), Vs[i]) + _dot(RBb[i], U0b[i]) for i in idx]
        Mt = [_bf(_dot_tn(TAb[i], Be[i])) for i in idx]
        Nt = [_dot_tn(Vs[i], Ke[i]) + _dot_tn(U0b[i], Be[i]) for i in idx]

        Y = []
        S0 = s_ref[...]
        for i in idx:
            S0b = _bf(S0)
            Y.append(_dot_nt(Q[i], S0b) + Y0[i])
            S0 = S0 * g_end[i] + _dot(S0b, Mt[i]) + Nt[i]
        s_ref[...] = S0

        for i, c in enumerate(chunks):
            y = Y[i][:C, :] + Y[i][C:, :]
            mean = _dot_sel_lhs(y, bd_mean, parts=2)
            yc = y - mean
            var = _dot_sel_lhs(yc * yc, bd_mean, parts=2)
            yn = yc * lax.rsqrt(var + RWKV_GN_EPS) * gng + gnb
            bonus = _dot_sel_lhs(r[i] * k[i] * rk, bd_one, parts=2) * v[i]
            o_ref[c * C:(c + 1) * C, :] = _bf((yn + bonus) * ld(g_ref, c).astype(F32))

    for first in range(0, n_chunks, RWKV_GROUP):
        group(list(range(first, min(first + RWKV_GROUP, n_chunks))))


def _rwkv_chunk(r, lw, k, v, kn, b, g, gn_g, gn_b, r_k, B, S, n_chunks):
    n = B * S
    rows = n_chunks * RWKV_CHUNK
    ns = S // rows
    npair = RWKV_DIM // LANES
    blk = pl.BlockSpec((rows, LANES), lambda bb, p, s: (bb * ns + s, p))
    par = pl.BlockSpec((1, LANES), lambda bb, p, s: (0, p))
    return pl.pallas_call(
        functools.partial(_rwkv_chunk_kernel, n_chunks=n_chunks),
        grid=(B, npair, ns),
        in_specs=[blk] * 7 + [par] * 3,
        out_specs=blk,
        out_shape=jax.ShapeDtypeStruct((n, RWKV_DIM), BF16),
        scratch_shapes=[pltpu.VMEM((LANES, LANES), F32)],
        compiler_params=_params("parallel", "parallel", "arbitrary"),
        name="rwkv_chunk",
    )(r, lw, k, v, kn, b, g, gn_g, gn_b, r_k)


def _fox_c_kernel(ps_ref, pst_ref, bl_ref, bc_ref, cc_ref, cr_ref, *, seq):
    L = LANES
    ri = _iota2((L, L), 0)
    ci = _iota2((L, L), 1)
    tril = _bf(jnp.where(ri >= ci, 1.0, 0.0))
    triu = _bf(jnp.where(ri <= ci, 1.0, 0.0))
    carry = jnp.zeros((1, L), F32)
    carry_c = jnp.zeros((SMALL_ROWS, 1), F32)
    for i in range(seq // L):
        rows = slice(i * L, (i + 1) * L)
        cs = _dot_sel_rhs(tril, _log_sigmoid(ps_ref[rows, :] + bl_ref[...])) + carry
        cc_ref[rows, :] = cs
        carry = cs[L - 1:L, :]
        cst = _dot_sel_lhs(_log_sigmoid(pst_ref[:, rows] + bc_ref[...]), triu) + carry_c
        cr_ref[:, rows] = cst
        carry_c = cst[:, L - 1:L]


def _fox_c(ps, pst, bias_lane, bias_col, B, S):
    n = B * S
    return pl.pallas_call(
        functools.partial(_fox_c_kernel, seq=S),
        grid=(B,),
        in_specs=[pl.BlockSpec((S, LANES), lambda b: (b, 0)),
                  pl.BlockSpec((SMALL_ROWS, S), lambda b: (0, b)),
                  pl.BlockSpec((1, LANES), lambda b: (0, 0)),
                  pl.BlockSpec((SMALL_ROWS, 1), lambda b: (0, 0))],
        out_specs=[pl.BlockSpec((S, LANES), lambda b: (b, 0)),
                   pl.BlockSpec((SMALL_ROWS, S), lambda b: (0, b))],
        out_shape=[jax.ShapeDtypeStruct((n, LANES), F32),
                   jax.ShapeDtypeStruct((SMALL_ROWS, n), F32)],
        compiler_params=_params("parallel"),
        name="fox_cumgate",
    )(ps, pst, bias_lane, bias_col)


def _transpose_blocks(x):
    n = x.shape[0] // LANES
    return jnp.concatenate([x[i * LANES:(i + 1) * LANES, :].T for i in range(n)], axis=1)


def _fox_attn_kernel(q_ref, k_ref, v_ref, cc_ref, cr_ref, o_ref, vt_ref, ck_ref, *, tq, seq):
    pair = pl.program_id(1)
    qi = pl.program_id(2)

    @pl.when(qi == 0)
    def _():
        for i in range(seq // LANES):
            cols = slice(i * LANES, (i + 1) * LANES)
            vt_ref[:, cols] = _bf(v_ref[cols, :].astype(F32).T)
        ri = _iota2((LANES, LANES), 0)
        for h in range(PAIR):
            sel = _bf(jnp.where(ri == PAIR * pair + h, 1.0, 0.0))
            ck_ref[h] = _dot_sel_lhs(cc_ref[...], sel)

    qt = _transpose_blocks(q_ref[...].astype(F32) * (HEAD ** -0.5))
    in0 = _iota2((LANES, 1), 0) < HEAD
    qts = (_bf(jnp.where(in0, qt, 0.0)), _bf(jnp.where(in0, 0.0, qt)))
    q0 = pl.multiple_of(qi * tq, tq)
    cqs = tuple(cr_ref[pl.ds(PAIR * pair + h, 1), pl.ds(q0, tq)] for h in range(PAIR))
    causal = _iota2((tq, tq), 0) <= _iota2((tq, tq), 1)

    def step(kj, carry, diagonal):
        start = pl.multiple_of(kj * tq, tq)
        kb = k_ref[pl.ds(start, tq), :]
        new = []
        for h in range(PAIR):
            m, l, acc = carry[h]
            ck = ck_ref[h, pl.ds(start, tq), :]
            s = _dot(kb, qts[h]) + cqs[h] - jnp.concatenate([ck] * (tq // LANES), axis=1)
            if diagonal:
                s = jnp.where(causal, s, -jnp.inf)
            m_new = jnp.maximum(m, jnp.max(s, axis=0, keepdims=True))
            p = jnp.exp(s - m_new)
            alpha = jnp.exp(m - m_new)
            l = alpha * l + jnp.sum(p, axis=0, keepdims=True)
            vt = vt_ref[h * HEAD:(h + 1) * HEAD, pl.ds(start, tq)]
            acc = alpha * acc + _dot(vt, _bf(p))
            new.append((m_new, l, acc))
        return tuple(new)

    init = tuple((jnp.full((1, tq), -jnp.inf, F32), jnp.zeros((1, tq), F32),
                  jnp.zeros((HEAD, tq), F32)) for _ in range(PAIR))
    carry = lax.fori_loop(0, qi, lambda kj, cr: step(kj, cr, False), init)
    carry = step(qi, carry, True)
    ot = jnp.concatenate([acc / l for (_, l, acc) in carry], axis=0)
    o_ref[...] = jnp.concatenate([ot[:, i * LANES:(i + 1) * LANES].T for i in range(tq // LANES)],
                                 axis=0).astype(o_ref.dtype)


def _fox_attn(qkv, cc, cr, B, S, tq):
    n = B * S
    nq = S // tq
    npair = FOX_DIM // LANES
    return pl.pallas_call(
        functools.partial(_fox_attn_kernel, tq=tq, seq=S),
        grid=(B, npair, nq),
        in_specs=[pl.BlockSpec((tq, LANES), lambda b, p, i: (b * nq + i, p)),
                  pl.BlockSpec((S, LANES), lambda b, p, i: (b, npair + p)),
                  pl.BlockSpec((S, LANES), lambda b, p, i: (b, 2 * npair + p)),
                  pl.BlockSpec((S, LANES), lambda b, p, i: (b, 0)),
                  pl.BlockSpec((SMALL_ROWS, S), lambda b, p, i: (0, b))],
        out_specs=pl.BlockSpec((tq, LANES), lambda b, p, i: (b * nq + i, p)),
        out_shape=jax.ShapeDtypeStruct((n, FOX_DIM), BF16),
        scratch_shapes=[pltpu.VMEM((LANES, S), BF16), pltpu.VMEM((PAIR, S, LANES), F32)],
        compiler_params=_params("parallel", "parallel", "arbitrary"),
        name="fox_attn",
    )(qkv, qkv, qkv, cc, cr)


def _mamba_kernel(z_ref, xbc_ref, ps_ref, pst_ref, cw_ref, cb_ref, dtbl_ref, dtbc_ref, al_ref, ac_ref,
                  dl_ref, ng_ref, ex_ref, o_ref, halo_ref, st_ref):
    L = SSD_CHUNK

    @pl.when(pl.program_id(1) == 0)
    def _():
        halo_ref[...] = jnp.zeros_like(halo_ref)
        st_ref[...] = jnp.zeros_like(st_ref)

    xbc = xbc_ref[...].astype(F32)
    halo = halo_ref[...]
    row8 = _iota2((SUBLANES, 1), 0)
    acc = cb_ref[...] + cw_ref[CONV_WIDTH - 1:CONV_WIDTH, :] * xbc
    for sft in range(1, CONV_WIDTH):
        rolled = pltpu.roll(xbc, sft, axis=0)
        top = jnp.where(row8 < sft, pltpu.roll(halo, sft, axis=0), rolled[:SUBLANES, :])
        shifted = jnp.concatenate([top, rolled[SUBLANES:, :]], axis=0)
        acc = acc + cw_ref[CONV_WIDTH - 1 - sft:CONV_WIDTH - sft, :] * shifted
    halo_ref[...] = xbc[L - SUBLANES:, :]
    act = acc * _sigmoid(acc)
    xs = act[:, :SSM_DIM]
    Bm = act[:, SSM_DIM:SSM_DIM + SSM_GROUPS * SSM_STATE]
    Cm = act[:, SSM_DIM + SSM_GROUPS * SSM_STATE:]

    ri = _iota2((L, L), 0)
    ci = _iota2((L, L), 1)
    lower = ri >= ci
    tril = _bf(jnp.where(lower, 1.0, 0.0))
    triu = _bf(jnp.where(ri <= ci, 1.0, 0.0))
    lane = _iota2((1, LANES), 1)
    in0 = lane < HEAD

    dt_full = _dot_sel_lhs(_softplus(ps_ref[...] + dtbl_ref[...]), ex_ref[...])
    cs = _dot_sel_rhs(tril, dt_full * al_ref[...])
    cs_end = cs[L - 1:L, :]
    ecs = jnp.exp(cs)
    X = xs * dt_full
    Xe = X * jnp.exp(cs_end - cs)
    csr = _dot_sel_lhs(_softplus(pst_ref[...] + dtbc_ref[...]) * ac_ref[...], triu)

    heads_per_group = SSM_HEADS // SSM_GROUPS
    pairs_per_group = heads_per_group // PAIR
    ys = []
    for g in range(SSM_GROUPS):
        Bg = Bm[:, g * SSM_STATE:(g + 1) * SSM_STATE]
        Cg = _bf(Cm[:, g * SSM_STATE:(g + 1) * SSM_STATE])
        CB = _dot_nt(Cg, _bf(Bg))
        BgT = _bf(Bg.T)
        for pp in range(pairs_per_group):
            p = g * pairs_per_group + pp
            cols = slice(p * LANES, (p + 1) * LANES)
            Xp = X[:, cols]
            ydiag = None
            for hh in range(PAIR):
                h = PAIR * p + hh
                decay = jnp.exp(cs[:, h * HEAD:h * HEAD + 1] - csr[DT_OFF + h:DT_OFF + h + 1, :])
                Mh = _bf(CB * jnp.where(lower, decay, 0.0))
                Xh = _bf(jnp.where(in0, Xp, 0.0) if hh == 0 else jnp.where(in0, 0.0, Xp))
                t = _dot(Mh, Xh)
                ydiag = t if ydiag is None else ydiag + t
            st = st_ref[p]
            yoff = _dot(Cg, _bf(st)) * ecs[:, cols]
            st_ref[p] = st * ecs[L - 1:L, cols] + _dot(BgT, _bf(Xe[:, cols]))
            ys.append(ydiag + yoff + dl_ref[:, cols] * xs[:, cols])
    z = z_ref[...].astype(F32)
    y = jnp.concatenate(ys, axis=1) * (z * _sigmoid(z))
    gw = SSM_DIM // SSM_GROUPS
    outs = []
    for g in range(SSM_GROUPS):
        yg = y[:, g * gw:(g + 1) * gw]
        ms = jnp.mean(yg * yg, axis=-1, keepdims=True)
        outs.append(yg * lax.rsqrt(ms + NORM_EPS))
    o_ref[...] = _bf(jnp.concatenate(outs, axis=1) * ng_ref[...])


def _mamba(z, xbc, ps, pst, cw, cb, dtbl, dtbc, al, ac, dl, ng, ex, B, S):
    n = B * S
    L = SSD_CHUNK
    nc = S // L
    row = lambda w: pl.BlockSpec((L, w), lambda b, c: (b * nc + c, 0))
    full = lambda a: pl.BlockSpec(a.shape, lambda b, c: (0,) * a.ndim)
    return pl.pallas_call(
        _mamba_kernel,
        grid=(B, nc),
        in_specs=[row(SSM_DIM), row(SSM_CONV_DIM), row(LANES),
                  pl.BlockSpec((SMALL_ROWS, L), lambda b, c: (0, b * nc + c))]
                 + [full(a) for a in (cw, cb, dtbl, dtbc, al, ac, dl, ng, ex)],
        out_specs=row(SSM_DIM),
        out_shape=jax.ShapeDtypeStruct((n, SSM_DIM), BF16),
        scratch_shapes=[pltpu.VMEM((SUBLANES, SSM_CONV_DIM), F32),
                        pltpu.VMEM((SSM_HEADS // PAIR, SSM_STATE, LANES), F32)],
        compiler_params=_params("parallel", "arbitrary"),
        name="mamba_ssd",
    )(z, xbc, ps, pst, cw, cb, dtbl, dtbc, al, ac, dl, ng, ex)


def _layer_norm(h, g, b):
    mu = jnp.mean(h, axis=-1, keepdims=True)
    hc = h - mu
    var = jnp.mean(hc * hc, axis=-1, keepdims=True)
    return hc * lax.rsqrt(var + LN_EPS) * g + b


def _merge_kernel(or_ref, of_ref, om_ref, pg_ref, x_ref, wr_ref, wf_ref, wm_ref, wo_ref, gb_ref,
                  lg_ref, lb_ref, rwh_ref, rwl_ref, x1_ref, x1b_ref, lgt_ref, *, alpha):
    D = D_MODEL
    gate = lambda i: _sigmoid(pg_ref[:, i * D:(i + 1) * D].astype(F32) + gb_ref[:, i * D:(i + 1) * D])
    merged = (gate(0) * _dot(or_ref[...], wr_ref[...])
              + gate(1) * _dot(of_ref[...], wf_ref[...])
              + gate(2) * _dot(om_ref[...], wm_ref[...]))
    mix = _dot(_bf(merged), wo_ref[...])
    x1 = _layer_norm(alpha * x_ref[...] + mix, lg_ref[...], lb_ref[...])
    x1_ref[...] = x1
    xh = _bf(x1)
    x1b_ref[...] = xh
    xl = _bf(x1 - xh.astype(F32))
    lgt_ref[...] = (_dot_nt(rwh_ref[...], xh) + _dot_nt(rwh_ref[...], xl) + _dot_nt(rwl_ref[...], xh))


def _merge(o_r, o_f, o_m, pg, x, wr, wf, wm, wo, gb, lg, lb, rwh, rwl, tm, alpha):
    n = x.shape[0]
    row = lambda w: pl.BlockSpec((tm, w), lambda i: (i, 0))
    full = lambda a: pl.BlockSpec(a.shape, lambda i: (0,) * a.ndim)
    return pl.pallas_call(
        functools.partial(_merge_kernel, alpha=alpha),
        grid=(n // tm,),
        in_specs=[row(RWKV_DIM), row(FOX_DIM), row(SSM_DIM), row(3 * D_MODEL), row(D_MODEL)]
                 + [full(a) for a in (wr, wf, wm, wo, gb, lg, lb, rwh, rwl)],
        out_specs=[row(D_MODEL), row(D_MODEL), pl.BlockSpec((N_EXPERTS, tm), lambda i: (0, i))],
        out_shape=[jax.ShapeDtypeStruct((n, D_MODEL), F32),
                   jax.ShapeDtypeStruct((n, D_MODEL), BF16),
                   jax.ShapeDtypeStruct((N_EXPERTS, n), F32)],
        compiler_params=_params("parallel"),
        name="merge_ln1_router",
    )(o_r, o_f, o_m, pg, x, wr, wf, wm, wo, gb, lg, lb, rwh, rwl)


def _rows(*vals):
    t = vals[0].shape[1]
    sub = _iota2((SUBLANES, t), 0)
    out = jnp.zeros((SUBLANES, t), vals[0].dtype)
    for i, v in enumerate(vals):
        out = jnp.where(sub == i, v, out)
    return out


def _route_select_kernel(lgt_ref, bias_ref, e_ref, g_ref, rank_ref, cnt_ref, carry_ref):
    @pl.when(pl.program_id(0) == 0)
    def _():
        carry_ref[...] = jnp.zeros_like(carry_ref)

    t = lgt_ref.shape[1]
    aff = _sigmoid(lgt_ref[...])
    sel = aff + bias_ref[...]
    gsz = EXPERTS_PER_GROUP
    sub = _iota2((gsz, t), 0).astype(F32)
    best = e1 = e2 = None
    for g in range(N_EXPERT_GROUPS):
        s = sel[g * gsz:(g + 1) * gsz, :]
        m1 = jnp.max(s, axis=0, keepdims=True)
        i1 = jnp.min(jnp.where(s == m1, sub, float(gsz)), axis=0, keepdims=True)
        s2 = jnp.where(sub == i1, -jnp.inf, s)
        m2 = jnp.max(s2, axis=0, keepdims=True)
        i2 = jnp.min(jnp.where(s2 == m2, sub, float(gsz)), axis=0, keepdims=True)
        score = m1 + m2
        if g == 0:
            best, e1, e2 = score, i1, i2
        else:
            better = score > best
            best = jnp.where(better, score, best)
            e1 = jnp.where(better, i1 + float(g * gsz), e1)
            e2 = jnp.where(better, i2 + float(g * gsz), e2)
    row = _iota2((N_EXPERTS, t), 0).astype(F32)
    oh1 = jnp.where(row == e1, 1.0, 0.0)
    oh2 = jnp.where(row == e2, 1.0, 0.0)
    w1 = jnp.sum(oh1 * aff, axis=0, keepdims=True)
    w2 = jnp.sum(oh2 * aff, axis=0, keepdims=True)
    den = w1 + w2
    cnt = oh1 + oh2
    before = _bf(jnp.where(_iota2((t, t), 0) < _iota2((t, t), 1), 1.0, 0.0))
    prefix = _dot(_bf(cnt), before) + carry_ref[...]
    r1 = jnp.sum(oh1 * prefix, axis=0, keepdims=True)
    r2 = jnp.sum(oh2 * prefix, axis=0, keepdims=True)
    carry_ref[...] = carry_ref[...] + jnp.sum(cnt, axis=1, keepdims=True)
    e_ref[...] = _rows(e1, e2)
    g_ref[...] = _rows(w1 / den, w2 / den)
    rank_ref[...] = _rows(r1, r2)
    cnt_ref[...] = jnp.broadcast_to(carry_ref[...], cnt_ref.shape)


def _route_select(lgt, bias_col, tile):
    n = lgt.shape[1]
    col = lambda r: pl.BlockSpec((r, tile), lambda i: (0, i))
    out = jax.ShapeDtypeStruct((SUBLANES, n), F32)
    return pl.pallas_call(
        _route_select_kernel,
        grid=(n // tile,),
        in_specs=[col(N_EXPERTS), pl.BlockSpec((N_EXPERTS, 1), lambda i: (0, 0))],
        out_specs=[col(SUBLANES)] * 3 + [pl.BlockSpec((N_EXPERTS, LANES), lambda i: (0, 0))],
        out_shape=[out] * 3 + [jax.ShapeDtypeStruct((N_EXPERTS, LANES), F32)],
        scratch_shapes=[pltpu.VMEM((N_EXPERTS, 1), F32)],
        compiler_params=_params("arbitrary"),
        name="route_select",
    )(lgt, bias_col)


def _route_place_kernel(e_ref, rank_ref, cnt_ref, dest_ref, be_ref, nu_ref):
    t = e_ref.shape[1]
    blk = float(MOE_BLOCK)
    counts = cnt_ref[...]
    padded = jnp.floor((counts + (blk - 1.0)) / blk) * blk
    ne = N_EXPERTS
    lower = _bf(jnp.where(_iota2((ne, ne), 0) > _iota2((ne, ne), 1), 1.0, 0.0))
    pad_start = _dot_sel_rhs(lower, padded)
    pad_end = pad_start + padded
    row = _iota2((ne, t), 0).astype(F32)
    ps_t = jnp.concatenate([pad_start] * (t // LANES), axis=1)
    e = e_ref[...]
    rank = rank_ref[...]
    dests = []
    for k in range(TOP_K):
        oh = jnp.where(row == e[k:k + 1, :], 1.0, 0.0)
        dests.append(jnp.sum(oh * ps_t, axis=0, keepdims=True) + rank[k:k + 1, :])
    dest_ref[...] = _rows(*dests).astype(jnp.int32)
    nbp = be_ref.shape[1]
    blk_start = _iota2((ne, nbp), 1).astype(F32) * blk
    pe_t = jnp.concatenate([pad_end] * (nbp // LANES), axis=1)
    be = jnp.sum(jnp.where(pe_t <= blk_start, 1.0, 0.0), axis=0, keepdims=True)
    be_ref[...] = jnp.broadcast_to(jnp.minimum(be, float(ne - 1)), be_ref.shape).astype(jnp.int32)
    nu_ref[...] = jnp.broadcast_to(pad_end[ne - 1:ne, :] / blk, nu_ref.shape).astype(jnp.int32)


def _route_place(e, rank, cnt, tile, n_blocks):
    n = e.shape[1]
    nbp = -(-n_blocks // LANES) * LANES
    col = pl.BlockSpec((SUBLANES, tile), lambda i: (0, i))
    return pl.pallas_call(
        _route_place_kernel,
        grid=(n // tile,),
        in_specs=[col, col, pl.BlockSpec((N_EXPERTS, LANES), lambda i: (0, 0))],
        out_specs=[col, pl.BlockSpec((SUBLANES, nbp), lambda i: (0, 0)),
                   pl.BlockSpec((SUBLANES, LANES), lambda i: (0, 0))],
        out_shape=[jax.ShapeDtypeStruct((SUBLANES, n), jnp.int32),
                   jax.ShapeDtypeStruct((SUBLANES, nbp), jnp.int32),
                   jax.ShapeDtypeStruct((SUBLANES, LANES), jnp.int32)],
        compiler_params=_params("arbitrary"),
        name="route_place",
    )(e, rank, cnt)


def _moe_kernel(be_ref, nu_ref, xb_ref, wg_ref, wu_ref, wd_ref, yb_ref, wgb, wub, wdb):
    i = pl.program_id(0)
    new_expert = jnp.logical_or(i == 0, be_ref[i] != be_ref[jnp.maximum(i - 1, 0)])

    @pl.when(new_expert)
    def _():
        wgb[...] = _bf(wg_ref[0])
        wub[...] = _bf(wu_ref[0])
        wdb[...] = _bf(wd_ref[0])

    @pl.when(i < nu_ref[0])
    def _():
        x = xb_ref[...]
        gt = _dot(x, wgb[...])
        h = gt * _sigmoid(gt) * _dot(x, wub[...])
        yb_ref[...] = _dot(_bf(h), wdb[...])

    @pl.when(i >= nu_ref[0])
    def _():
        yb_ref[...] = jnp.zeros_like(yb_ref)


def _moe(block_expert, n_used, xb, wg, wu, wd):
    rows, d = xb.shape
    blk = MOE_BLOCK
    nb = rows // blk
    grid_spec = pltpu.PrefetchScalarGridSpec(
        num_scalar_prefetch=2,
        grid=(nb,),
        in_specs=[pl.BlockSpec((blk, d), lambda i, be, nu: (i, 0)),
                  pl.BlockSpec((1, d, D_EXPERT), lambda i, be, nu: (be[i], 0, 0)),
                  pl.BlockSpec((1, d, D_EXPERT), lambda i, be, nu: (be[i], 0, 0)),
                  pl.BlockSpec((1, D_EXPERT, d), lambda i, be, nu: (be[i], 0, 0))],
        out_specs=pl.BlockSpec((blk, d), lambda i, be, nu: (i, 0)),
        scratch_shapes=[pltpu.VMEM((d, D_EXPERT), BF16), pltpu.VMEM((d, D_EXPERT), BF16),
                        pltpu.VMEM((D_EXPERT, d), BF16)])
    return pl.pallas_call(
        _moe_kernel,
        grid_spec=grid_spec,
        out_shape=jax.ShapeDtypeStruct((rows, d), F32),
        compiler_params=_params("arbitrary"),
        name="moe_experts",
    )(block_expert, n_used, xb, wg, wu, wd)


def _ln2_kernel(x_ref, f_ref, g_ref, b_ref, o_ref, ob_ref, *, alpha):
    y = _layer_norm(alpha * x_ref[...] + f_ref[...], g_ref[...], b_ref[...])
    o_ref[...] = y
    ob_ref[...] = _bf(y)


def _ln2(x, ffn, g, b, tm, alpha):
    n, d = x.shape
    row = pl.BlockSpec((tm, d), lambda i: (i, 0))
    par = pl.BlockSpec((1, d), lambda i: (0, 0))
    return pl.pallas_call(
        functools.partial(_ln2_kernel, alpha=alpha),
        grid=(n // tm,),
        in_specs=[row, row, par, par],
        out_specs=[row, row],
        out_shape=[jax.ShapeDtypeStruct((n, d), F32), jax.ShapeDtypeStruct((n, d), BF16)],
        compiler_params=_params("parallel"),
        name="ln2",
    )(x, ffn, g, b)


def _route(lgt, bias_col, tile):
    n = lgt.shape[1]
    n_blocks = -(-(n * TOP_K) // MOE_BLOCK) + N_EXPERTS
    e, g, rank, cnt = _route_select(lgt, bias_col, tile)
    dest, be, nu = _route_place(e, rank, cnt, tile, n_blocks)
    return g[:TOP_K].T, dest[:TOP_K].T, be[0, :n_blocks], nu[0, :1], n_blocks


def _tiles(B, S):
    n = B * S
    return dict(
        tm_proj=min(1024, n),
        ts_prep=min(512, S),
        rwkv_chunks=min(8, S // RWKV_CHUNK),
        tq=min(256, S),
        tm_merge=min(512, n),
        t_route=min(512, n),
        tm_ln=min(1024, n),
    )


def _pad_rows(a, before, total):
    return jnp.zeros((total, a.shape[1]), a.dtype).at[before:before + a.shape[0]].set(a)


def _block_diag_ones(dim):
    i = jnp.arange(dim) // HEAD
    return (i[:, None] == i[None, :]).astype(BF16)


def kernel(x, w_in, rwkv_mu, rwkv_w0, rwkv_w_up, rwkv_a0, rwkv_a_up, rwkv_g_up, rwkv_k_k, rwkv_k_a,
           rwkv_r_k, rwkv_gn_g, rwkv_gn_b, fox_f_bias, ssm_conv_w, ssm_conv_b, ssm_dt_bias, ssm_a_log,
           ssm_d, ssm_norm_g, gate_bias, w_br_rwkv, w_br_fox, w_br_ssm, w_out, ln1_g, ln1_b, router_w,
           router_bias, exp_w_gate, exp_w_up, exp_w_down, ln2_g, ln2_b):
    B, S, D = x.shape
    depth = w_in.shape[0]
    n = B * S
    t = _tiles(B, S)
    alpha = (2 * depth) ** 0.25

    row1 = lambda a: a.reshape(1, -1).astype(F32)
    bd512 = _block_diag_ones(RWKV_DIM)
    expand = (jnp.arange(LANES)[:, None] == DT_OFF + jnp.arange(SSM_DIM)[None, :] // HEAD).astype(BF16)
    rw = router_w.astype(F32).T
    rwh = rw.astype(BF16)
    rwl = (rw - rwh.astype(F32)).astype(BF16)
    rb_col = router_bias.astype(F32).reshape(N_EXPERTS, 1)

    xf = x.reshape(n, D).astype(F32)
    xb16 = xf.astype(BF16)
    for l in range(depth):
        w = w_in[l]
        w_small = jnp.concatenate([w[:, OFF_SSM - FOX_HEADS:OFF_SSM], w[:, OFF_GATE - SSM_HEADS:OFF_GATE]], axis=1)
        w_small_l = jnp.zeros((D, LANES), F32).at[:, :w_small.shape[1]].set(w_small).astype(BF16)
        w_small_t = jnp.zeros((SMALL_ROWS, D), F32).at[:w_small.shape[1]].set(w_small.T).astype(BF16)
        mm = lambda lo, hi: _matmul(xb16, w[:, lo:hi].astype(BF16), t["tm_proj"],
                                    512 if (hi - lo) % 512 == 0 else 256, BF16)
        p_rwkv = mm(0, RWKV_COLS)
        p_fox = mm(OFF_FOX, OFF_FOX + 3 * FOX_DIM)
        p_z = mm(OFF_SSM, OFF_SSM + SSM_DIM)
        p_xbc = mm(OFF_SSM + SSM_DIM, OFF_SSM + SSM_DIM + SSM_CONV_DIM)
        p_gate = mm(OFF_GATE, OFF_GATE + 3 * D)
        ps = _matmul(xb16, w_small_l, t["tm_proj"], LANES)
        pst = _matmul_nt(w_small_t, xb16, t["tm_proj"])

        wup = _pad_rows(rwkv_w_up[l], 0, DECAY_LORA + ICLR_LORA).astype(BF16)
        aup = _pad_rows(rwkv_a_up[l], DECAY_LORA, DECAY_LORA + ICLR_LORA).astype(BF16)
        r, lw, k, v, kn, b, g = _rwkv_prep(
            p_rwkv, B, S, t["ts_prep"], row1(rwkv_mu[l]), row1(rwkv_w0[l]), wup, row1(rwkv_a0[l]), aup,
            rwkv_g_up[l].astype(BF16), row1(rwkv_k_k[l]), row1(rwkv_k_a[l]), bd512)
        o_r = _rwkv_chunk(r, lw, k, v, kn, b, g, row1(rwkv_gn_g[l]), row1(rwkv_gn_b[l]),
                          row1(rwkv_r_k[l]), B, S, t["rwkv_chunks"])

        fb_lane = jnp.zeros((1, LANES), F32).at[0, :FOX_HEADS].set(fox_f_bias[l].astype(F32))
        fb_col = jnp.zeros((SMALL_ROWS, 1), F32).at[:FOX_HEADS, 0].set(fox_f_bias[l].astype(F32))
        cc, cr = _fox_c(ps, pst, fb_lane, fb_col, B, S)
        o_f = _fox_attn(p_fox, cc, cr, B, S, t["tq"])

        dtb = ssm_dt_bias[l].astype(F32)
        a_neg = -jnp.exp(ssm_a_log[l].astype(F32))
        dtbl = jnp.zeros((1, LANES), F32).at[0, DT_OFF:DT_OFF + SSM_HEADS].set(dtb)
        dtbc = jnp.zeros((SMALL_ROWS, 1), F32).at[DT_OFF:DT_OFF + SSM_HEADS, 0].set(dtb)
        ac = jnp.zeros((SMALL_ROWS, 1), F32).at[DT_OFF:DT_OFF + SSM_HEADS, 0].set(a_neg)
        al = jnp.repeat(a_neg, HEAD).reshape(1, SSM_DIM)
        dl = jnp.repeat(ssm_d[l].astype(F32), HEAD).reshape(1, SSM_DIM)
        o_m = _mamba(p_z, p_xbc, ps, pst, ssm_conv_w[l].astype(F32), row1(ssm_conv_b[l]), dtbl, dtbc,
                     al, ac, dl, row1(ssm_norm_g[l]), expand, B, S)

        x1, x1b, logits = _merge(
            o_r, o_f, o_m, p_gate, xf, w_br_rwkv[l].astype(BF16), w_br_fox[l].astype(BF16),
            w_br_ssm[l].astype(BF16), w_out[l].astype(BF16), row1(gate_bias[l]), row1(ln1_g[l]),
            row1(ln1_b[l]), rwh, rwl, t["tm_merge"], alpha)

        gate, dest, block_expert, n_used, n_blocks = _route(logits, rb_col, t["t_route"])
        tok = jnp.broadcast_to(jnp.arange(n, dtype=jnp.int32)[:, None], (n, TOP_K))
        buf_tok = jnp.zeros((n_blocks * MOE_BLOCK,), jnp.int32).at[dest.reshape(-1)].set(tok.reshape(-1))
        yb = _moe(block_expert, n_used, x1b[buf_tok], exp_w_gate[l], exp_w_up[l], exp_w_down[l])
        ffn = jnp.sum(yb[dest] * gate[..., None], axis=1)
        xf, xb16 = _ln2(x1, ffn, row1(ln2_g[l]), row1(ln2_b[l]), t["tm_ln"], alpha)
    return xf.reshape(B, S, D).astype(x.dtype)
```

```python
import functools
import math

import jax
import jax.numpy as jnp
from jax import lax
from jax.experimental import pallas as pl
from jax.experimental.pallas import tpu as pltpu

F32 = jnp.float32
BF16 = jnp.bfloat16

LANES = 128
SUBLANES = 8
VMEM_LIMIT = 48 * 1024 * 1024

D_MODEL = 1024
HEAD = 64
PAIR = LANES // HEAD
RWKV_DIM = 512
DECAY_LORA = 64
ICLR_LORA = 64
GATE_LORA = 128
RWKV_COLS = 3 * RWKV_DIM + DECAY_LORA + ICLR_LORA + GATE_LORA
RWKV_GN_EPS = 64e-5
FOX_DIM = 512
FOX_HEADS = 8
SSM_DIM = 1024
SSM_HEADS = 16
SSM_GROUPS = 2
SSM_STATE = 128
CONV_WIDTH = 4
SSM_CONV_DIM = SSM_DIM + 2 * SSM_GROUPS * SSM_STATE
NORM_EPS = 1e-5
N_EXPERTS = 32
N_EXPERT_GROUPS = 4
EXPERTS_PER_GROUP = N_EXPERTS // N_EXPERT_GROUPS
TOP_K = 2
D_EXPERT = 512
LN_EPS = 1e-5
LOG2E = math.log2(math.e)
OFF_FOX = RWKV_COLS
OFF_SSM = OFF_FOX + 3 * FOX_DIM + FOX_HEADS
OFF_GATE = OFF_SSM + SSM_DIM + SSM_CONV_DIM + SSM_HEADS

SMALL_ROWS = 32
DT_OFF = 8
FOX_WIDTH = 256
RWKV_CHUNK = 64
RWKV_GROUP = 8
SSD_CHUNK = 128
MOE_BLOCK = 256


def _bf(x):
    return x.astype(BF16)


def _dot(a, b):
    return jnp.dot(a, b, preferred_element_type=F32)


def _dot_nt(a, b):
    return lax.dot_general(a, b, (((1,), (1,)), ((), ())), preferred_element_type=F32)


def _dot_tn(a, b):
    return lax.dot_general(a, b, (((0,), (0,)), ((), ())), preferred_element_type=F32)


def _split(x, parts):
    out = []
    rem = x
    for _ in range(parts):
        hi = _bf(rem)
        out.append(hi)
        rem = rem - hi.astype(F32)
    return out


def _dot_sel_rhs(m01, x, parts=3):
    acc = None
    for part in _split(x, parts):
        t = _dot(m01, part)
        acc = t if acc is None else acc + t
    return acc


def _dot_sel_lhs(x, m01, parts=3):
    acc = None
    for part in _split(x, parts):
        t = _dot(part, m01)
        acc = t if acc is None else acc + t
    return acc


def _sigmoid(x):
    return 1.0 / (1.0 + jnp.exp(-x))


def _softplus(x):
    return jnp.maximum(x, 0.0) + jnp.log(1.0 + jnp.exp(-jnp.abs(x)))


def _log_sigmoid(x):
    return jnp.minimum(x, 0.0) - jnp.log(1.0 + jnp.exp(-jnp.abs(x)))


def _iota2(shape, dim):
    return lax.broadcasted_iota(jnp.int32, shape, dim)


def _params(*sem):
    return pltpu.CompilerParams(dimension_semantics=sem, vmem_limit_bytes=VMEM_LIMIT)


def _mm_kernel(x_ref, w_ref, o_ref):
    o_ref[...] = _dot(x_ref[...], w_ref[...]).astype(o_ref.dtype)


def _matmul(x, w, tm, tn, out_dtype=F32):
    n, k = x.shape
    nc = w.shape[1]
    return pl.pallas_call(
        _mm_kernel,
        grid=(n // tm, nc // tn),
        in_specs=[pl.BlockSpec((tm, k), lambda i, j: (i, 0)),
                  pl.BlockSpec((k, tn), lambda i, j: (0, j))],
        out_specs=pl.BlockSpec((tm, tn), lambda i, j: (i, j)),
        out_shape=jax.ShapeDtypeStruct((n, nc), out_dtype),
        compiler_params=_params("parallel", "parallel"),
        name="in_proj",
    )(x, w)


def _mm_nt_kernel(wt_ref, x_ref, o_ref):
    o_ref[...] = _dot_nt(wt_ref[...], x_ref[...])


def _matmul_nt(wt, x, tm):
    r, k = wt.shape
    n = x.shape[0]
    return pl.pallas_call(
        _mm_nt_kernel,
        grid=(n // tm,),
        in_specs=[pl.BlockSpec((r, k), lambda i: (0, 0)),
                  pl.BlockSpec((tm, k), lambda i: (i, 0))],
        out_specs=pl.BlockSpec((r, tm), lambda i: (0, i)),
        out_shape=jax.ShapeDtypeStruct((r, n), F32),
        compiler_params=_params("parallel"),
        name="in_proj_small_t",
    )(wt, x)


def _rwkv_prep_kernel(p_ref, mu_ref, w0_ref, wup_ref, a0_ref, aup_ref, gup_ref, kk_ref, ka_ref,
                      bd_ref, r_o, lw_o, k_o, v_o, kn_o, b_o, g_o, carry_ref):
    @pl.when(pl.program_id(1) == 0)
    def _():
        carry_ref[...] = jnp.zeros_like(carry_ref)

    p = p_ref[...].astype(F32)
    ts = p.shape[0]
    row = _iota2((ts, 1), 0)
    prev = jnp.where(row == 0, carry_ref[...], pltpu.roll(p, 1, axis=0))
    carry_ref[...] = p[ts - 1:ts, :]
    p = p + mu_ref[...] * (prev - p)

    R = RWKV_DIM
    r = p[:, 0:R]
    k = p[:, R:2 * R]
    v = p[:, 2 * R:3 * R]
    lora = p[:, 3 * R:3 * R + DECAY_LORA + ICLR_LORA]
    gd = p[:, 3 * R + DECAY_LORA + ICLR_LORA:]
    w_pre = w0_ref[...] + _dot(_bf(jnp.tanh(lora)), wup_ref[...])
    lw = -jnp.exp(_log_sigmoid(w_pre) - 0.5)
    a = _sigmoid(a0_ref[...] + _dot(_bf(lora), aup_ref[...]))
    g = _dot(_bf(_sigmoid(gd)), gup_ref[...])
    kk = k * kk_ref[...]
    ss = _dot_sel_lhs(kk * kk, bd_ref[...], parts=2)
    kn = kk / jnp.maximum(jnp.sqrt(ss), 1e-12)
    r_o[...] = _bf(r)
    lw_o[...] = lw
    k_o[...] = _bf(k * (1.0 + (a - 1.0) * ka_ref[...]))
    v_o[...] = _bf(v)
    kn_o[...] = _bf(kn)
    b_o[...] = _bf(kn * a)
    g_o[...] = _bf(g)


def _rwkv_prep(p, B, S, ts, mu, w0, wup, a0, aup, gup, k_k, k_a, bd):
    n = B * S
    ns = S // ts
    row_spec = lambda w: pl.BlockSpec((ts, w), lambda b, s: (b * ns + s, 0))
    full = lambda a: pl.BlockSpec(a.shape, lambda b, s: (0,) * a.ndim)
    out = lambda dt: jax.ShapeDtypeStruct((n, RWKV_DIM), dt)
    return pl.pallas_call(
        _rwkv_prep_kernel,
        grid=(B, ns),
        in_specs=[row_spec(RWKV_COLS)] + [full(a) for a in (mu, w0, wup, a0, aup, gup, k_k, k_a, bd)],
        out_specs=[row_spec(RWKV_DIM)] * 7,
        out_shape=[out(BF16), out(F32)] + [out(BF16)] * 5,
        scratch_shapes=[pltpu.VMEM((1, RWKV_COLS), F32)],
        compiler_params=_params("parallel", "arbitrary"),
        name="rwkv_prep",
    )(p, mu, w0, wup, a0, aup, gup, k_k, k_a, bd)


def _rwkv_chunk_kernel(r_ref, lw_ref, k_ref, v_ref, kn_ref, b_ref, g_ref, gng_ref, gnb_ref, rk_ref,
                       o_ref, s_ref, *, n_chunks):
    C = RWKV_CHUNK

    @pl.when(pl.program_id(2) == 0)
    def _():
        s_ref[...] = jnp.zeros_like(s_ref)

    lane = _iota2((1, LANES), 1)
    m0 = (lane < HEAD).astype(F32)
    m1 = 1.0 - m0
    ri = _iota2((LANES, LANES), 0)
    ci = _iota2((LANES, LANES), 1)
    same = (ri >= HEAD) == (ci >= HEAD)
    smask = jnp.where(same & (ri > ci), 1.0, 0.0)
    imask = jnp.where(same & (ri >= ci), 1.0, 0.0)
    eye = jnp.where(ri == ci, 1.0, 0.0)
    bd_mean = _bf(jnp.where(same, 1.0 / HEAD, 0.0))
    bd_one = _bf(jnp.where(same, 1.0, 0.0))
    tri = _bf(jnp.where(_iota2((C, C), 0) >= _iota2((C, C), 1), 1.0, 0.0))

    def stack(x):
        return jnp.concatenate([x * m0, x * m1], axis=0)

    gng = gng_ref[...]
    gnb = gnb_ref[...]
    rk = rk_ref[...]

    def group(chunks):
        ld = lambda ref, c: ref[c * C:(c + 1) * C, :]
        r = [ld(r_ref, c).astype(F32) for c in chunks]
        lw = [ld(lw_ref, c) for c in chunks]
        k = [ld(k_ref, c).astype(F32) for c in chunks]
        v = [ld(v_ref, c).astype(F32) for c in chunks]
        kn = [ld(kn_ref, c).astype(F32) for c in chunks]
        b = [ld(b_ref, c).astype(F32) for c in chunks]
        idx = range(len(chunks))

        cs = [_dot_sel_rhs(tri, lw[i]) for i in idx]
        Rs, As, Ks, Bs, Rt, At, Be, Ke, Vs, g_end = ([] for _ in range(10))
        for i in idx:
            sh = cs[i][C // 2 - 1:C // 2, :]
            cs_end = cs[i][C - 1:C, :]
            e_pos = jnp.exp(cs[i] - sh)
            e_neg = jnp.exp(sh - cs[i])
            e_prev = jnp.exp(cs[i] - lw[i] - sh)
            e_end = jnp.exp(cs_end - cs[i])
            Rs.append(_bf(stack(r[i] * e_pos)))
            As.append(_bf(stack(-kn[i] * e_prev)))
            Ks.append(_bf(stack(k[i] * e_neg)))
            Bs.append(_bf(stack(b[i] * e_neg)))
            Rt.append(stack(r[i] * jnp.exp(cs[i])))
            At.append(_bf(stack(-kn[i] * jnp.exp(cs[i] - lw[i]))))
            Be.append(_bf(stack(b[i] * e_end)))
            Ke.append(_bf(stack(k[i] * e_end)))
            Vs.append(_bf(stack(v[i])))
            g_end.append(jnp.exp(cs_end))

        Aab = [_dot_nt(As[i], Bs[i]) * smask for i in idx]
        Aak = [_dot_nt(As[i], Ks[i]) * smask for i in idx]
        RK = [_dot_nt(Rs[i], Ks[i]) * imask for i in idx]
        RBb = [_bf(_dot_nt(Rs[i], Bs[i]) * imask) for i in idx]

        T = [eye + Aab[i] for i in idx]
        P = Aab
        for _ in range(int(math.log2(C)) - 1):
            Pb = [_bf(P[i]) for i in idx]
            P = [_dot(Pb[i], Pb[i]) for i in idx]
            T = [T[i] + _dot(_bf(P[i]), _bf(T[i])) for i in idx]
        Tb = [_bf(T[i]) for i in idx]

        W1 = [_dot(_bf(Aak[i]), Vs[i]) for i in idx]
        TAb = [_bf(_dot(Tb[i], At[i])) for i in idx]
        U0b = [_bf(_dot(Tb[i], _bf(W1[i]))) for i in idx]
        Q = [_bf(Rt[i] + _dot(RBb[i], TAb[i])) for i in idx]
        Y0 = [_dot(_bf(RK[i]), Vs[i]) + _dot(RBb[i], U0b[i]) for i in idx]
        Mt = [_bf(_dot_tn(TAb[i], Be[i])) for i in idx]
        Nt = [_dot_tn(Vs[i], Ke[i]) + _dot_tn(U0b[i], Be[i]) for i in idx]

        Y = []
        S0 = s_ref[...]
        for i in idx:
            S0b = _bf(S0)
            Y.append(_dot_nt(Q[i], S0b) + Y0[i])
            S0 = S0 * g_end[i] + _dot(S0b, Mt[i]) + Nt[i]
        s_ref[...] = S0

        for i, c in enumerate(chunks):
            y = Y[i][:C, :] + Y[i][C:, :]
            mean = _dot_sel_lhs(y, bd_mean, parts=2)
            yc = y - mean
            var = _dot_sel_lhs(yc * yc, bd_mean, parts=2)
            yn = yc * lax.rsqrt(var + RWKV_GN_EPS) * gng + gnb
            bonus = _dot_sel_lhs(r[i] * k[i] * rk, bd_one, parts=2) * v[i]
            o_ref[c * C:(c + 1) * C, :] = _bf((yn + bonus) * ld(g_ref, c).astype(F32))

    for first in range(0, n_chunks, RWKV_GROUP):
        group(list(range(first, min(first + RWKV_GROUP, n_chunks))))


def _rwkv_chunk(r, lw, k, v, kn, b, g, gn_g, gn_b, r_k, B, S, n_chunks):
    n = B * S
    rows = n_chunks * RWKV_CHUNK
    ns = S // rows
    npair = RWKV_DIM // LANES
    blk = pl.BlockSpec((rows, LANES), lambda bb, p, s: (bb * ns + s, p))
    par = pl.BlockSpec((1, LANES), lambda bb, p, s: (0, p))
    return pl.pallas_call(
        functools.partial(_rwkv_chunk_kernel, n_chunks=n_chunks),
        grid=(B, npair, ns),
        in_specs=[blk] * 7 + [par] * 3,
        out_specs=blk,
        out_shape=jax.ShapeDtypeStruct((n, RWKV_DIM), BF16),
        scratch_shapes=[pltpu.VMEM((LANES, LANES), F32)],
        compiler_params=_params("parallel", "parallel", "arbitrary"),
        name="rwkv_chunk",
    )(r, lw, k, v, kn, b, g, gn_g, gn_b, r_k)


def _fox_c_kernel(ps_ref, pst_ref, bl_ref, bc_ref, cc_ref, cr_ref, *, seq):
    L = LANES
    ri = _iota2((L, L), 0)
    ci = _iota2((L, L), 1)
    tril = _bf(jnp.where(ri >= ci, 1.0, 0.0))
    triu = _bf(jnp.where(ri <= ci, 1.0, 0.0))
    carry = jnp.zeros((1, L), F32)
    carry_c = jnp.zeros((SMALL_ROWS, 1), F32)
    for i in range(seq // L):
        rows = slice(i * L, (i + 1) * L)
        cs = _dot_sel_rhs(tril, _log_sigmoid(ps_ref[rows, :] + bl_ref[...])) + carry
        cc_ref[rows, :] = cs
        carry = cs[L - 1:L, :]
        cst = _dot_sel_lhs(_log_sigmoid(pst_ref[:, rows] + bc_ref[...]), triu) + carry_c
        cr_ref[:, rows] = cst
        carry_c = cst[:, L - 1:L]


def _fox_c(ps, pst, bias_lane, bias_col, B, S):
    n = B * S
    return pl.pallas_call(
        functools.partial(_fox_c_kernel, seq=S),
        grid=(B,),
        in_specs=[pl.BlockSpec((S, LANES), lambda b: (b, 0)),
                  pl.BlockSpec((SMALL_ROWS, S), lambda b: (0, b)),
                  pl.BlockSpec((1, LANES), lambda b: (0, 0)),
                  pl.BlockSpec((SMALL_ROWS, 1), lambda b: (0, 0))],
        out_specs=[pl.BlockSpec((S, LANES), lambda b: (b, 0)),
                   pl.BlockSpec((SMALL_ROWS, S), lambda b: (0, b))],
        out_shape=[jax.ShapeDtypeStruct((n, LANES), F32),
                   jax.ShapeDtypeStruct((SMALL_ROWS, n), F32)],
        compiler_params=_params("parallel"),
        name="fox_cumgate",
    )(ps, pst, bias_lane, bias_col)


def _transpose_blocks(x):
    n = x.shape[0] // LANES
    return jnp.concatenate([x[i * LANES:(i + 1) * LANES, :].T for i in range(n)], axis=1)


def _fox_attn_kernel(q_ref, k_ref, v_ref, cc_ref, cr_ref, o_ref, vt_ref, ck_ref, s_scr, p_scr, *, tq, seq):
    nblk = q_ref.shape[1] // LANES
    heads = nblk * PAIR
    head0 = pl.program_id(1) * heads
    qi = pl.program_id(2)

    @pl.when(qi == 0)
    def _():
        for i in range(seq // LANES):
            rows = slice(i * LANES, (i + 1) * LANES)
            for c in range(nblk):
                cols = slice(c * LANES, (c + 1) * LANES)
                vt_ref[cols, rows] = _bf(v_ref[rows, cols].astype(F32).T)
        ri = _iota2((LANES, LANES), 0)
        for h in range(heads):
            sel = _bf(jnp.where(ri == head0 + h, 1.0, 0.0))
            ck_ref[h] = _dot_sel_lhs(cc_ref[...], sel) * LOG2E

    in0 = _iota2((LANES, 1), 0) < HEAD
    qts = []
    for c in range(nblk):
        qt = _transpose_blocks(q_ref[:, c * LANES:(c + 1) * LANES].astype(F32) * (HEAD ** -0.5 * LOG2E))
        qts += [_bf(jnp.where(in0, qt, 0.0)), _bf(jnp.where(in0, 0.0, qt))]
    q0 = pl.multiple_of(qi * tq, tq)
    cqs = tuple(cr_ref[pl.ds(head0 + h, 1), pl.ds(q0, tq)] * LOG2E for h in range(heads))
    key_minus_query = _iota2((tq, tq), 0) - _iota2((tq, tq), 1)
    last_block = seq // tq - 1

    s_scr[...] = jnp.full(s_scr.shape, -jnp.inf, F32)
    p_scr[...] = jnp.zeros(p_scr.shape, BF16)

    def step(j, carry, masked):
        blk_pv = jnp.clip(j - 2, 0, last_block)
        blk_qk = jnp.minimum(j, last_block)
        pv0 = pl.multiple_of(blk_pv * tq, tq)
        qk0 = pl.multiple_of(blk_qk * tq, tq)
        kbs = [k_ref[pl.ds(qk0, tq), c * LANES:(c + 1) * LANES] for c in range(nblk)]
        if masked:
            visible = key_minus_query <= (qi - j) * tq
        new = []
        for h in range(heads):
            m, l, alpha, acc = carry[h]
            vt = vt_ref[h * HEAD:(h + 1) * HEAD, pl.ds(pv0, tq)]
            acc = alpha * acc + _dot(vt, p_scr[h])
            s = s_scr[h]
            m_new = jnp.maximum(m, jnp.max(s, axis=0, keepdims=True))
            p = jnp.exp2(s - m_new)
            alpha = jnp.exp2(m - m_new)
            l = alpha * l + jnp.sum(p, axis=0, keepdims=True)
            p_scr[h] = _bf(p)
            ck = ck_ref[h, pl.ds(qk0, tq), :]
            s_next = _dot(kbs[h // PAIR], qts[h]) + cqs[h] - jnp.concatenate([ck] * (tq // LANES), axis=1)
            s_scr[h] = jnp.where(visible, s_next, -jnp.inf) if masked else s_next
            new.append((m_new, l, alpha, acc))
        return tuple(new)

    init = tuple((jnp.full((1, tq), -1e30, F32), jnp.zeros((1, tq), F32), jnp.ones((1, tq), F32),
                  jnp.zeros((HEAD, tq), F32)) for _ in range(heads))
    carry = lax.fori_loop(0, qi, lambda j, cr: step(j, cr, False), init)
    carry = lax.fori_loop(qi, qi + 3, lambda j, cr: step(j, cr, True), carry)
    ot = jnp.concatenate([acc / l for (_, l, _, acc) in carry], axis=0)
    o_ref[...] = jnp.concatenate(
        [jnp.concatenate([ot[c * LANES:(c + 1) * LANES, i * LANES:(i + 1) * LANES].T for c in range(nblk)], axis=1)
         for i in range(tq // LANES)], axis=0).astype(o_ref.dtype)


def _fox_attn(qkv, cc, cr, B, S, tq):
    n = B * S
    nq = S // tq
    w = FOX_WIDTH
    nw = FOX_DIM // w
    heads = w // HEAD
    return pl.pallas_call(
        functools.partial(_fox_attn_kernel, tq=tq, seq=S),
        grid=(B, nw, nq),
        in_specs=[pl.BlockSpec((tq, w), lambda b, p, i: (b * nq + i, p)),
                  pl.BlockSpec((S, w), lambda b, p, i: (b, nw + p)),
                  pl.BlockSpec((S, w), lambda b, p, i: (b, 2 * nw + p)),
                  pl.BlockSpec((S, LANES), lambda b, p, i: (b, 0)),
                  pl.BlockSpec((SMALL_ROWS, S), lambda b, p, i: (0, b))],
        out_specs=pl.BlockSpec((tq, w), lambda b, p, i: (b * nq + i, p)),
        out_shape=jax.ShapeDtypeStruct((n, FOX_DIM), BF16),
        scratch_shapes=[pltpu.VMEM((w, S), BF16), pltpu.VMEM((heads, S, LANES), F32),
                        pltpu.VMEM((heads, tq, tq), F32), pltpu.VMEM((heads, tq, tq), BF16)],
        compiler_params=_params("parallel", "parallel", "arbitrary"),
        name="fox_attn",
    )(qkv, qkv, qkv, cc, cr)


def _mamba_kernel(z_ref, xbc_ref, ps_ref, pst_ref, cw_ref, cb_ref, dtbl_ref, dtbc_ref, al_ref, ac_ref,
                  dl_ref, ng_ref, ex_ref, o_ref, halo_ref, st_ref):
    L = SSD_CHUNK

    @pl.when(pl.program_id(1) == 0)
    def _():
        halo_ref[...] = jnp.zeros_like(halo_ref)
        st_ref[...] = jnp.zeros_like(st_ref)

    xbc = xbc_ref[...].astype(F32)
    halo = halo_ref[...]
    row8 = _iota2((SUBLANES, 1), 0)
    acc = cb_ref[...] + cw_ref[CONV_WIDTH - 1:CONV_WIDTH, :] * xbc
    for sft in range(1, CONV_WIDTH):
        rolled = pltpu.roll(xbc, sft, axis=0)
        top = jnp.where(row8 < sft, pltpu.roll(halo, sft, axis=0), rolled[:SUBLANES, :])
        shifted = jnp.concatenate([top, rolled[SUBLANES:, :]], axis=0)
        acc = acc + cw_ref[CONV_WIDTH - 1 - sft:CONV_WIDTH - sft, :] * shifted
    halo_ref[...] = xbc[L - SUBLANES:, :]
    act = acc * _sigmoid(acc)
    xs = act[:, :SSM_DIM]
    Bm = act[:, SSM_DIM:SSM_DIM + SSM_GROUPS * SSM_STATE]
    Cm = act[:, SSM_DIM + SSM_GROUPS * SSM_STATE:]

    ri = _iota2((L, L), 0)
    ci = _iota2((L, L), 1)
    lower = ri >= ci
    tril = _bf(jnp.where(lower, 1.0, 0.0))
    triu = _bf(jnp.where(ri <= ci, 1.0, 0.0))
    lane = _iota2((1, LANES), 1)
    in0 = lane < HEAD

    dt_full = _dot_sel_lhs(_softplus(ps_ref[...] + dtbl_ref[...]), ex_ref[...])
    cs = _dot_sel_rhs(tril, dt_full * al_ref[...])
    cs_end = cs[L - 1:L, :]
    ecs = jnp.exp(cs)
    X = xs * dt_full
    Xe = X * jnp.exp(cs_end - cs)
    csr = _dot_sel_lhs(_softplus(pst_ref[...] + dtbc_ref[...]) * ac_ref[...], triu)

    heads_per_group = SSM_HEADS // SSM_GROUPS
    pairs_per_group = heads_per_group // PAIR
    ys = []
    for g in range(SSM_GROUPS):
        Bg = Bm[:, g * SSM_STATE:(g + 1) * SSM_STATE]
        Cg = _bf(Cm[:, g * SSM_STATE:(g + 1) * SSM_STATE])
        CB = _dot_nt(Cg, _bf(Bg))
        BgT = _bf(Bg.T)
        for pp in range(pairs_per_group):
            p = g * pairs_per_group + pp
            cols = slice(p * LANES, (p + 1) * LANES)
            Xp = X[:, cols]
            ydiag = None
            for hh in range(PAIR):
                h = PAIR * p + hh
                decay = jnp.exp(cs[:, h * HEAD:h * HEAD + 1] - csr[DT_OFF + h:DT_OFF + h + 1, :])
                Mh = _bf(CB * jnp.where(lower, decay, 0.0))
                Xh = _bf(jnp.where(in0, Xp, 0.0) if hh == 0 else jnp.where(in0, 0.0, Xp))
                t = _dot(Mh, Xh)
                ydiag = t if ydiag is None else ydiag + t
            st = st_ref[p]
            yoff = _dot(Cg, _bf(st)) * ecs[:, cols]
            st_ref[p] = st * ecs[L - 1:L, cols] + _dot(BgT, _bf(Xe[:, cols]))
            ys.append(ydiag + yoff + dl_ref[:, cols] * xs[:, cols])
    z = z_ref[...].astype(F32)
    y = jnp.concatenate(ys, axis=1) * (z * _sigmoid(z))
    gw = SSM_DIM // SSM_GROUPS
    outs = []
    for g in range(SSM_GROUPS):
        yg = y[:, g * gw:(g + 1) * gw]
        ms = jnp.mean(yg * yg, axis=-1, keepdims=True)
        outs.append(yg * lax.rsqrt(ms + NORM_EPS))
    o_ref[...] = _bf(jnp.concatenate(outs, axis=1) * ng_ref[...])


def _mamba(z, xbc, ps, pst, cw, cb, dtbl, dtbc, al, ac, dl, ng, ex, B, S):
    n = B * S
    L = SSD_CHUNK
    nc = S // L
    row = lambda w: pl.BlockSpec((L, w), lambda b, c: (b * nc + c, 0))
    full = lambda a: pl.BlockSpec(a.shape, lambda b, c: (0,) * a.ndim)
    return pl.pallas_call(
        _mamba_kernel,
        grid=(B, nc),
        in_specs=[row(SSM_DIM), row(SSM_CONV_DIM), row(LANES),
                  pl.BlockSpec((SMALL_ROWS, L), lambda b, c: (0, b * nc + c))]
                 + [full(a) for a in (cw, cb, dtbl, dtbc, al, ac, dl, ng, ex)],
        out_specs=row(SSM_DIM),
        out_shape=jax.ShapeDtypeStruct((n, SSM_DIM), BF16),
        scratch_shapes=[pltpu.VMEM((SUBLANES, SSM_CONV_DIM), F32),
                        pltpu.VMEM((SSM_HEADS // PAIR, SSM_STATE, LANES), F32)],
        compiler_params=_params("parallel", "arbitrary"),
        name="mamba_ssd",
    )(z, xbc, ps, pst, cw, cb, dtbl, dtbc, al, ac, dl, ng, ex)


def _layer_norm(h, g, b):
    mu = jnp.mean(h, axis=-1, keepdims=True)
    hc = h - mu
    var = jnp.mean(hc * hc, axis=-1, keepdims=True)
    return hc * lax.rsqrt(var + LN_EPS) * g + b


def _merge_kernel(or_ref, of_ref, om_ref, pg_ref, x_ref, wr_ref, wf_ref, wm_ref, wo_ref, gb_ref,
                  lg_ref, lb_ref, rwh_ref, rwl_ref, x1_ref, lgt_ref, *, alpha):
    D = D_MODEL
    gate = lambda i: _sigmoid(pg_ref[:, i * D:(i + 1) * D].astype(F32) + gb_ref[:, i * D:(i + 1) * D])
    merged = (gate(0) * _dot(or_ref[...], wr_ref[...])
              + gate(1) * _dot(of_ref[...], wf_ref[...])
              + gate(2) * _dot(om_ref[...], wm_ref[...]))
    mix = _dot(_bf(merged), wo_ref[...])
    x1 = _layer_norm(alpha * x_ref[...] + mix, lg_ref[...], lb_ref[...])
    x1_ref[...] = x1
    xh = _bf(x1)
    xl = _bf(x1 - xh.astype(F32))
    lgt_ref[...] = (_dot_nt(rwh_ref[...], xh) + _dot_nt(rwh_ref[...], xl) + _dot_nt(rwl_ref[...], xh))


def _merge(o_r, o_f, o_m, pg, x, wr, wf, wm, wo, gb, lg, lb, rwh, rwl, tm, alpha):
    n = x.shape[0]
    row = lambda w: pl.BlockSpec((tm, w), lambda i: (i, 0))
    full = lambda a: pl.BlockSpec(a.shape, lambda i: (0,) * a.ndim)
    return pl.pallas_call(
        functools.partial(_merge_kernel, alpha=alpha),
        grid=(n // tm,),
        in_specs=[row(RWKV_DIM), row(FOX_DIM), row(SSM_DIM), row(3 * D_MODEL), row(D_MODEL)]
                 + [full(a) for a in (wr, wf, wm, wo, gb, lg, lb, rwh, rwl)],
        out_specs=[row(D_MODEL), pl.BlockSpec((N_EXPERTS, tm), lambda i: (0, i))],
        out_shape=[jax.ShapeDtypeStruct((n, D_MODEL), F32),
                   jax.ShapeDtypeStruct((N_EXPERTS, n), F32)],
        compiler_params=_params("parallel"),
        name="merge_ln1_router",
    )(o_r, o_f, o_m, pg, x, wr, wf, wm, wo, gb, lg, lb, rwh, rwl)


def _rows(*vals):
    t = vals[0].shape[1]
    sub = _iota2((SUBLANES, t), 0)
    out = jnp.zeros((SUBLANES, t), vals[0].dtype)
    for i, v in enumerate(vals):
        out = jnp.where(sub == i, v, out)
    return out


def _route_select_kernel(lgt_ref, bias_ref, e_ref, g_ref, rank_ref, cnt_ref, carry_ref):
    @pl.when(pl.program_id(0) == 0)
    def _():
        carry_ref[...] = jnp.zeros_like(carry_ref)

    t = lgt_ref.shape[1]
    aff = _sigmoid(lgt_ref[...])
    sel = aff + bias_ref[...]
    gsz = EXPERTS_PER_GROUP
    sub = _iota2((gsz, t), 0).astype(F32)
    best = e1 = e2 = None
    for g in range(N_EXPERT_GROUPS):
        s = sel[g * gsz:(g + 1) * gsz, :]
        m1 = jnp.max(s, axis=0, keepdims=True)
        i1 = jnp.min(jnp.where(s == m1, sub, float(gsz)), axis=0, keepdims=True)
        s2 = jnp.where(sub == i1, -jnp.inf, s)
        m2 = jnp.max(s2, axis=0, keepdims=True)
        i2 = jnp.min(jnp.where(s2 == m2, sub, float(gsz)), axis=0, keepdims=True)
        score = m1 + m2
        if g == 0:
            best, e1, e2 = score, i1, i2
        else:
            better = score > best
            best = jnp.where(better, score, best)
            e1 = jnp.where(better, i1 + float(g * gsz), e1)
            e2 = jnp.where(better, i2 + float(g * gsz), e2)
    row = _iota2((N_EXPERTS, t), 0).astype(F32)
    oh1 = jnp.where(row == e1, 1.0, 0.0)
    oh2 = jnp.where(row == e2, 1.0, 0.0)
    w1 = jnp.sum(oh1 * aff, axis=0, keepdims=True)
    w2 = jnp.sum(oh2 * aff, axis=0, keepdims=True)
    den = w1 + w2
    cnt = oh1 + oh2
    before = _bf(jnp.where(_iota2((t, t), 0) < _iota2((t, t), 1), 1.0, 0.0))
    prefix = _dot(_bf(cnt), before) + carry_ref[...]
    r1 = jnp.sum(oh1 * prefix, axis=0, keepdims=True)
    r2 = jnp.sum(oh2 * prefix, axis=0, keepdims=True)
    carry_ref[...] = carry_ref[...] + jnp.sum(cnt, axis=1, keepdims=True)
    e_ref[...] = _rows(e1, e2)
    g_ref[...] = _rows(w1 / den, w2 / den)
    rank_ref[...] = _rows(r1, r2)
    cnt_ref[...] = jnp.broadcast_to(carry_ref[...], cnt_ref.shape)


def _route_select(lgt, bias_col, tile):
    n = lgt.shape[1]
    col = lambda r: pl.BlockSpec((r, tile), lambda i: (0, i))
    out = jax.ShapeDtypeStruct((SUBLANES, n), F32)
    return pl.pallas_call(
        _route_select_kernel,
        grid=(n // tile,),
        in_specs=[col(N_EXPERTS), pl.BlockSpec((N_EXPERTS, 1), lambda i: (0, 0))],
        out_specs=[col(SUBLANES)] * 3 + [pl.BlockSpec((N_EXPERTS, LANES), lambda i: (0, 0))],
        out_shape=[out] * 3 + [jax.ShapeDtypeStruct((N_EXPERTS, LANES), F32)],
        scratch_shapes=[pltpu.VMEM((N_EXPERTS, 1), F32)],
        compiler_params=_params("arbitrary"),
        name="route_select",
    )(lgt, bias_col)


def _route_place_kernel(e_ref, rank_ref, cnt_ref, dest_ref, be_ref, nu_ref):
    t = e_ref.shape[1]
    blk = float(MOE_BLOCK)
    counts = cnt_ref[...]
    padded = jnp.floor((counts + (blk - 1.0)) / blk) * blk
    ne = N_EXPERTS
    lower = _bf(jnp.where(_iota2((ne, ne), 0) > _iota2((ne, ne), 1), 1.0, 0.0))
    pad_start = _dot_sel_rhs(lower, padded)
    pad_end = pad_start + padded
    row = _iota2((ne, t), 0).astype(F32)
    ps_t = jnp.concatenate([pad_start] * (t // LANES), axis=1)
    e = e_ref[...]
    rank = rank_ref[...]
    dests = []
    for k in range(TOP_K):
        oh = jnp.where(row == e[k:k + 1, :], 1.0, 0.0)
        dests.append(jnp.sum(oh * ps_t, axis=0, keepdims=True) + rank[k:k + 1, :])
    dest_ref[...] = _rows(*dests).astype(jnp.int32)
    nbp = be_ref.shape[1]
    blk_start = _iota2((ne, nbp), 1).astype(F32) * blk
    pe_t = jnp.concatenate([pad_end] * (nbp // LANES), axis=1)
    be = jnp.sum(jnp.where(pe_t <= blk_start, 1.0, 0.0), axis=0, keepdims=True)
    be_ref[...] = jnp.broadcast_to(jnp.minimum(be, float(ne - 1)), be_ref.shape).astype(jnp.int32)
    nu_ref[...] = jnp.broadcast_to(pad_end[ne - 1:ne, :] / blk, nu_ref.shape).astype(jnp.int32)


def _route_place(e, rank, cnt, tile, n_blocks):
    n = e.shape[1]
    nbp = -(-n_blocks // LANES) * LANES
    col = pl.BlockSpec((SUBLANES, tile), lambda i: (0, i))
    return pl.pallas_call(
        _route_place_kernel,
        grid=(n // tile,),
        in_specs=[col, col, pl.BlockSpec((N_EXPERTS, LANES), lambda i: (0, 0))],
        out_specs=[col, pl.BlockSpec((SUBLANES, nbp), lambda i: (0, 0)),
                   pl.BlockSpec((SUBLANES, LANES), lambda i: (0, 0))],
        out_shape=[jax.ShapeDtypeStruct((SUBLANES, n), jnp.int32),
                   jax.ShapeDtypeStruct((SUBLANES, nbp), jnp.int32),
                   jax.ShapeDtypeStruct((SUBLANES, LANES), jnp.int32)],
        compiler_params=_params("arbitrary"),
        name="route_place",
    )(e, rank, cnt)


def _moe_kernel(be_ref, nu_ref, xb_ref, wg_ref, wu_ref, wd_ref, yb_ref, wgb, wub, wdb):
    i = pl.program_id(0)
    new_expert = jnp.logical_or(i == 0, be_ref[i] != be_ref[jnp.maximum(i - 1, 0)])

    @pl.when(new_expert)
    def _():
        wgb[...] = _bf(wg_ref[0])
        wub[...] = _bf(wu_ref[0])
        wdb[...] = _bf(wd_ref[0])

    @pl.when(i < nu_ref[0])
    def _():
        x = _bf(xb_ref[...])
        gt = _dot(x, wgb[...])
        h = gt * _sigmoid(gt) * _dot(x, wub[...])
        yb_ref[...] = _dot(_bf(h), wdb[...])

    @pl.when(i >= nu_ref[0])
    def _():
        yb_ref[...] = jnp.zeros_like(yb_ref)


def _moe(block_expert, n_used, xb, wg, wu, wd):
    rows, d = xb.shape
    blk = MOE_BLOCK
    nb = rows // blk
    grid_spec = pltpu.PrefetchScalarGridSpec(
        num_scalar_prefetch=2,
        grid=(nb,),
        in_specs=[pl.BlockSpec((blk, d), lambda i, be, nu: (i, 0)),
                  pl.BlockSpec((1, d, D_EXPERT), lambda i, be, nu: (be[i], 0, 0)),
                  pl.BlockSpec((1, d, D_EXPERT), lambda i, be, nu: (be[i], 0, 0)),
                  pl.BlockSpec((1, D_EXPERT, d), lambda i, be, nu: (be[i], 0, 0))],
        out_specs=pl.BlockSpec((blk, d), lambda i, be, nu: (i, 0)),
        scratch_shapes=[pltpu.VMEM((d, D_EXPERT), BF16), pltpu.VMEM((d, D_EXPERT), BF16),
                        pltpu.VMEM((D_EXPERT, d), BF16)])
    return pl.pallas_call(
        _moe_kernel,
        grid_spec=grid_spec,
        out_shape=jax.ShapeDtypeStruct((rows, d), F32),
        compiler_params=_params("arbitrary"),
        name="moe_experts",
    )(block_expert, n_used, xb, wg, wu, wd)


def _dispatch_kernel(dest_ref, x_hbm, buf_in_hbm, buf_hbm, sem, *, tile, n):
    del buf_in_hbm
    base = pl.program_id(0) * tile

    def row_copy(j, k):
        t = base + j
        return pltpu.make_async_copy(x_hbm.at[pl.ds(t, 1)], buf_hbm.at[pl.ds(dest_ref[k * n + t], 1)], sem)

    def start(j, carry):
        for k in range(TOP_K):
            row_copy(j, k).start()
        return carry

    def wait(j, carry):
        for k in range(TOP_K):
            row_copy(j, k).wait()
        return carry

    lax.fori_loop(0, tile, start, 0)
    lax.fori_loop(0, tile, wait, 0)


def _dispatch(dest_flat, x, rows, tile):
    n, d = x.shape
    any_spec = pl.BlockSpec(memory_space=pl.ANY)
    grid_spec = pltpu.PrefetchScalarGridSpec(
        num_scalar_prefetch=1,
        grid=(n // tile,),
        in_specs=[any_spec, any_spec],
        out_specs=any_spec,
        scratch_shapes=[pltpu.SemaphoreType.DMA(())])
    return pl.pallas_call(
        functools.partial(_dispatch_kernel, tile=tile, n=n),
        grid_spec=grid_spec,
        out_shape=jax.ShapeDtypeStruct((rows, d), x.dtype),
        input_output_aliases={2: 0},
        compiler_params=_params("arbitrary"),
        name="moe_dispatch",
    )(dest_flat, x, jnp.zeros((rows, d), x.dtype))


def _combine_ln2_kernel(dest_ref, x_ref, gate_ref, yb_hbm, g_ref, b_ref, o_ref, ob_ref, buf, sems,
                        *, tile, n, alpha):
    i = pl.program_id(0)
    steps = pl.num_programs(0)

    def row_copy(step, slot, j, k):
        t = step * tile + j
        return pltpu.make_async_copy(yb_hbm.at[pl.ds(dest_ref[k * n + t], 1)],
                                     buf.at[slot, k, pl.ds(j, 1)], sems.at[slot])

    def start_tile(step, slot):
        def body(j, carry):
            for k in range(TOP_K):
                row_copy(step, slot, j, k).start()
            return carry
        lax.fori_loop(0, tile, body, 0)

    def wait_tile(step, slot):
        def body(j, carry):
            for k in range(TOP_K):
                row_copy(step, slot, j, k).wait()
            return carry
        lax.fori_loop(0, tile, body, 0)

    slot = lax.rem(i, 2)

    @pl.when(i == 0)
    def _():
        start_tile(0, 0)

    @pl.when(i + 1 < steps)
    def _():
        start_tile(i + 1, 1 - slot)

    wait_tile(i, slot)
    gate = gate_ref[...]
    ffn = gate[:, 0:1] * buf[slot, 0] + gate[:, 1:2] * buf[slot, 1]
    y = _layer_norm(alpha * x_ref[...] + ffn, g_ref[...], b_ref[...])
    o_ref[...] = y
    ob_ref[...] = _bf(y)


def _combine_ln2(dest_flat, x, gate, yb, g, b, tile, alpha):
    n, d = x.shape
    row = pl.BlockSpec((tile, d), lambda i, dest: (i, 0))
    par = pl.BlockSpec((1, d), lambda i, dest: (0, 0))
    grid_spec = pltpu.PrefetchScalarGridSpec(
        num_scalar_prefetch=1,
        grid=(n // tile,),
        in_specs=[row, pl.BlockSpec((tile, TOP_K), lambda i, dest: (i, 0)),
                  pl.BlockSpec(memory_space=pl.ANY), par, par],
        out_specs=[row, row],
        scratch_shapes=[pltpu.VMEM((2, TOP_K, tile, d), F32), pltpu.SemaphoreType.DMA((2,))])
    return pl.pallas_call(
        functools.partial(_combine_ln2_kernel, tile=tile, n=n, alpha=alpha),
        grid_spec=grid_spec,
        out_shape=[jax.ShapeDtypeStruct((n, d), F32), jax.ShapeDtypeStruct((n, d), BF16)],
        compiler_params=_params("arbitrary"),
        name="moe_combine_ln2",
    )(dest_flat, x, gate, yb, g, b)


def _route(lgt, bias_col, tile):
    n = lgt.shape[1]
    n_blocks = -(-(n * TOP_K) // MOE_BLOCK) + N_EXPERTS
    e, g, rank, cnt = _route_select(lgt, bias_col, tile)
    dest, be, nu = _route_place(e, rank, cnt, tile, n_blocks)
    return g[:TOP_K].T, dest[:TOP_K].reshape(-1), be[0, :n_blocks], nu[0, :1], n_blocks


def _tiles(B, S):
    n = B * S
    return dict(
        tm_proj=min(1024, n),
        ts_prep=min(512, S),
        rwkv_chunks=min(8, S // RWKV_CHUNK),
        tq=min(256, S),
        tm_merge=min(512, n),
        t_route=min(512, n),
        t_dispatch=min(512, n),
        t_combine=min(256, n),
    )


def _pad_rows(a, before, total):
    return jnp.zeros((total, a.shape[1]), a.dtype).at[before:before + a.shape[0]].set(a)


def _block_diag_ones(dim):
    i = jnp.arange(dim) // HEAD
    return (i[:, None] == i[None, :]).astype(BF16)


def kernel(x, w_in, rwkv_mu, rwkv_w0, rwkv_w_up, rwkv_a0, rwkv_a_up, rwkv_g_up, rwkv_k_k, rwkv_k_a,
           rwkv_r_k, rwkv_gn_g, rwkv_gn_b, fox_f_bias, ssm_conv_w, ssm_conv_b, ssm_dt_bias, ssm_a_log,
           ssm_d, ssm_norm_g, gate_bias, w_br_rwkv, w_br_fox, w_br_ssm, w_out, ln1_g, ln1_b, router_w,
           router_bias, exp_w_gate, exp_w_up, exp_w_down, ln2_g, ln2_b):
    B, S, D = x.shape
    depth = w_in.shape[0]
    n = B * S
    t = _tiles(B, S)
    alpha = (2 * depth) ** 0.25

    row1 = lambda a: a.reshape(1, -1).astype(F32)
    bd512 = _block_diag_ones(RWKV_DIM)
    expand = (jnp.arange(LANES)[:, None] == DT_OFF + jnp.arange(SSM_DIM)[None, :] // HEAD).astype(BF16)
    rw = router_w.astype(F32).T
    rwh = rw.astype(BF16)
    rwl = (rw - rwh.astype(F32)).astype(BF16)
    rb_col = router_bias.astype(F32).reshape(N_EXPERTS, 1)

    xf = x.reshape(n, D).astype(F32)
    xb16 = xf.astype(BF16)
    for l in range(depth):
        w = w_in[l]
        w_small = jnp.concatenate([w[:, OFF_SSM - FOX_HEADS:OFF_SSM], w[:, OFF_GATE - SSM_HEADS:OFF_GATE]], axis=1)
        w_small_l = jnp.zeros((D, LANES), F32).at[:, :w_small.shape[1]].set(w_small).astype(BF16)
        w_small_t = jnp.zeros((SMALL_ROWS, D), F32).at[:w_small.shape[1]].set(w_small.T).astype(BF16)
        mm = lambda lo, hi: _matmul(xb16, w[:, lo:hi].astype(BF16), t["tm_proj"],
                                    512 if (hi - lo) % 512 == 0 else 256, BF16)
        p_rwkv = mm(0, RWKV_COLS)
        p_fox = mm(OFF_FOX, OFF_FOX + 3 * FOX_DIM)
        p_z = mm(OFF_SSM, OFF_SSM + SSM_DIM)
        p_xbc = mm(OFF_SSM + SSM_DIM, OFF_SSM + SSM_DIM + SSM_CONV_DIM)
        p_gate = mm(OFF_GATE, OFF_GATE + 3 * D)
        ps = _matmul(xb16, w_small_l, t["tm_proj"], LANES)
        pst = _matmul_nt(w_small_t, xb16, t["tm_proj"])

        wup = _pad_rows(rwkv_w_up[l], 0, DECAY_LORA + ICLR_LORA).astype(BF16)
        aup = _pad_rows(rwkv_a_up[l], DECAY_LORA, DECAY_LORA + ICLR_LORA).astype(BF16)
        r, lw, k, v, kn, b, g = _rwkv_prep(
            p_rwkv, B, S, t["ts_prep"], row1(rwkv_mu[l]), row1(rwkv_w0[l]), wup, row1(rwkv_a0[l]), aup,
            rwkv_g_up[l].astype(BF16), row1(rwkv_k_k[l]), row1(rwkv_k_a[l]), bd512)
        o_r = _rwkv_chunk(r, lw, k, v, kn, b, g, row1(rwkv_gn_g[l]), row1(rwkv_gn_b[l]),
                          row1(rwkv_r_k[l]), B, S, t["rwkv_chunks"])

        fb_lane = jnp.zeros((1, LANES), F32).at[0, :FOX_HEADS].set(fox_f_bias[l].astype(F32))
        fb_col = jnp.zeros((SMALL_ROWS, 1), F32).at[:FOX_HEADS, 0].set(fox_f_bias[l].astype(F32))
        cc, cr = _fox_c(ps, pst, fb_lane, fb_col, B, S)
        o_f = _fox_attn(p_fox, cc, cr, B, S, t["tq"])

        dtb = ssm_dt_bias[l].astype(F32)
        a_neg = -jnp.exp(ssm_a_log[l].astype(F32))
        dtbl = jnp.zeros((1, LANES), F32).at[0, DT_OFF:DT_OFF + SSM_HEADS].set(dtb)
        dtbc = jnp.zeros((SMALL_ROWS, 1), F32).at[DT_OFF:DT_OFF + SSM_HEADS, 0].set(dtb)
        ac = jnp.zeros((SMALL_ROWS, 1), F32).at[DT_OFF:DT_OFF + SSM_HEADS, 0].set(a_neg)
        al = jnp.repeat(a_neg, HEAD).reshape(1, SSM_DIM)
        dl = jnp.repeat(ssm_d[l].astype(F32), HEAD).reshape(1, SSM_DIM)
        o_m = _mamba(p_z, p_xbc, ps, pst, ssm_conv_w[l].astype(F32), row1(ssm_conv_b[l]), dtbl, dtbc,
                     al, ac, dl, row1(ssm_norm_g[l]), expand, B, S)

        x1, logits = _merge(
            o_r, o_f, o_m, p_gate, xf, w_br_rwkv[l].astype(BF16), w_br_fox[l].astype(BF16),
            w_br_ssm[l].astype(BF16), w_out[l].astype(BF16), row1(gate_bias[l]), row1(ln1_g[l]),
            row1(ln1_b[l]), rwh, rwl, t["tm_merge"], alpha)

        gate, dest, block_expert, n_used, n_blocks = _route(logits, rb_col, t["t_route"])
        xb = _dispatch(dest, x1, n_blocks * MOE_BLOCK, t["t_dispatch"])
        yb = _moe(block_expert, n_used, xb, exp_w_gate[l], exp_w_up[l], exp_w_down[l])
        xf, xb16 = _combine_ln2(dest, x1, gate, yb, row1(ln2_g[l]), row1(ln2_b[l]), t["t_combine"], alpha)
    return xf.reshape(B, S, D).astype(x.dtype)
```

```python
import functools
import math

import jax
import jax.numpy as jnp
from jax import lax
from jax.experimental import pallas as pl
from jax.experimental.pallas import tpu as pltpu

F32 = jnp.float32
BF16 = jnp.bfloat16

LANES = 128
SUBLANES = 8
VMEM_LIMIT = 48 * 1024 * 1024

D_MODEL = 1024
HEAD = 64
PAIR = LANES // HEAD
RWKV_DIM = 512
DECAY_LORA = 64
ICLR_LORA = 64
GATE_LORA = 128
RWKV_COLS = 3 * RWKV_DIM + DECAY_LORA + ICLR_LORA + GATE_LORA
RWKV_GN_EPS = 64e-5
FOX_DIM = 512
FOX_HEADS = 8
SSM_DIM = 1024
SSM_HEADS = 16
SSM_GROUPS = 2
SSM_STATE = 128
CONV_WIDTH = 4
SSM_CONV_DIM = SSM_DIM + 2 * SSM_GROUPS * SSM_STATE
NORM_EPS = 1e-5
N_EXPERTS = 32
N_EXPERT_GROUPS = 4
EXPERTS_PER_GROUP = N_EXPERTS // N_EXPERT_GROUPS
TOP_K = 2
D_EXPERT = 512
LN_EPS = 1e-5
LOG2E = math.log2(math.e)
OFF_FOX = RWKV_COLS
OFF_SSM = OFF_FOX + 3 * FOX_DIM + FOX_HEADS
OFF_GATE = OFF_SSM + SSM_DIM + SSM_CONV_DIM + SSM_HEADS

SMALL_ROWS = 32
DT_OFF = 8
FOX_WIDTH = 256
RWKV_CHUNK = 64
RWKV_GROUP = 8
SSD_CHUNK = 128
MOE_BLOCK = 256


def _bf(x):
    return x.astype(BF16)


def _dot(a, b):
    return jnp.dot(a, b, preferred_element_type=F32)


def _dot_nt(a, b):
    return lax.dot_general(a, b, (((1,), (1,)), ((), ())), preferred_element_type=F32)


def _dot_tn(a, b):
    return lax.dot_general(a, b, (((0,), (0,)), ((), ())), preferred_element_type=F32)


def _split(x, parts):
    out = []
    rem = x
    for _ in range(parts):
        hi = _bf(rem)
        out.append(hi)
        rem = rem - hi.astype(F32)
    return out


def _dot_sel_rhs(m01, x, parts=3):
    acc = None
    for part in _split(x, parts):
        t = _dot(m01, part)
        acc = t if acc is None else acc + t
    return acc


def _dot_sel_lhs(x, m01, parts=3):
    acc = None
    for part in _split(x, parts):
        t = _dot(part, m01)
        acc = t if acc is None else acc + t
    return acc


def _sigmoid(x):
    return 1.0 / (1.0 + jnp.exp(-x))


def _softplus(x):
    return jnp.maximum(x, 0.0) + jnp.log(1.0 + jnp.exp(-jnp.abs(x)))


def _log_sigmoid(x):
    return jnp.minimum(x, 0.0) - jnp.log(1.0 + jnp.exp(-jnp.abs(x)))


def _iota2(shape, dim):
    return lax.broadcasted_iota(jnp.int32, shape, dim)


def _params(*sem):
    return pltpu.CompilerParams(dimension_semantics=sem, vmem_limit_bytes=VMEM_LIMIT)


def _mm_kernel(x_ref, w_ref, o_ref):
    o_ref[...] = _dot(x_ref[...], w_ref[...]).astype(o_ref.dtype)


def _matmul(x, w, tm, tn, out_dtype=F32):
    n, k = x.shape
    nc = w.shape[1]
    return pl.pallas_call(
        _mm_kernel,
        grid=(n // tm, nc // tn),
        in_specs=[pl.BlockSpec((tm, k), lambda i, j: (i, 0)),
                  pl.BlockSpec((k, tn), lambda i, j: (0, j))],
        out_specs=pl.BlockSpec((tm, tn), lambda i, j: (i, j)),
        out_shape=jax.ShapeDtypeStruct((n, nc), out_dtype),
        compiler_params=_params("parallel", "parallel"),
        name="in_proj",
    )(x, w)


def _mm_nt_kernel(wt_ref, x_ref, o_ref):
    o_ref[...] = _dot_nt(wt_ref[...], x_ref[...])


def _matmul_nt(wt, x, tm):
    r, k = wt.shape
    n = x.shape[0]
    return pl.pallas_call(
        _mm_nt_kernel,
        grid=(n // tm,),
        in_specs=[pl.BlockSpec((r, k), lambda i: (0, 0)),
                  pl.BlockSpec((tm, k), lambda i: (i, 0))],
        out_specs=pl.BlockSpec((r, tm), lambda i: (0, i)),
        out_shape=jax.ShapeDtypeStruct((r, n), F32),
        compiler_params=_params("parallel"),
        name="in_proj_small_t",
    )(wt, x)


def _rwkv_prep_kernel(p_ref, mu_ref, w0_ref, wup_ref, a0_ref, aup_ref, gup_ref, kk_ref, ka_ref,
                      bd_ref, r_o, lw_o, k_o, v_o, kn_o, b_o, g_o, carry_ref):
    @pl.when(pl.program_id(1) == 0)
    def _():
        carry_ref[...] = jnp.zeros_like(carry_ref)

    p = p_ref[...].astype(F32)
    ts = p.shape[0]
    row = _iota2((ts, 1), 0)
    prev = jnp.where(row == 0, carry_ref[...], pltpu.roll(p, 1, axis=0))
    carry_ref[...] = p[ts - 1:ts, :]
    p = p + mu_ref[...] * (prev - p)

    R = RWKV_DIM
    r = p[:, 0:R]
    k = p[:, R:2 * R]
    v = p[:, 2 * R:3 * R]
    lora = p[:, 3 * R:3 * R + DECAY_LORA + ICLR_LORA]
    gd = p[:, 3 * R + DECAY_LORA + ICLR_LORA:]
    w_pre = w0_ref[...] + _dot(_bf(jnp.tanh(lora)), wup_ref[...])
    lw = -jnp.exp(_log_sigmoid(w_pre) - 0.5)
    a = _sigmoid(a0_ref[...] + _dot(_bf(lora), aup_ref[...]))
    g = _dot(_bf(_sigmoid(gd)), gup_ref[...])
    kk = k * kk_ref[...]
    ss = _dot_sel_lhs(kk * kk, bd_ref[...], parts=2)
    kn = kk / jnp.maximum(jnp.sqrt(ss), 1e-12)
    r_o[...] = _bf(r)
    lw_o[...] = lw
    k_o[...] = _bf(k * (1.0 + (a - 1.0) * ka_ref[...]))
    v_o[...] = _bf(v)
    kn_o[...] = _bf(kn)
    b_o[...] = _bf(kn * a)
    g_o[...] = _bf(g)


def _rwkv_prep(p, B, S, ts, mu, w0, wup, a0, aup, gup, k_k, k_a, bd):
    n = B * S
    ns = S // ts
    row_spec = lambda w: pl.BlockSpec((ts, w), lambda b, s: (b * ns + s, 0))
    full = lambda a: pl.BlockSpec(a.shape, lambda b, s: (0,) * a.ndim)
    out = lambda dt: jax.ShapeDtypeStruct((n, RWKV_DIM), dt)
    return pl.pallas_call(
        _rwkv_prep_kernel,
        grid=(B, ns),
        in_specs=[row_spec(RWKV_COLS)] + [full(a) for a in (mu, w0, wup, a0, aup, gup, k_k, k_a, bd)],
        out_specs=[row_spec(RWKV_DIM)] * 7,
        out_shape=[out(BF16), out(F32)] + [out(BF16)] * 5,
        scratch_shapes=[pltpu.VMEM((1, RWKV_COLS), F32)],
        compiler_params=_params("parallel", "arbitrary"),
        name="rwkv_prep",
    )(p, mu, w0, wup, a0, aup, gup, k_k, k_a, bd)


def _rwkv_chunk_kernel(r_ref, lw_ref, k_ref, v_ref, kn_ref, b_ref, g_ref, gng_ref, gnb_ref, rk_ref,
                       o_ref, s_ref, *, n_chunks):
    C = RWKV_CHUNK

    @pl.when(pl.program_id(2) == 0)
    def _():
        s_ref[...] = jnp.zeros_like(s_ref)

    lane = _iota2((1, LANES), 1)
    m0 = (lane < HEAD).astype(F32)
    m1 = 1.0 - m0
    ri = _iota2((LANES, LANES), 0)
    ci = _iota2((LANES, LANES), 1)
    same = (ri >= HEAD) == (ci >= HEAD)
    smask = jnp.where(same & (ri > ci), 1.0, 0.0)
    imask = jnp.where(same & (ri >= ci), 1.0, 0.0)
    eye = jnp.where(ri == ci, 1.0, 0.0)
    bd_mean = _bf(jnp.where(same, 1.0 / HEAD, 0.0))
    bd_one = _bf(jnp.where(same, 1.0, 0.0))
    tri = _bf(jnp.where(_iota2((C, C), 0) >= _iota2((C, C), 1), 1.0, 0.0))

    def stack(x):
        return jnp.concatenate([x * m0, x * m1], axis=0)

    gng = gng_ref[...]
    gnb = gnb_ref[...]
    rk = rk_ref[...]

    def group(chunks):
        ld = lambda ref, c: ref[c * C:(c + 1) * C, :]
        r = [ld(r_ref, c).astype(F32) for c in chunks]
        lw = [ld(lw_ref, c) for c in chunks]
        k = [ld(k_ref, c).astype(F32) for c in chunks]
        v = [ld(v_ref, c).astype(F32) for c in chunks]
        kn = [ld(kn_ref, c).astype(F32) for c in chunks]
        b = [ld(b_ref, c).astype(F32) for c in chunks]
        idx = range(len(chunks))

        cs = [_dot_sel_rhs(tri, lw[i]) for i in idx]
        Rs, As, Ks, Bs, Rt, At, Be, Ke, Vs, g_end = ([] for _ in range(10))
        for i in idx:
            sh = cs[i][C // 2 - 1:C // 2, :]
            cs_end = cs[i][C - 1:C, :]
            e_pos = jnp.exp(cs[i] - sh)
            e_neg = jnp.exp(sh - cs[i])
            e_prev = jnp.exp(cs[i] - lw[i] - sh)
            e_end = jnp.exp(cs_end - cs[i])
            Rs.append(_bf(stack(r[i] * e_pos)))
            As.append(_bf(stack(-kn[i] * e_prev)))
            Ks.append(_bf(stack(k[i] * e_neg)))
            Bs.append(_bf(stack(b[i] * e_neg)))
            Rt.append(stack(r[i] * jnp.exp(cs[i])))
            At.append(_bf(stack(-kn[i] * jnp.exp(cs[i] - lw[i]))))
            Be.append(_bf(stack(b[i] * e_end)))
            Ke.append(_bf(stack(k[i] * e_end)))
            Vs.append(_bf(stack(v[i])))
            g_end.append(jnp.exp(cs_end))

        lanes2 = lambda x, y: jnp.concatenate([x, y], axis=1)
        L = LANES
        BK = [jnp.concatenate([Bs[i], Ks[i]], axis=0) for i in idx]
        GA = [_dot_nt(As[i], BK[i]) for i in idx]
        GR = [_dot_nt(Rs[i], BK[i]) for i in idx]
        Aab = [GA[i][:, :L] * smask for i in idx]
        Aak = [_bf(GA[i][:, L:] * smask) for i in idx]
        RBb = [_bf(GR[i][:, :L] * imask) for i in idx]
        RKb = [_bf(GR[i][:, L:] * imask) for i in idx]
        W1b = [_bf(_dot(Aak[i], Vs[i])) for i in idx]

        T = [eye + Aab[i] for i in idx]
        Pb = [_bf(Aab[i]) for i in idx]
        Pb = [_bf(_dot(Pb[i], Pb[i])) for i in idx]
        for _ in range(int(math.log2(C)) - 2):
            PT = [_dot(Pb[i], lanes2(Pb[i], _bf(T[i]))) for i in idx]
            Pb = [_bf(PT[i][:, :L]) for i in idx]
            T = [T[i] + PT[i][:, L:] for i in idx]
        T = [T[i] + _dot(Pb[i], _bf(T[i])) for i in idx]
        Tb = [_bf(T[i]) for i in idx]

        TU = [_dot(Tb[i], lanes2(At[i], W1b[i])) for i in idx]
        TUb = [_bf(TU[i]) for i in idx]
        TAb = [TUb[i][:, :L] for i in idx]
        U0b = [TUb[i][:, L:] for i in idx]
        RTU = [_dot(RBb[i], TUb[i]) for i in idx]
        Q = [_bf(Rt[i] + RTU[i][:, :L]) for i in idx]
        Y0 = [_dot(RKb[i], Vs[i]) + RTU[i][:, L:] for i in idx]
        Mt = [_bf(_dot_tn(TAb[i], Be[i])) for i in idx]
        Nt = [_dot_tn(Vs[i], Ke[i]) + _dot_tn(U0b[i], Be[i]) for i in idx]

        Y = []
        S0 = s_ref[...]
        for i in idx:
            S0b = _bf(S0)
            Y.append(_dot_nt(Q[i], S0b) + Y0[i])
            S0 = S0 * g_end[i] + _dot(S0b, Mt[i]) + Nt[i]
        s_ref[...] = S0

        rows = slice(chunks[0] * C, (chunks[-1] + 1) * C)
        cat = lambda xs: jnp.concatenate(xs, axis=0)
        y = cat([Y[i][:C, :] + Y[i][C:, :] for i in idx])
        r_all, k_all, v_all = cat(r), cat(k), cat(v)
        mean = _dot_sel_lhs(y, bd_mean, parts=2)
        yc = y - mean
        var = _dot_sel_lhs(yc * yc, bd_mean, parts=2)
        yn = yc * lax.rsqrt(var + RWKV_GN_EPS) * gng + gnb
        bonus = _dot_sel_lhs(r_all * k_all * rk, bd_one, parts=2) * v_all
        o_ref[rows, :] = _bf((yn + bonus) * g_ref[rows, :].astype(F32))

    for first in range(0, n_chunks, RWKV_GROUP):
        group(list(range(first, min(first + RWKV_GROUP, n_chunks))))


def _rwkv_chunk(r, lw, k, v, kn, b, g, gn_g, gn_b, r_k, B, S, n_chunks):
    n = B * S
    rows = n_chunks * RWKV_CHUNK
    ns = S // rows
    npair = RWKV_DIM // LANES
    blk = pl.BlockSpec((rows, LANES), lambda bb, p, s: (bb * ns + s, p))
    par = pl.BlockSpec((1, LANES), lambda bb, p, s: (0, p))
    return pl.pallas_call(
        functools.partial(_rwkv_chunk_kernel, n_chunks=n_chunks),
        grid=(B, npair, ns),
        in_specs=[blk] * 7 + [par] * 3,
        out_specs=blk,
        out_shape=jax.ShapeDtypeStruct((n, RWKV_DIM), BF16),
        scratch_shapes=[pltpu.VMEM((LANES, LANES), F32)],
        compiler_params=_params("parallel", "parallel", "arbitrary"),
        name="rwkv_chunk",
    )(r, lw, k, v, kn, b, g, gn_g, gn_b, r_k)


def _fox_c_kernel(ps_ref, pst_ref, bl_ref, bc_ref, cc_ref, cr_ref, *, seq):
    L = LANES
    ri = _iota2((L, L), 0)
    ci = _iota2((L, L), 1)
    tril = _bf(jnp.where(ri >= ci, 1.0, 0.0))
    triu = _bf(jnp.where(ri <= ci, 1.0, 0.0))
    carry = jnp.zeros((1, L), F32)
    carry_c = jnp.zeros((SMALL_ROWS, 1), F32)
    for i in range(seq // L):
        rows = slice(i * L, (i + 1) * L)
        cs = _dot_sel_rhs(tril, _log_sigmoid(ps_ref[rows, :] + bl_ref[...])) + carry
        cc_ref[rows, :] = cs
        carry = cs[L - 1:L, :]
        cst = _dot_sel_lhs(_log_sigmoid(pst_ref[:, rows] + bc_ref[...]), triu) + carry_c
        cr_ref[:, rows] = cst
        carry_c = cst[:, L - 1:L]


def _fox_c(ps, pst, bias_lane, bias_col, B, S):
    n = B * S
    return pl.pallas_call(
        functools.partial(_fox_c_kernel, seq=S),
        grid=(B,),
        in_specs=[pl.BlockSpec((S, LANES), lambda b: (b, 0)),
                  pl.BlockSpec((SMALL_ROWS, S), lambda b: (0, b)),
                  pl.BlockSpec((1, LANES), lambda b: (0, 0)),
                  pl.BlockSpec((SMALL_ROWS, 1), lambda b: (0, 0))],
        out_specs=[pl.BlockSpec((S, LANES), lambda b: (b, 0)),
                   pl.BlockSpec((SMALL_ROWS, S), lambda b: (0, b))],
        out_shape=[jax.ShapeDtypeStruct((n, LANES), F32),
                   jax.ShapeDtypeStruct((SMALL_ROWS, n), F32)],
        compiler_params=_params("parallel"),
        name="fox_cumgate",
    )(ps, pst, bias_lane, bias_col)


def _transpose_blocks(x):
    n = x.shape[0] // LANES
    return jnp.concatenate([x[i * LANES:(i + 1) * LANES, :].T for i in range(n)], axis=1)


def _fox_attn_kernel(q_ref, k_ref, v_ref, cc_ref, cr_ref, o_ref, vt_ref, ck_ref, s_scr, p_scr, *, tq, seq):
    nblk = q_ref.shape[1] // LANES
    heads = nblk * PAIR
    head0 = pl.program_id(1) * heads
    qi = pl.program_id(2)

    @pl.when(qi == 0)
    def _():
        for i in range(seq // LANES):
            rows = slice(i * LANES, (i + 1) * LANES)
            for c in range(nblk):
                cols = slice(c * LANES, (c + 1) * LANES)
                vt_ref[cols, rows] = _bf(v_ref[rows, cols].astype(F32).T)
        ri = _iota2((LANES, LANES), 0)
        for h in range(heads):
            sel = _bf(jnp.where(ri == head0 + h, 1.0, 0.0))
            ck_ref[h] = _dot_sel_lhs(cc_ref[...], sel) * LOG2E

    in0 = _iota2((LANES, 1), 0) < HEAD
    qts = []
    for c in range(nblk):
        qt = _transpose_blocks(q_ref[:, c * LANES:(c + 1) * LANES].astype(F32) * (HEAD ** -0.5 * LOG2E))
        qts += [_bf(jnp.where(in0, qt, 0.0)), _bf(jnp.where(in0, 0.0, qt))]
    q0 = pl.multiple_of(qi * tq, tq)
    cqs = tuple(cr_ref[pl.ds(head0 + h, 1), pl.ds(q0, tq)] * LOG2E for h in range(heads))
    key_minus_query = _iota2((tq, tq), 0) - _iota2((tq, tq), 1)
    last_block = seq // tq - 1

    s_scr[...] = jnp.full(s_scr.shape, -jnp.inf, F32)
    p_scr[...] = jnp.zeros(p_scr.shape, BF16)

    def step(j, carry, masked):
        blk_pv = jnp.clip(j - 2, 0, last_block)
        blk_qk = jnp.minimum(j, last_block)
        pv0 = pl.multiple_of(blk_pv * tq, tq)
        qk0 = pl.multiple_of(blk_qk * tq, tq)
        kbs = [k_ref[pl.ds(qk0, tq), c * LANES:(c + 1) * LANES] for c in range(nblk)]
        if masked:
            visible = key_minus_query <= (qi - j) * tq
        new = []
        for h in range(heads):
            m, l, alpha, acc = carry[h]
            vt = vt_ref[h * HEAD:(h + 1) * HEAD, pl.ds(pv0, tq)]
            acc = alpha * acc + _dot(vt, p_scr[h])
            s = s_scr[h]
            m_new = jnp.maximum(m, jnp.max(s, axis=0, keepdims=True))
            p = jnp.exp2(s - m_new)
            alpha = jnp.exp2(m - m_new)
            l = alpha * l + jnp.sum(p, axis=0, keepdims=True)
            p_scr[h] = _bf(p)
            ck = ck_ref[h, pl.ds(qk0, tq), :]
            s_next = _dot(kbs[h // PAIR], qts[h]) + cqs[h] - jnp.concatenate([ck] * (tq // LANES), axis=1)
            s_scr[h] = jnp.where(visible, s_next, -jnp.inf) if masked else s_next
            new.append((m_new, l, alpha, acc))
        return tuple(new)

    init = tuple((jnp.full((1, tq), -1e30, F32), jnp.zeros((1, tq), F32), jnp.ones((1, tq), F32),
                  jnp.zeros((HEAD, tq), F32)) for _ in range(heads))
    carry = lax.fori_loop(0, qi, lambda j, cr: step(j, cr, False), init)
    carry = lax.fori_loop(qi, qi + 3, lambda j, cr: step(j, cr, True), carry)
    ot = jnp.concatenate([acc / l for (_, l, _, acc) in carry], axis=0)
    o_ref[...] = jnp.concatenate(
        [jnp.concatenate([ot[c * LANES:(c + 1) * LANES, i * LANES:(i + 1) * LANES].T for c in range(nblk)], axis=1)
         for i in range(tq // LANES)], axis=0).astype(o_ref.dtype)


def _fox_attn(qkv, cc, cr, B, S, tq):
    n = B * S
    nq = S // tq
    w = FOX_WIDTH
    nw = FOX_DIM // w
    heads = w // HEAD
    return pl.pallas_call(
        functools.partial(_fox_attn_kernel, tq=tq, seq=S),
        grid=(B, nw, nq),
        in_specs=[pl.BlockSpec((tq, w), lambda b, p, i: (b * nq + i, p)),
                  pl.BlockSpec((S, w), lambda b, p, i: (b, nw + p)),
                  pl.BlockSpec((S, w), lambda b, p, i: (b, 2 * nw + p)),
                  pl.BlockSpec((S, LANES), lambda b, p, i: (b, 0)),
                  pl.BlockSpec((SMALL_ROWS, S), lambda b, p, i: (0, b))],
        out_specs=pl.BlockSpec((tq, w), lambda b, p, i: (b * nq + i, p)),
        out_shape=jax.ShapeDtypeStruct((n, FOX_DIM), BF16),
        scratch_shapes=[pltpu.VMEM((w, S), BF16), pltpu.VMEM((heads, S, LANES), F32),
                        pltpu.VMEM((heads, tq, tq), F32), pltpu.VMEM((heads, tq, tq), BF16)],
        compiler_params=_params("parallel", "parallel", "arbitrary"),
        name="fox_attn",
    )(qkv, qkv, qkv, cc, cr)


def _mamba_kernel(z_ref, xbc_ref, ps_ref, pst_ref, cw_ref, cb_ref, dtbl_ref, dtbc_ref, al_ref, ac_ref,
                  dl_ref, ng_ref, ex_ref, o_ref, halo_ref, st_ref):
    L = SSD_CHUNK

    @pl.when(pl.program_id(1) == 0)
    def _():
        halo_ref[...] = jnp.zeros_like(halo_ref)
        st_ref[...] = jnp.zeros_like(st_ref)

    xbc = xbc_ref[...].astype(F32)
    halo = halo_ref[...]
    row8 = _iota2((SUBLANES, 1), 0)
    acc = cb_ref[...] + cw_ref[CONV_WIDTH - 1:CONV_WIDTH, :] * xbc
    for sft in range(1, CONV_WIDTH):
        rolled = pltpu.roll(xbc, sft, axis=0)
        top = jnp.where(row8 < sft, pltpu.roll(halo, sft, axis=0), rolled[:SUBLANES, :])
        shifted = jnp.concatenate([top, rolled[SUBLANES:, :]], axis=0)
        acc = acc + cw_ref[CONV_WIDTH - 1 - sft:CONV_WIDTH - sft, :] * shifted
    halo_ref[...] = xbc[L - SUBLANES:, :]
    act = acc * _sigmoid(acc)
    xs = act[:, :SSM_DIM]
    Bm = act[:, SSM_DIM:SSM_DIM + SSM_GROUPS * SSM_STATE]
    Cm = act[:, SSM_DIM + SSM_GROUPS * SSM_STATE:]

    ri = _iota2((L, L), 0)
    ci = _iota2((L, L), 1)
    lower = ri >= ci
    tril = _bf(jnp.where(lower, 1.0, 0.0))
    triu = _bf(jnp.where(ri <= ci, 1.0, 0.0))
    lane = _iota2((1, LANES), 1)
    in0 = lane < HEAD

    dt_full = _dot_sel_lhs(_softplus(ps_ref[...] + dtbl_ref[...]), ex_ref[...])
    cs = _dot_sel_rhs(tril, dt_full * al_ref[...])
    cs_end = cs[L - 1:L, :]
    ecs = jnp.exp(cs)
    X = xs * dt_full
    Xe = X * jnp.exp(cs_end - cs)
    csr = _dot_sel_lhs(_softplus(pst_ref[...] + dtbc_ref[...]) * ac_ref[...], triu)

    heads_per_group = SSM_HEADS // SSM_GROUPS
    pairs_per_group = heads_per_group // PAIR
    ys = []
    for g in range(SSM_GROUPS):
        Bg = Bm[:, g * SSM_STATE:(g + 1) * SSM_STATE]
        Cg = _bf(Cm[:, g * SSM_STATE:(g + 1) * SSM_STATE])
        CB = _dot_nt(Cg, _bf(Bg))
        BgT = _bf(Bg.T)
        for pp in range(pairs_per_group):
            p = g * pairs_per_group + pp
            cols = slice(p * LANES, (p + 1) * LANES)
            Xp = X[:, cols]
            ydiag = None
            for hh in range(PAIR):
                h = PAIR * p + hh
                decay = jnp.exp(cs[:, h * HEAD:h * HEAD + 1] - csr[DT_OFF + h:DT_OFF + h + 1, :])
                Mh = _bf(CB * jnp.where(lower, decay, 0.0))
                Xh = _bf(jnp.where(in0, Xp, 0.0) if hh == 0 else jnp.where(in0, 0.0, Xp))
                t = _dot(Mh, Xh)
                ydiag = t if ydiag is None else ydiag + t
            st = st_ref[p]
            yoff = _dot(Cg, _bf(st)) * ecs[:, cols]
            st_ref[p] = st * ecs[L - 1:L, cols] + _dot(BgT, _bf(Xe[:, cols]))
            ys.append(ydiag + yoff + dl_ref[:, cols] * xs[:, cols])
    z = z_ref[...].astype(F32)
    y = jnp.concatenate(ys, axis=1) * (z * _sigmoid(z))
    gw = SSM_DIM // SSM_GROUPS
    outs = []
    for g in range(SSM_GROUPS):
        yg = y[:, g * gw:(g + 1) * gw]
        ms = jnp.mean(yg * yg, axis=-1, keepdims=True)
        outs.append(yg * lax.rsqrt(ms + NORM_EPS))
    o_ref[...] = _bf(jnp.concatenate(outs, axis=1) * ng_ref[...])


def _mamba(z, xbc, ps, pst, cw, cb, dtbl, dtbc, al, ac, dl, ng, ex, B, S):
    n = B * S
    L = SSD_CHUNK
    nc = S // L
    row = lambda w: pl.BlockSpec((L, w), lambda b, c: (b * nc + c, 0))
    full = lambda a: pl.BlockSpec(a.shape, lambda b, c: (0,) * a.ndim)
    return pl.pallas_call(
        _mamba_kernel,
        grid=(B, nc),
        in_specs=[row(SSM_DIM), row(SSM_CONV_DIM), row(LANES),
                  pl.BlockSpec((SMALL_ROWS, L), lambda b, c: (0, b * nc + c))]
                 + [full(a) for a in (cw, cb, dtbl, dtbc, al, ac, dl, ng, ex)],
        out_specs=row(SSM_DIM),
        out_shape=jax.ShapeDtypeStruct((n, SSM_DIM), BF16),
        scratch_shapes=[pltpu.VMEM((SUBLANES, SSM_CONV_DIM), F32),
                        pltpu.VMEM((SSM_HEADS // PAIR, SSM_STATE, LANES), F32)],
        compiler_params=_params("parallel", "arbitrary"),
        name="mamba_ssd",
    )(z, xbc, ps, pst, cw, cb, dtbl, dtbc, al, ac, dl, ng, ex)


def _layer_norm(h, g, b):
    mu = jnp.mean(h, axis=-1, keepdims=True)
    hc = h - mu
    var = jnp.mean(hc * hc, axis=-1, keepdims=True)
    return hc * lax.rsqrt(var + LN_EPS) * g + b


def _merge_kernel(or_ref, of_ref, om_ref, pg_ref, x_ref, wr_ref, wf_ref, wm_ref, wo_ref, gb_ref,
                  lg_ref, lb_ref, rwh_ref, rwl_ref, x1_ref, lgt_ref, *, alpha):
    D = D_MODEL
    gate = lambda i: _sigmoid(pg_ref[:, i * D:(i + 1) * D].astype(F32) + gb_ref[:, i * D:(i + 1) * D])
    merged = (gate(0) * _dot(or_ref[...], wr_ref[...])
              + gate(1) * _dot(of_ref[...], wf_ref[...])
              + gate(2) * _dot(om_ref[...], wm_ref[...]))
    mix = _dot(_bf(merged), wo_ref[...])
    x1 = _layer_norm(alpha * x_ref[...] + mix, lg_ref[...], lb_ref[...])
    x1_ref[...] = x1
    xh = _bf(x1)
    xl = _bf(x1 - xh.astype(F32))
    lgt_ref[...] = (_dot_nt(rwh_ref[...], xh) + _dot_nt(rwh_ref[...], xl) + _dot_nt(rwl_ref[...], xh))


def _merge(o_r, o_f, o_m, pg, x, wr, wf, wm, wo, gb, lg, lb, rwh, rwl, tm, alpha):
    n = x.shape[0]
    row = lambda w: pl.BlockSpec((tm, w), lambda i: (i, 0))
    full = lambda a: pl.BlockSpec(a.shape, lambda i: (0,) * a.ndim)
    return pl.pallas_call(
        functools.partial(_merge_kernel, alpha=alpha),
        grid=(n // tm,),
        in_specs=[row(RWKV_DIM), row(FOX_DIM), row(SSM_DIM), row(3 * D_MODEL), row(D_MODEL)]
                 + [full(a) for a in (wr, wf, wm, wo, gb, lg, lb, rwh, rwl)],
        out_specs=[row(D_MODEL), pl.BlockSpec((N_EXPERTS, tm), lambda i: (0, i))],
        out_shape=[jax.ShapeDtypeStruct((n, D_MODEL), F32),
                   jax.ShapeDtypeStruct((N_EXPERTS, n), F32)],
        compiler_params=_params("parallel"),
        name="merge_ln1_router",
    )(o_r, o_f, o_m, pg, x, wr, wf, wm, wo, gb, lg, lb, rwh, rwl)


def _rows(*vals):
    t = vals[0].shape[1]
    sub = _iota2((SUBLANES, t), 0)
    out = jnp.zeros((SUBLANES, t), vals[0].dtype)
    for i, v in enumerate(vals):
        out = jnp.where(sub == i, v, out)
    return out


def _route_select_kernel(lgt_ref, bias_ref, e_ref, g_ref, rank_ref, cnt_ref, carry_ref):
    @pl.when(pl.program_id(0) == 0)
    def _():
        carry_ref[...] = jnp.zeros_like(carry_ref)

    t = lgt_ref.shape[1]
    aff = _sigmoid(lgt_ref[...])
    sel = aff + bias_ref[...]
    gsz = EXPERTS_PER_GROUP
    sub = _iota2((gsz, t), 0).astype(F32)
    best = e1 = e2 = None
    for g in range(N_EXPERT_GROUPS):
        s = sel[g * gsz:(g + 1) * gsz, :]
        m1 = jnp.max(s, axis=0, keepdims=True)
        i1 = jnp.min(jnp.where(s == m1, sub, float(gsz)), axis=0, keepdims=True)
        s2 = jnp.where(sub == i1, -jnp.inf, s)
        m2 = jnp.max(s2, axis=0, keepdims=True)
        i2 = jnp.min(jnp.where(s2 == m2, sub, float(gsz)), axis=0, keepdims=True)
        score = m1 + m2
        if g == 0:
            best, e1, e2 = score, i1, i2
        else:
            better = score > best
            best = jnp.where(better, score, best)
            e1 = jnp.where(better, i1 + float(g * gsz), e1)
            e2 = jnp.where(better, i2 + float(g * gsz), e2)
    row = _iota2((N_EXPERTS, t), 0).astype(F32)
    oh1 = jnp.where(row == e1, 1.0, 0.0)
    oh2 = jnp.where(row == e2, 1.0, 0.0)
    w1 = jnp.sum(oh1 * aff, axis=0, keepdims=True)
    w2 = jnp.sum(oh2 * aff, axis=0, keepdims=True)
    den = w1 + w2
    cnt = oh1 + oh2
    before = _bf(jnp.where(_iota2((t, t), 0) < _iota2((t, t), 1), 1.0, 0.0))
    prefix = _dot(_bf(cnt), before) + carry_ref[...]
    r1 = jnp.sum(oh1 * prefix, axis=0, keepdims=True)
    r2 = jnp.sum(oh2 * prefix, axis=0, keepdims=True)
    carry_ref[...] = carry_ref[...] + jnp.sum(cnt, axis=1, keepdims=True)
    e_ref[...] = _rows(e1, e2)
    g_ref[...] = _rows(w1 / den, w2 / den)
    rank_ref[...] = _rows(r1, r2)
    cnt_ref[...] = jnp.broadcast_to(carry_ref[...], cnt_ref.shape)


def _route_select(lgt, bias_col, tile):
    n = lgt.shape[1]
    col = lambda r: pl.BlockSpec((r, tile), lambda i: (0, i))
    out = jax.ShapeDtypeStruct((SUBLANES, n), F32)
    return pl.pallas_call(
        _route_select_kernel,
        grid=(n // tile,),
        in_specs=[col(N_EXPERTS), pl.BlockSpec((N_EXPERTS, 1), lambda i: (0, 0))],
        out_specs=[col(SUBLANES)] * 3 + [pl.BlockSpec((N_EXPERTS, LANES), lambda i: (0, 0))],
        out_shape=[out] * 3 + [jax.ShapeDtypeStruct((N_EXPERTS, LANES), F32)],
        scratch_shapes=[pltpu.VMEM((N_EXPERTS, 1), F32)],
        compiler_params=_params("arbitrary"),
        name="route_select",
    )(lgt, bias_col)


def _route_place_kernel(e_ref, rank_ref, cnt_ref, dest_ref, be_ref, nu_ref):
    t = e_ref.shape[1]
    blk = float(MOE_BLOCK)
    counts = cnt_ref[...]
    padded = jnp.floor((counts + (blk - 1.0)) / blk) * blk
    ne = N_EXPERTS
    lower = _bf(jnp.where(_iota2((ne, ne), 0) > _iota2((ne, ne), 1), 1.0, 0.0))
    pad_start = _dot_sel_rhs(lower, padded)
    pad_end = pad_start + padded
    row = _iota2((ne, t), 0).astype(F32)
    ps_t = jnp.concatenate([pad_start] * (t // LANES), axis=1)
    e = e_ref[...]
    rank = rank_ref[...]
    dests = []
    for k in range(TOP_K):
        oh = jnp.where(row == e[k:k + 1, :], 1.0, 0.0)
        dests.append(jnp.sum(oh * ps_t, axis=0, keepdims=True) + rank[k:k + 1, :])
    dest_ref[...] = _rows(*dests).astype(jnp.int32)
    nbp = be_ref.shape[1]
    blk_start = _iota2((ne, nbp), 1).astype(F32) * blk
    pe_t = jnp.concatenate([pad_end] * (nbp // LANES), axis=1)
    be = jnp.sum(jnp.where(pe_t <= blk_start, 1.0, 0.0), axis=0, keepdims=True)
    be_ref[...] = jnp.broadcast_to(jnp.minimum(be, float(ne - 1)), be_ref.shape).astype(jnp.int32)
    nu_ref[...] = jnp.broadcast_to(pad_end[ne - 1:ne, :] / blk, nu_ref.shape).astype(jnp.int32)


def _route_place(e, rank, cnt, tile, n_blocks):
    n = e.shape[1]
    nbp = -(-n_blocks // LANES) * LANES
    col = pl.BlockSpec((SUBLANES, tile), lambda i: (0, i))
    return pl.pallas_call(
        _route_place_kernel,
        grid=(n // tile,),
        in_specs=[col, col, pl.BlockSpec((N_EXPERTS, LANES), lambda i: (0, 0))],
        out_specs=[col, pl.BlockSpec((SUBLANES, nbp), lambda i: (0, 0)),
                   pl.BlockSpec((SUBLANES, LANES), lambda i: (0, 0))],
        out_shape=[jax.ShapeDtypeStruct((SUBLANES, n), jnp.int32),
                   jax.ShapeDtypeStruct((SUBLANES, nbp), jnp.int32),
                   jax.ShapeDtypeStruct((SUBLANES, LANES), jnp.int32)],
        compiler_params=_params("arbitrary"),
        name="route_place",
    )(e, rank, cnt)


def _moe_kernel(be_ref, nu_ref, xb_ref, wg_ref, wu_ref, wd_ref, yb_ref, wgb, wub, wdb):
    i = pl.program_id(0)
    new_expert = jnp.logical_or(i == 0, be_ref[i] != be_ref[jnp.maximum(i - 1, 0)])

    @pl.when(new_expert)
    def _():
        wgb[...] = _bf(wg_ref[0])
        wub[...] = _bf(wu_ref[0])
        wdb[...] = _bf(wd_ref[0])

    @pl.when(i < nu_ref[0])
    def _():
        x = _bf(xb_ref[...])
        gt = _dot(x, wgb[...])
        h = gt * _sigmoid(gt) * _dot(x, wub[...])
        yb_ref[...] = _dot(_bf(h), wdb[...])

    @pl.when(i >= nu_ref[0])
    def _():
        yb_ref[...] = jnp.zeros_like(yb_ref)


def _moe(block_expert, n_used, xb, wg, wu, wd):
    rows, d = xb.shape
    blk = MOE_BLOCK
    nb = rows // blk
    grid_spec = pltpu.PrefetchScalarGridSpec(
        num_scalar_prefetch=2,
        grid=(nb,),
        in_specs=[pl.BlockSpec((blk, d), lambda i, be, nu: (i, 0)),
                  pl.BlockSpec((1, d, D_EXPERT), lambda i, be, nu: (be[i], 0, 0)),
                  pl.BlockSpec((1, d, D_EXPERT), lambda i, be, nu: (be[i], 0, 0)),
                  pl.BlockSpec((1, D_EXPERT, d), lambda i, be, nu: (be[i], 0, 0))],
        out_specs=pl.BlockSpec((blk, d), lambda i, be, nu: (i, 0)),
        scratch_shapes=[pltpu.VMEM((d, D_EXPERT), BF16), pltpu.VMEM((d, D_EXPERT), BF16),
                        pltpu.VMEM((D_EXPERT, d), BF16)])
    return pl.pallas_call(
        _moe_kernel,
        grid_spec=grid_spec,
        out_shape=jax.ShapeDtypeStruct((rows, d), F32),
        compiler_params=_params("arbitrary"),
        name="moe_experts",
    )(block_expert, n_used, xb, wg, wu, wd)


def _dispatch_kernel(dest_ref, x_ref, buf_in_hbm, buf_hbm, sem, *, tile, n):
    del buf_in_hbm
    base = pl.program_id(0) * tile

    def row_copy(j, k):
        return pltpu.make_async_copy(x_ref.at[pl.ds(j, 1)],
                                     buf_hbm.at[pl.ds(dest_ref[k * n + base + j], 1)], sem)

    def start(j, carry):
        for k in range(TOP_K):
            row_copy(j, k).start()
        return carry

    def wait(j, carry):
        for k in range(TOP_K):
            row_copy(j, k).wait()
        return carry

    lax.fori_loop(0, tile, start, 0)
    lax.fori_loop(0, tile, wait, 0)


def _dispatch(dest_flat, x, rows, tile):
    n, d = x.shape
    any_spec = pl.BlockSpec(memory_space=pl.ANY)
    grid_spec = pltpu.PrefetchScalarGridSpec(
        num_scalar_prefetch=1,
        grid=(n // tile,),
        in_specs=[pl.BlockSpec((tile, d), lambda i, dest: (i, 0)), any_spec],
        out_specs=any_spec,
        scratch_shapes=[pltpu.SemaphoreType.DMA(())])
    return pl.pallas_call(
        functools.partial(_dispatch_kernel, tile=tile, n=n),
        grid_spec=grid_spec,
        out_shape=jax.ShapeDtypeStruct((rows, d), x.dtype),
        input_output_aliases={2: 0},
        compiler_params=_params("arbitrary"),
        name="moe_dispatch",
    )(dest_flat, x, jnp.zeros((rows, d), x.dtype))


def _combine_ln2_kernel(dest_ref, x_ref, gate_ref, yb_hbm, g_ref, b_ref, o_ref, ob_ref, buf, sems,
                        *, tile, n, alpha):
    i = pl.program_id(0)
    steps = pl.num_programs(0)

    def row_copy(step, slot, j, k):
        t = step * tile + j
        return pltpu.make_async_copy(yb_hbm.at[pl.ds(dest_ref[k * n + t], 1)],
                                     buf.at[slot, k, pl.ds(j, 1)], sems.at[slot])

    def start_tile(step, slot):
        def body(j, carry):
            for k in range(TOP_K):
                row_copy(step, slot, j, k).start()
            return carry
        lax.fori_loop(0, tile, body, 0)

    def wait_tile(step, slot):
        def body(j, carry):
            for k in range(TOP_K):
                row_copy(step, slot, j, k).wait()
            return carry
        lax.fori_loop(0, tile, body, 0)

    slot = lax.rem(i, 2)

    @pl.when(i == 0)
    def _():
        start_tile(0, 0)

    for s in range(2):
        @pl.when(slot == s)
        def _():
            @pl.when(i + 1 < steps)
            def _():
                start_tile(i + 1, 1 - s)
            wait_tile(i, s)

    gate = gate_ref[...]
    ffn = gate[:, 0:1] * buf[slot, 0] + gate[:, 1:2] * buf[slot, 1]
    y = _layer_norm(alpha * x_ref[...] + ffn, g_ref[...], b_ref[...])
    o_ref[...] = y
    ob_ref[...] = _bf(y)


def _combine_ln2(dest_flat, x, gate, yb, g, b, tile, alpha):
    n, d = x.shape
    row = pl.BlockSpec((tile, d), lambda i, dest: (i, 0))
    par = pl.BlockSpec((1, d), lambda i, dest: (0, 0))
    grid_spec = pltpu.PrefetchScalarGridSpec(
        num_scalar_prefetch=1,
        grid=(n // tile,),
        in_specs=[row, pl.BlockSpec((tile, TOP_K), lambda i, dest: (i, 0)),
                  pl.BlockSpec(memory_space=pl.ANY), par, par],
        out_specs=[row, row],
        scratch_shapes=[pltpu.VMEM((2, TOP_K, tile, d), F32), pltpu.SemaphoreType.DMA((2,))])
    return pl.pallas_call(
        functools.partial(_combine_ln2_kernel, tile=tile, n=n, alpha=alpha),
        grid_spec=grid_spec,
        out_shape=[jax.ShapeDtypeStruct((n, d), F32), jax.ShapeDtypeStruct((n, d), BF16)],
        compiler_params=_params("arbitrary"),
        name="moe_combine_ln2",
    )(dest_flat, x, gate, yb, g, b)


def _route(lgt, bias_col, tile):
    n = lgt.shape[1]
    n_blocks = -(-(n * TOP_K) // MOE_BLOCK) + N_EXPERTS
    e, g, rank, cnt = _route_select(lgt, bias_col, tile)
    dest, be, nu = _route_place(e, rank, cnt, tile, n_blocks)
    return g[:TOP_K].T, dest[:TOP_K].reshape(-1), be[0, :n_blocks], nu[0, :1], n_blocks


def _proj_tn(width):
    blocks = width // LANES
    best = max(d for d in range(1, blocks + 1) if blocks % d == 0 and d * LANES <= 1024)
    return best * LANES


def _tiles(B, S):
    n = B * S
    return dict(
        tm_proj=min(1024, n),
        ts_prep=min(512, S),
        rwkv_chunks=min(8, S // RWKV_CHUNK),
        tq=min(256, S),
        tm_merge=min(512, n),
        t_route=min(512, n),
        t_dispatch=min(512, n),
        t_combine=min(256, n),
    )


def _pad_rows(a, before, total):
    return jnp.zeros((total, a.shape[1]), a.dtype).at[before:before + a.shape[0]].set(a)


def _block_diag_ones(dim):
    i = jnp.arange(dim) // HEAD
    return (i[:, None] == i[None, :]).astype(BF16)


def kernel(x, w_in, rwkv_mu, rwkv_w0, rwkv_w_up, rwkv_a0, rwkv_a_up, rwkv_g_up, rwkv_k_k, rwkv_k_a,
           rwkv_r_k, rwkv_gn_g, rwkv_gn_b, fox_f_bias, ssm_conv_w, ssm_conv_b, ssm_dt_bias, ssm_a_log,
           ssm_d, ssm_norm_g, gate_bias, w_br_rwkv, w_br_fox, w_br_ssm, w_out, ln1_g, ln1_b, router_w,
           router_bias, exp_w_gate, exp_w_up, exp_w_down, ln2_g, ln2_b):
    B, S, D = x.shape
    depth = w_in.shape[0]
    n = B * S
    t = _tiles(B, S)
    alpha = (2 * depth) ** 0.25

    row1 = lambda a: a.reshape(1, -1).astype(F32)
    bd512 = _block_diag_ones(RWKV_DIM)
    expand = (jnp.arange(LANES)[:, None] == DT_OFF + jnp.arange(SSM_DIM)[None, :] // HEAD).astype(BF16)
    rw = router_w.astype(F32).T
    rwh = rw.astype(BF16)
    rwl = (rw - rwh.astype(F32)).astype(BF16)
    rb_col = router_bias.astype(F32).reshape(N_EXPERTS, 1)

    xf = x.reshape(n, D).astype(F32)
    xb16 = xf.astype(BF16)
    for l in range(depth):
        w = w_in[l]
        w_small = jnp.concatenate([w[:, OFF_SSM - FOX_HEADS:OFF_SSM], w[:, OFF_GATE - SSM_HEADS:OFF_GATE]], axis=1)
        w_small_l = jnp.zeros((D, LANES), F32).at[:, :w_small.shape[1]].set(w_small).astype(BF16)
        w_small_t = jnp.zeros((SMALL_ROWS, D), F32).at[:w_small.shape[1]].set(w_small.T).astype(BF16)
        mm = lambda lo, hi: _matmul(xb16, w[:, lo:hi].astype(BF16), t["tm_proj"], _proj_tn(hi - lo), BF16)
        p_rwkv = mm(0, RWKV_COLS)
        p_fox = mm(OFF_FOX, OFF_FOX + 3 * FOX_DIM)
        p_z = mm(OFF_SSM, OFF_SSM + SSM_DIM)
        p_xbc = mm(OFF_SSM + SSM_DIM, OFF_SSM + SSM_DIM + SSM_CONV_DIM)
        p_gate = mm(OFF_GATE, OFF_GATE + 3 * D)
        ps = _matmul(xb16, w_small_l, t["tm_proj"], LANES)
        pst = _matmul_nt(w_small_t, xb16, t["tm_proj"])

        wup = _pad_rows(rwkv_w_up[l], 0, DECAY_LORA + ICLR_LORA).astype(BF16)
        aup = _pad_rows(rwkv_a_up[l], DECAY_LORA, DECAY_LORA + ICLR_LORA).astype(BF16)
        r, lw, k, v, kn, b, g = _rwkv_prep(
            p_rwkv, B, S, t["ts_prep"], row1(rwkv_mu[l]), row1(rwkv_w0[l]), wup, row1(rwkv_a0[l]), aup,
            rwkv_g_up[l].astype(BF16), row1(rwkv_k_k[l]), row1(rwkv_k_a[l]), bd512)
        o_r = _rwkv_chunk(r, lw, k, v, kn, b, g, row1(rwkv_gn_g[l]), row1(rwkv_gn_b[l]),
                          row1(rwkv_r_k[l]), B, S, t["rwkv_chunks"])

        fb_lane = jnp.zeros((1, LANES), F32).at[0, :FOX_HEADS].set(fox_f_bias[l].astype(F32))
        fb_col = jnp.zeros((SMALL_ROWS, 1), F32).at[:FOX_HEADS, 0].set(fox_f_bias[l].astype(F32))
        cc, cr = _fox_c(ps, pst, fb_lane, fb_col, B, S)
        o_f = _fox_attn(p_fox, cc, cr, B, S, t["tq"])

        dtb = ssm_dt_bias[l].astype(F32)
        a_neg = -jnp.exp(ssm_a_log[l].astype(F32))
        dtbl = jnp.zeros((1, LANES), F32).at[0, DT_OFF:DT_OFF + SSM_HEADS].set(dtb)
        dtbc = jnp.zeros((SMALL_ROWS, 1), F32).at[DT_OFF:DT_OFF + SSM_HEADS, 0].set(dtb)
        ac = jnp.zeros((SMALL_ROWS, 1), F32).at[DT_OFF:DT_OFF + SSM_HEADS, 0].set(a_neg)
        al = jnp.repeat(a_neg, HEAD).reshape(1, SSM_DIM)
        dl = jnp.repeat(ssm_d[l].astype(F32), HEAD).reshape(1, SSM_DIM)
        o_m = _mamba(p_z, p_xbc, ps, pst, ssm_conv_w[l].astype(F32), row1(ssm_conv_b[l]), dtbl, dtbc,
                     al, ac, dl, row1(ssm_norm_g[l]), expand, B, S)

        x1, logits = _merge(
            o_r, o_f, o_m, p_gate, xf, w_br_rwkv[l].astype(BF16), w_br_fox[l].astype(BF16),
            w_br_ssm[l].astype(BF16), w_out[l].astype(BF16), row1(gate_bias[l]), row1(ln1_g[l]),
            row1(ln1_b[l]), rwh, rwl, t["tm_merge"], alpha)

        gate, dest, block_expert, n_used, n_blocks = _route(logits, rb_col, t["t_route"])
        xb = _dispatch(dest, x1, n_blocks * MOE_BLOCK, t["t_dispatch"])
        yb = _moe(block_expert, n_used, xb, exp_w_gate[l], exp_w_up[l], exp_w_down[l])
        xf, xb16 = _combine_ln2(dest, x1, gate, yb, row1(ln2_g[l]), row1(ln2_b[l]), t["t_combine"], alpha)
    return xf.reshape(B, S, D).astype(x.dtype)
```

```python
import functools
import math

import jax
import jax.numpy as jnp
from jax import lax
from jax.experimental import pallas as pl
from jax.experimental.pallas import tpu as pltpu

F32 = jnp.float32
BF16 = jnp.bfloat16

LANES = 128
SUBLANES = 8
VMEM_LIMIT = 48 * 1024 * 1024

D_MODEL = 1024
HEAD = 64
PAIR = LANES // HEAD
RWKV_DIM = 512
DECAY_LORA = 64
ICLR_LORA = 64
GATE_LORA = 128
RWKV_COLS = 3 * RWKV_DIM + DECAY_LORA + ICLR_LORA + GATE_LORA
RWKV_GN_EPS = 64e-5
FOX_DIM = 512
FOX_HEADS = 8
SSM_DIM = 1024
SSM_HEADS = 16
SSM_GROUPS = 2
SSM_STATE = 128
CONV_WIDTH = 4
SSM_CONV_DIM = SSM_DIM + 2 * SSM_GROUPS * SSM_STATE
NORM_EPS = 1e-5
N_EXPERTS = 32
N_EXPERT_GROUPS = 4
EXPERTS_PER_GROUP = N_EXPERTS // N_EXPERT_GROUPS
TOP_K = 2
D_EXPERT = 512
LN_EPS = 1e-5
LOG2E = math.log2(math.e)
OFF_FOX = RWKV_COLS
OFF_SSM = OFF_FOX + 3 * FOX_DIM + FOX_HEADS
OFF_GATE = OFF_SSM + SSM_DIM + SSM_CONV_DIM + SSM_HEADS

SMALL_ROWS = 32
DT_OFF = 8
FOX_WIDTH = 256
RWKV_CHUNK = 64
RWKV_GROUP = 8
SSD_CHUNK = 128
MOE_BLOCK = 256


def _bf(x):
    return x.astype(BF16)


def _dot(a, b):
    return jnp.dot(a, b, preferred_element_type=F32)


def _dot_nt(a, b):
    return lax.dot_general(a, b, (((1,), (1,)), ((), ())), preferred_element_type=F32)


def _dot_tn(a, b):
    return lax.dot_general(a, b, (((0,), (0,)), ((), ())), preferred_element_type=F32)


def _split(x, parts):
    out = []
    rem = x
    for _ in range(parts):
        hi = _bf(rem)
        out.append(hi)
        rem = rem - hi.astype(F32)
    return out


def _dot_sel_rhs(m01, x, parts=3):
    acc = None
    for part in _split(x, parts):
        t = _dot(m01, part)
        acc = t if acc is None else acc + t
    return acc


def _dot_sel_lhs(x, m01, parts=3):
    acc = None
    for part in _split(x, parts):
        t = _dot(part, m01)
        acc = t if acc is None else acc + t
    return acc


def _sigmoid(x):
    return 1.0 / (1.0 + jnp.exp(-x))


def _softplus(x):
    return jnp.maximum(x, 0.0) + jnp.log(1.0 + jnp.exp(-jnp.abs(x)))


def _log_sigmoid(x):
    return jnp.minimum(x, 0.0) - jnp.log(1.0 + jnp.exp(-jnp.abs(x)))


def _iota2(shape, dim):
    return lax.broadcasted_iota(jnp.int32, shape, dim)


def _params(*sem):
    return pltpu.CompilerParams(dimension_semantics=sem, vmem_limit_bytes=VMEM_LIMIT)


def _mm_kernel(x_ref, w_ref, o_ref):
    o_ref[...] = _dot(x_ref[...], w_ref[...]).astype(o_ref.dtype)


def _matmul(x, w, tm, tn, out_dtype=F32):
    n, k = x.shape
    nc = w.shape[1]
    return pl.pallas_call(
        _mm_kernel,
        grid=(n // tm, nc // tn),
        in_specs=[pl.BlockSpec((tm, k), lambda i, j: (i, 0)),
                  pl.BlockSpec((k, tn), lambda i, j: (0, j))],
        out_specs=pl.BlockSpec((tm, tn), lambda i, j: (i, j)),
        out_shape=jax.ShapeDtypeStruct((n, nc), out_dtype),
        compiler_params=_params("parallel", "parallel"),
        name="in_proj",
    )(x, w)


def _mm_nt_kernel(wt_ref, x_ref, o_ref):
    o_ref[...] = _dot_nt(wt_ref[...], x_ref[...])


def _matmul_nt(wt, x, tm):
    r, k = wt.shape
    n = x.shape[0]
    return pl.pallas_call(
        _mm_nt_kernel,
        grid=(n // tm,),
        in_specs=[pl.BlockSpec((r, k), lambda i: (0, 0)),
                  pl.BlockSpec((tm, k), lambda i: (i, 0))],
        out_specs=pl.BlockSpec((r, tm), lambda i: (0, i)),
        out_shape=jax.ShapeDtypeStruct((r, n), F32),
        compiler_params=_params("parallel"),
        name="in_proj_small_t",
    )(wt, x)


def _rwkv_prep_kernel(p_ref, mu_ref, w0_ref, wup_ref, a0_ref, aup_ref, gup_ref, kk_ref, ka_ref,
                      bd_ref, r_o, lw_o, k_o, v_o, kn_o, b_o, g_o, carry_ref):
    @pl.when(pl.program_id(1) == 0)
    def _():
        carry_ref[...] = jnp.zeros_like(carry_ref)

    p = p_ref[...].astype(F32)
    ts = p.shape[0]
    row = _iota2((ts, 1), 0)
    prev = jnp.where(row == 0, carry_ref[...], pltpu.roll(p, 1, axis=0))
    carry_ref[...] = p[ts - 1:ts, :]
    p = p + mu_ref[...] * (prev - p)

    R = RWKV_DIM
    r = p[:, 0:R]
    k = p[:, R:2 * R]
    v = p[:, 2 * R:3 * R]
    lora = p[:, 3 * R:3 * R + DECAY_LORA + ICLR_LORA]
    gd = p[:, 3 * R + DECAY_LORA + ICLR_LORA:]
    w_pre = w0_ref[...] + _dot(_bf(jnp.tanh(lora)), wup_ref[...])
    lw = -jnp.exp(_log_sigmoid(w_pre) - 0.5)
    a = _sigmoid(a0_ref[...] + _dot(_bf(lora), aup_ref[...]))
    g = _dot(_bf(_sigmoid(gd)), gup_ref[...])
    kk = k * kk_ref[...]
    ss = _dot_sel_lhs(kk * kk, bd_ref[...], parts=2)
    kn = kk / jnp.maximum(jnp.sqrt(ss), 1e-12)
    r_o[...] = _bf(r)
    lw_o[...] = lw
    k_o[...] = _bf(k * (1.0 + (a - 1.0) * ka_ref[...]))
    v_o[...] = _bf(v)
    kn_o[...] = _bf(kn)
    b_o[...] = _bf(kn * a)
    g_o[...] = _bf(g)


def _rwkv_prep(p, B, S, ts, mu, w0, wup, a0, aup, gup, k_k, k_a, bd):
    n = B * S
    ns = S // ts
    row_spec = lambda w: pl.BlockSpec((ts, w), lambda b, s: (b * ns + s, 0))
    full = lambda a: pl.BlockSpec(a.shape, lambda b, s: (0,) * a.ndim)
    out = lambda dt: jax.ShapeDtypeStruct((n, RWKV_DIM), dt)
    return pl.pallas_call(
        _rwkv_prep_kernel,
        grid=(B, ns),
        in_specs=[row_spec(RWKV_COLS)] + [full(a) for a in (mu, w0, wup, a0, aup, gup, k_k, k_a, bd)],
        out_specs=[row_spec(RWKV_DIM)] * 7,
        out_shape=[out(BF16), out(F32)] + [out(BF16)] * 5,
        scratch_shapes=[pltpu.VMEM((1, RWKV_COLS), F32)],
        compiler_params=_params("parallel", "arbitrary"),
        name="rwkv_prep",
    )(p, mu, w0, wup, a0, aup, gup, k_k, k_a, bd)


def _rwkv_chunk_kernel(r_ref, lw_ref, k_ref, v_ref, kn_ref, b_ref, g_ref, gng_ref, gnb_ref, rk_ref,
                       o_ref, s_ref, *, n_chunks):
    C = RWKV_CHUNK

    @pl.when(pl.program_id(2) == 0)
    def _():
        s_ref[...] = jnp.zeros_like(s_ref)

    lane = _iota2((1, LANES), 1)
    m0 = (lane < HEAD).astype(F32)
    m1 = 1.0 - m0
    ri = _iota2((LANES, LANES), 0)
    ci = _iota2((LANES, LANES), 1)
    same = (ri >= HEAD) == (ci >= HEAD)
    smask = jnp.where(same & (ri > ci), 1.0, 0.0)
    imask = jnp.where(same & (ri >= ci), 1.0, 0.0)
    eye = jnp.where(ri == ci, 1.0, 0.0)
    bd_mean = _bf(jnp.where(same, 1.0 / HEAD, 0.0))
    bd_one = _bf(jnp.where(same, 1.0, 0.0))
    tri = _bf(jnp.where(_iota2((C, C), 0) >= _iota2((C, C), 1), 1.0, 0.0))

    def stack(x):
        return jnp.concatenate([x * m0, x * m1], axis=0)

    gng = gng_ref[...]
    gnb = gnb_ref[...]
    rk = rk_ref[...]

    def group(chunks):
        ld = lambda ref, c: ref[c * C:(c + 1) * C, :]
        r = [ld(r_ref, c).astype(F32) for c in chunks]
        lw = [ld(lw_ref, c) for c in chunks]
        k = [ld(k_ref, c).astype(F32) for c in chunks]
        v = [ld(v_ref, c).astype(F32) for c in chunks]
        kn = [ld(kn_ref, c).astype(F32) for c in chunks]
        b = [ld(b_ref, c).astype(F32) for c in chunks]
        idx = range(len(chunks))

        cs = [_dot_sel_rhs(tri, lw[i]) for i in idx]
        Rs, As, Ks, Bs, Rt, At, Be, Ke, Vs, g_end = ([] for _ in range(10))
        for i in idx:
            sh = cs[i][C // 2 - 1:C // 2, :]
            cs_end = cs[i][C - 1:C, :]
            e_pos = jnp.exp(cs[i] - sh)
            e_neg = jnp.exp(sh - cs[i])
            e_prev = jnp.exp(cs[i] - lw[i] - sh)
            e_end = jnp.exp(cs_end - cs[i])
            Rs.append(_bf(stack(r[i] * e_pos)))
            As.append(_bf(stack(-kn[i] * e_prev)))
            Ks.append(_bf(stack(k[i] * e_neg)))
            Bs.append(_bf(stack(b[i] * e_neg)))
            Rt.append(stack(r[i] * jnp.exp(cs[i])))
            At.append(_bf(stack(-kn[i] * jnp.exp(cs[i] - lw[i]))))
            Be.append(_bf(stack(b[i] * e_end)))
            Ke.append(_bf(stack(k[i] * e_end)))
            Vs.append(_bf(stack(v[i])))
            g_end.append(jnp.exp(cs_end))

        lanes2 = lambda x, y: jnp.concatenate([x, y], axis=1)
        L = LANES
        BK = [jnp.concatenate([Bs[i], Ks[i]], axis=0) for i in idx]
        GA = [_dot_nt(As[i], BK[i]) for i in idx]
        GR = [_dot_nt(Rs[i], BK[i]) for i in idx]
        Aab = [GA[i][:, :L] * smask for i in idx]
        Aak = [_bf(GA[i][:, L:] * smask) for i in idx]
        RBb = [_bf(GR[i][:, :L] * imask) for i in idx]
        RKb = [_bf(GR[i][:, L:] * imask) for i in idx]
        W1b = [_bf(_dot(Aak[i], Vs[i])) for i in idx]

        T = [eye + Aab[i] for i in idx]
        Pb = [_bf(Aab[i]) for i in idx]
        Pb = [_bf(_dot(Pb[i], Pb[i])) for i in idx]
        for _ in range(int(math.log2(C)) - 2):
            PT = [_dot(Pb[i], lanes2(Pb[i], _bf(T[i]))) for i in idx]
            Pb = [_bf(PT[i][:, :L]) for i in idx]
            T = [T[i] + PT[i][:, L:] for i in idx]
        T = [T[i] + _dot(Pb[i], _bf(T[i])) for i in idx]
        Tb = [_bf(T[i]) for i in idx]

        TU = [_dot(Tb[i], lanes2(At[i], W1b[i])) for i in idx]
        TUb = [_bf(TU[i]) for i in idx]
        TAb = [TUb[i][:, :L] for i in idx]
        U0b = [TUb[i][:, L:] for i in idx]
        RTU = [_dot(RBb[i], TUb[i]) for i in idx]
        Q = [_bf(Rt[i] + RTU[i][:, :L]) for i in idx]
        Y0 = [_dot(RKb[i], Vs[i]) + RTU[i][:, L:] for i in idx]
        Mt = [_bf(_dot_tn(TAb[i], Be[i])) for i in idx]
        Nt = [_dot_tn(Vs[i], Ke[i]) + _dot_tn(U0b[i], Be[i]) for i in idx]

        Y = []
        S0 = s_ref[...]
        for i in idx:
            S0b = _bf(S0)
            Y.append(_dot_nt(Q[i], S0b) + Y0[i])
            S0 = S0 * g_end[i] + _dot(S0b, Mt[i]) + Nt[i]
        s_ref[...] = S0

        rows = slice(chunks[0] * C, (chunks[-1] + 1) * C)
        cat = lambda xs: jnp.concatenate(xs, axis=0)
        y = cat([Y[i][:C, :] + Y[i][C:, :] for i in idx])
        r_all, k_all, v_all = cat(r), cat(k), cat(v)
        mean = _dot_sel_lhs(y, bd_mean, parts=2)
        yc = y - mean
        var = _dot_sel_lhs(yc * yc, bd_mean, parts=2)
        yn = yc * lax.rsqrt(var + RWKV_GN_EPS) * gng + gnb
        bonus = _dot_sel_lhs(r_all * k_all * rk, bd_one, parts=2) * v_all
        o_ref[rows, :] = _bf((yn + bonus) * g_ref[rows, :].astype(F32))

    for first in range(0, n_chunks, RWKV_GROUP):
        group(list(range(first, min(first + RWKV_GROUP, n_chunks))))


def _rwkv_chunk(r, lw, k, v, kn, b, g, gn_g, gn_b, r_k, B, S, n_chunks):
    n = B * S
    rows = n_chunks * RWKV_CHUNK
    ns = S // rows
    npair = RWKV_DIM // LANES
    blk = pl.BlockSpec((rows, LANES), lambda bb, p, s: (bb * ns + s, p))
    par = pl.BlockSpec((1, LANES), lambda bb, p, s: (0, p))
    return pl.pallas_call(
        functools.partial(_rwkv_chunk_kernel, n_chunks=n_chunks),
        grid=(B, npair, ns),
        in_specs=[blk] * 7 + [par] * 3,
        out_specs=blk,
        out_shape=jax.ShapeDtypeStruct((n, RWKV_DIM), BF16),
        scratch_shapes=[pltpu.VMEM((LANES, LANES), F32)],
        compiler_params=_params("parallel", "parallel", "arbitrary"),
        name="rwkv_chunk",
    )(r, lw, k, v, kn, b, g, gn_g, gn_b, r_k)


def _fox_c_kernel(ps_ref, pst_ref, bl_ref, bc_ref, cc_ref, cr_ref, *, seq):
    L = LANES
    ri = _iota2((L, L), 0)
    ci = _iota2((L, L), 1)
    tril = _bf(jnp.where(ri >= ci, 1.0, 0.0))
    triu = _bf(jnp.where(ri <= ci, 1.0, 0.0))
    carry = jnp.zeros((1, L), F32)
    carry_c = jnp.zeros((SMALL_ROWS, 1), F32)
    for i in range(seq // L):
        rows = slice(i * L, (i + 1) * L)
        cs = _dot_sel_rhs(tril, _log_sigmoid(ps_ref[rows, :] + bl_ref[...])) + carry
        cc_ref[rows, :] = cs
        carry = cs[L - 1:L, :]
        cst = _dot_sel_lhs(_log_sigmoid(pst_ref[:, rows] + bc_ref[...]), triu) + carry_c
        cr_ref[:, rows] = cst
        carry_c = cst[:, L - 1:L]


def _fox_c(ps, pst, bias_lane, bias_col, B, S):
    n = B * S
    return pl.pallas_call(
        functools.partial(_fox_c_kernel, seq=S),
        grid=(B,),
        in_specs=[pl.BlockSpec((S, LANES), lambda b: (b, 0)),
                  pl.BlockSpec((SMALL_ROWS, S), lambda b: (0, b)),
                  pl.BlockSpec((1, LANES), lambda b: (0, 0)),
                  pl.BlockSpec((SMALL_ROWS, 1), lambda b: (0, 0))],
        out_specs=[pl.BlockSpec((S, LANES), lambda b: (b, 0)),
                   pl.BlockSpec((SMALL_ROWS, S), lambda b: (0, b))],
        out_shape=[jax.ShapeDtypeStruct((n, LANES), F32),
                   jax.ShapeDtypeStruct((SMALL_ROWS, n), F32)],
        compiler_params=_params("parallel"),
        name="fox_cumgate",
    )(ps, pst, bias_lane, bias_col)


def _transpose_blocks(x):
    n = x.shape[0] // LANES
    return jnp.concatenate([x[i * LANES:(i + 1) * LANES, :].T for i in range(n)], axis=1)


def _fox_attn_kernel(q_ref, k_ref, v_ref, cc_ref, cr_ref, o_ref, vt_ref, ck_ref, s_scr, p_scr, *, tq, seq):
    nblk = q_ref.shape[1] // LANES
    heads = nblk * PAIR
    head0 = pl.program_id(1) * heads
    qi = pl.program_id(2)

    @pl.when(qi == 0)
    def _():
        for i in range(seq // LANES):
            rows = slice(i * LANES, (i + 1) * LANES)
            for c in range(nblk):
                cols = slice(c * LANES, (c + 1) * LANES)
                vt_ref[cols, rows] = _bf(v_ref[rows, cols].astype(F32).T)
        ri = _iota2((LANES, LANES), 0)
        for h in range(heads):
            sel = _bf(jnp.where(ri == head0 + h, 1.0, 0.0))
            ck_ref[h] = _dot_sel_lhs(cc_ref[...], sel) * LOG2E

    in0 = _iota2((LANES, 1), 0) < HEAD
    qts = []
    for c in range(nblk):
        qt = _transpose_blocks(q_ref[:, c * LANES:(c + 1) * LANES].astype(F32) * (HEAD ** -0.5 * LOG2E))
        qts += [_bf(jnp.where(in0, qt, 0.0)), _bf(jnp.where(in0, 0.0, qt))]
    q0 = pl.multiple_of(qi * tq, tq)
    cqs = tuple(cr_ref[pl.ds(head0 + h, 1), pl.ds(q0, tq)] * LOG2E for h in range(heads))
    key_minus_query = _iota2((tq, tq), 0) - _iota2((tq, tq), 1)
    last_block = seq // tq - 1

    s_scr[...] = jnp.full(s_scr.shape, -jnp.inf, F32)
    p_scr[...] = jnp.zeros(p_scr.shape, BF16)

    def step(j, carry, diagonal=False, scores=True, softmax=True):
        pv0 = pl.multiple_of(jnp.clip(j - 2, 0, last_block) * tq, tq)
        if scores:
            qk0 = pl.multiple_of(j * tq, tq)
            kbs = [k_ref[pl.ds(qk0, tq), c * LANES:(c + 1) * LANES] for c in range(nblk)]
        new = []
        for h in range(heads):
            m, l, alpha, acc = carry[h]
            vt = vt_ref[h * HEAD:(h + 1) * HEAD, pl.ds(pv0, tq)]
            acc = alpha * acc + _dot(vt, p_scr[h])
            if softmax:
                s = s_scr[h]
                m_new = jnp.maximum(m, jnp.max(s, axis=0, keepdims=True))
                p = jnp.exp2(s - m_new)
                alpha = jnp.exp2(m - m_new)
                l = alpha * l + jnp.sum(p, axis=0, keepdims=True)
                m = m_new
                p_scr[h] = _bf(p)
            if scores:
                ck = ck_ref[h, pl.ds(qk0, tq), :]
                s_next = _dot(kbs[h // PAIR], qts[h]) + cqs[h] - jnp.concatenate([ck] * (tq // LANES), axis=1)
                s_scr[h] = jnp.where(key_minus_query <= 0, s_next, -jnp.inf) if diagonal else s_next
            new.append((m, l, alpha, acc))
        return tuple(new)

    init = tuple((jnp.full((1, tq), -1e30, F32), jnp.zeros((1, tq), F32), jnp.ones((1, tq), F32),
                  jnp.zeros((HEAD, tq), F32)) for _ in range(heads))
    carry = lax.fori_loop(0, qi, step, init)
    carry = step(qi, carry, diagonal=True)
    carry = step(qi + 1, carry, scores=False)
    carry = step(qi + 2, carry, scores=False, softmax=False)
    ot = jnp.concatenate([acc / l for (_, l, _, acc) in carry], axis=0)
    o_ref[...] = jnp.concatenate(
        [jnp.concatenate([ot[c * LANES:(c + 1) * LANES, i * LANES:(i + 1) * LANES].T for c in range(nblk)], axis=1)
         for i in range(tq // LANES)], axis=0).astype(o_ref.dtype)


def _fox_attn(qkv, cc, cr, B, S, tq):
    n = B * S
    nq = S // tq
    w = FOX_WIDTH
    nw = FOX_DIM // w
    heads = w // HEAD
    return pl.pallas_call(
        functools.partial(_fox_attn_kernel, tq=tq, seq=S),
        grid=(B, nw, nq),
        in_specs=[pl.BlockSpec((tq, w), lambda b, p, i: (b * nq + i, p)),
                  pl.BlockSpec((S, w), lambda b, p, i: (b, nw + p)),
                  pl.BlockSpec((S, w), lambda b, p, i: (b, 2 * nw + p)),
                  pl.BlockSpec((S, LANES), lambda b, p, i: (b, 0)),
                  pl.BlockSpec((SMALL_ROWS, S), lambda b, p, i: (0, b))],
        out_specs=pl.BlockSpec((tq, w), lambda b, p, i: (b * nq + i, p)),
        out_shape=jax.ShapeDtypeStruct((n, FOX_DIM), BF16),
        scratch_shapes=[pltpu.VMEM((w, S), BF16), pltpu.VMEM((heads, S, LANES), F32),
                        pltpu.VMEM((heads, tq, tq), F32), pltpu.VMEM((heads, tq, tq), BF16)],
        compiler_params=_params("parallel", "parallel", "arbitrary"),
        name="fox_attn",
    )(qkv, qkv, qkv, cc, cr)


def _mamba_kernel(z_ref, xbc_ref, ps_ref, pst_ref, cw_ref, cb_ref, dtbl_ref, dtbc_ref, al_ref, ac_ref,
                  dl_ref, ng_ref, ex_ref, o_ref, halo_ref, st_ref):
    L = SSD_CHUNK

    @pl.when(pl.program_id(1) == 0)
    def _():
        halo_ref[...] = jnp.zeros_like(halo_ref)
        st_ref[...] = jnp.zeros_like(st_ref)

    xbc = xbc_ref[...].astype(F32)
    halo = halo_ref[...]
    row8 = _iota2((SUBLANES, 1), 0)
    acc = cb_ref[...] + cw_ref[CONV_WIDTH - 1:CONV_WIDTH, :] * xbc
    for sft in range(1, CONV_WIDTH):
        rolled = pltpu.roll(xbc, sft, axis=0)
        top = jnp.where(row8 < sft, pltpu.roll(halo, sft, axis=0), rolled[:SUBLANES, :])
        shifted = jnp.concatenate([top, rolled[SUBLANES:, :]], axis=0)
        acc = acc + cw_ref[CONV_WIDTH - 1 - sft:CONV_WIDTH - sft, :] * shifted
    halo_ref[...] = xbc[L - SUBLANES:, :]
    act = acc * _sigmoid(acc)
    xs = act[:, :SSM_DIM]
    Bm = act[:, SSM_DIM:SSM_DIM + SSM_GROUPS * SSM_STATE]
    Cm = act[:, SSM_DIM + SSM_GROUPS * SSM_STATE:]

    ri = _iota2((L, L), 0)
    ci = _iota2((L, L), 1)
    lower = ri >= ci
    tril = _bf(jnp.where(lower, 1.0, 0.0))
    triu = _bf(jnp.where(ri <= ci, 1.0, 0.0))
    lane = _iota2((1, LANES), 1)
    in0 = lane < HEAD

    dt_full = _dot_sel_lhs(_softplus(ps_ref[...] + dtbl_ref[...]), ex_ref[...])
    cs = _dot_sel_rhs(tril, dt_full * al_ref[...])
    cs_end = cs[L - 1:L, :]
    ecs = jnp.exp(cs)
    X = xs * dt_full
    Xe = X * jnp.exp(cs_end - cs)
    csr = _dot_sel_lhs(_softplus(pst_ref[...] + dtbc_ref[...]) * ac_ref[...], triu)

    heads_per_group = SSM_HEADS // SSM_GROUPS
    pairs_per_group = heads_per_group // PAIR
    ys = []
    for g in range(SSM_GROUPS):
        Bg = Bm[:, g * SSM_STATE:(g + 1) * SSM_STATE]
        Cg = _bf(Cm[:, g * SSM_STATE:(g + 1) * SSM_STATE])
        CB = _dot_nt(Cg, _bf(Bg))
        BgT = _bf(Bg.T)
        for pp in range(pairs_per_group):
            p = g * pairs_per_group + pp
            cols = slice(p * LANES, (p + 1) * LANES)
            Xp = X[:, cols]
            ydiag = None
            for hh in range(PAIR):
                h = PAIR * p + hh
                decay = jnp.exp(cs[:, h * HEAD:h * HEAD + 1] - csr[DT_OFF + h:DT_OFF + h + 1, :])
                Mh = _bf(CB * jnp.where(lower, decay, 0.0))
                Xh = _bf(jnp.where(in0, Xp, 0.0) if hh == 0 else jnp.where(in0, 0.0, Xp))
                t = _dot(Mh, Xh)
                ydiag = t if ydiag is None else ydiag + t
            st = st_ref[p]
            yoff = _dot(Cg, _bf(st)) * ecs[:, cols]
            st_ref[p] = st * ecs[L - 1:L, cols] + _dot(BgT, _bf(Xe[:, cols]))
            ys.append(ydiag + yoff + dl_ref[:, cols] * xs[:, cols])
    z = z_ref[...].astype(F32)
    y = jnp.concatenate(ys, axis=1) * (z * _sigmoid(z))
    gw = SSM_DIM // SSM_GROUPS
    outs = []
    for g in range(SSM_GROUPS):
        yg = y[:, g * gw:(g + 1) * gw]
        ms = jnp.mean(yg * yg, axis=-1, keepdims=True)
        outs.append(yg * lax.rsqrt(ms + NORM_EPS))
    o_ref[...] = _bf(jnp.concatenate(outs, axis=1) * ng_ref[...])


def _mamba(z, xbc, ps, pst, cw, cb, dtbl, dtbc, al, ac, dl, ng, ex, B, S):
    n = B * S
    L = SSD_CHUNK
    nc = S // L
    row = lambda w: pl.BlockSpec((L, w), lambda b, c: (b * nc + c, 0))
    full = lambda a: pl.BlockSpec(a.shape, lambda b, c: (0,) * a.ndim)
    return pl.pallas_call(
        _mamba_kernel,
        grid=(B, nc),
        in_specs=[row(SSM_DIM), row(SSM_CONV_DIM), row(LANES),
                  pl.BlockSpec((SMALL_ROWS, L), lambda b, c: (0, b * nc + c))]
                 + [full(a) for a in (cw, cb, dtbl, dtbc, al, ac, dl, ng, ex)],
        out_specs=row(SSM_DIM),
        out_shape=jax.ShapeDtypeStruct((n, SSM_DIM), BF16),
        scratch_shapes=[pltpu.VMEM((SUBLANES, SSM_CONV_DIM), F32),
                        pltpu.VMEM((SSM_HEADS // PAIR, SSM_STATE, LANES), F32)],
        compiler_params=_params("parallel", "arbitrary"),
        name="mamba_ssd",
    )(z, xbc, ps, pst, cw, cb, dtbl, dtbc, al, ac, dl, ng, ex)


def _layer_norm(h, g, b):
    mu = jnp.mean(h, axis=-1, keepdims=True)
    hc = h - mu
    var = jnp.mean(hc * hc, axis=-1, keepdims=True)
    return hc * lax.rsqrt(var + LN_EPS) * g + b


def _merge_kernel(or_ref, of_ref, om_ref, pg_ref, x_ref, wr_ref, wf_ref, wm_ref, wo_ref, gb_ref,
                  lg_ref, lb_ref, rwh_ref, rwl_ref, x1_ref, lgt_ref, *, alpha):
    D = D_MODEL
    gate = lambda i: _sigmoid(pg_ref[:, i * D:(i + 1) * D].astype(F32) + gb_ref[:, i * D:(i + 1) * D])
    merged = (gate(0) * _dot(or_ref[...], wr_ref[...])
              + gate(1) * _dot(of_ref[...], wf_ref[...])
              + gate(2) * _dot(om_ref[...], wm_ref[...]))
    mix = _dot(_bf(merged), wo_ref[...])
    x1 = _layer_norm(alpha * x_ref[...] + mix, lg_ref[...], lb_ref[...])
    x1_ref[...] = x1
    xh = _bf(x1)
    xl = _bf(x1 - xh.astype(F32))
    lgt_ref[...] = (_dot_nt(rwh_ref[...], xh) + _dot_nt(rwh_ref[...], xl) + _dot_nt(rwl_ref[...], xh))


def _merge(o_r, o_f, o_m, pg, x, wr, wf, wm, wo, gb, lg, lb, rwh, rwl, tm, alpha):
    n = x.shape[0]
    row = lambda w: pl.BlockSpec((tm, w), lambda i: (i, 0))
    full = lambda a: pl.BlockSpec(a.shape, lambda i: (0,) * a.ndim)
    return pl.pallas_call(
        functools.partial(_merge_kernel, alpha=alpha),
        grid=(n // tm,),
        in_specs=[row(RWKV_DIM), row(FOX_DIM), row(SSM_DIM), row(3 * D_MODEL), row(D_MODEL)]
                 + [full(a) for a in (wr, wf, wm, wo, gb, lg, lb, rwh, rwl)],
        out_specs=[row(D_MODEL), pl.BlockSpec((N_EXPERTS, tm), lambda i: (0, i))],
        out_shape=[jax.ShapeDtypeStruct((n, D_MODEL), F32),
                   jax.ShapeDtypeStruct((N_EXPERTS, n), F32)],
        compiler_params=_params("parallel"),
        name="merge_ln1_router",
    )(o_r, o_f, o_m, pg, x, wr, wf, wm, wo, gb, lg, lb, rwh, rwl)


def _rows(*vals):
    t = vals[0].shape[1]
    sub = _iota2((SUBLANES, t), 0)
    out = jnp.zeros((SUBLANES, t), vals[0].dtype)
    for i, v in enumerate(vals):
        out = jnp.where(sub == i, v, out)
    return out


def _route_select_kernel(lgt_ref, bias_ref, e_ref, g_ref, rank_ref, cnt_ref, carry_ref):
    @pl.when(pl.program_id(0) == 0)
    def _():
        carry_ref[...] = jnp.zeros_like(carry_ref)

    t = lgt_ref.shape[1]
    aff = _sigmoid(lgt_ref[...])
    sel = aff + bias_ref[...]
    gsz = EXPERTS_PER_GROUP
    sub = _iota2((gsz, t), 0).astype(F32)
    best = e1 = e2 = None
    for g in range(N_EXPERT_GROUPS):
        s = sel[g * gsz:(g + 1) * gsz, :]
        m1 = jnp.max(s, axis=0, keepdims=True)
        i1 = jnp.min(jnp.where(s == m1, sub, float(gsz)), axis=0, keepdims=True)
        s2 = jnp.where(sub == i1, -jnp.inf, s)
        m2 = jnp.max(s2, axis=0, keepdims=True)
        i2 = jnp.min(jnp.where(s2 == m2, sub, float(gsz)), axis=0, keepdims=True)
        score = m1 + m2
        if g == 0:
            best, e1, e2 = score, i1, i2
        else:
            better = score > best
            best = jnp.where(better, score, best)
            e1 = jnp.where(better, i1 + float(g * gsz), e1)
            e2 = jnp.where(better, i2 + float(g * gsz), e2)
    row = _iota2((N_EXPERTS, t), 0).astype(F32)
    oh1 = jnp.where(row == e1, 1.0, 0.0)
    oh2 = jnp.where(row == e2, 1.0, 0.0)
    w1 = jnp.sum(oh1 * aff, axis=0, keepdims=True)
    w2 = jnp.sum(oh2 * aff, axis=0, keepdims=True)
    den = w1 + w2
    cnt = oh1 + oh2
    before = _bf(jnp.where(_iota2((t, t), 0) < _iota2((t, t), 1), 1.0, 0.0))
    prefix = _dot(_bf(cnt), before) + carry_ref[...]
    r1 = jnp.sum(oh1 * prefix, axis=0, keepdims=True)
    r2 = jnp.sum(oh2 * prefix, axis=0, keepdims=True)
    carry_ref[...] = carry_ref[...] + jnp.sum(cnt, axis=1, keepdims=True)
    e_ref[...] = _rows(e1, e2)
    g_ref[...] = _rows(w1 / den, w2 / den)
    rank_ref[...] = _rows(r1, r2)
    cnt_ref[...] = jnp.broadcast_to(carry_ref[...], cnt_ref.shape)


def _route_select(lgt, bias_col, tile):
    n = lgt.shape[1]
    col = lambda r: pl.BlockSpec((r, tile), lambda i: (0, i))
    out = jax.ShapeDtypeStruct((SUBLANES, n), F32)
    return pl.pallas_call(
        _route_select_kernel,
        grid=(n // tile,),
        in_specs=[col(N_EXPERTS), pl.BlockSpec((N_EXPERTS, 1), lambda i: (0, 0))],
        out_specs=[col(SUBLANES)] * 3 + [pl.BlockSpec((N_EXPERTS, LANES), lambda i: (0, 0))],
        out_shape=[out] * 3 + [jax.ShapeDtypeStruct((N_EXPERTS, LANES), F32)],
        scratch_shapes=[pltpu.VMEM((N_EXPERTS, 1), F32)],
        compiler_params=_params("arbitrary"),
        name="route_select",
    )(lgt, bias_col)


def _route_place_kernel(e_ref, rank_ref, cnt_ref, dest_ref, be_ref, nu_ref):
    t = e_ref.shape[1]
    blk = float(MOE_BLOCK)
    counts = cnt_ref[...]
    padded = jnp.floor((counts + (blk - 1.0)) / blk) * blk
    ne = N_EXPERTS
    lower = _bf(jnp.where(_iota2((ne, ne), 0) > _iota2((ne, ne), 1), 1.0, 0.0))
    pad_start = _dot_sel_rhs(lower, padded)
    pad_end = pad_start + padded
    row = _iota2((ne, t), 0).astype(F32)
    ps_t = jnp.concatenate([pad_start] * (t // LANES), axis=1)
    e = e_ref[...]
    rank = rank_ref[...]
    dests = []
    for k in range(TOP_K):
        oh = jnp.where(row == e[k:k + 1, :], 1.0, 0.0)
        dests.append(jnp.sum(oh * ps_t, axis=0, keepdims=True) + rank[k:k + 1, :])
    dest_ref[...] = _rows(*dests).astype(jnp.int32)
    nbp = be_ref.shape[1]
    blk_start = _iota2((ne, nbp), 1).astype(F32) * blk
    pe_t = jnp.concatenate([pad_end] * (nbp // LANES), axis=1)
    be = jnp.sum(jnp.where(pe_t <= blk_start, 1.0, 0.0), axis=0, keepdims=True)
    be_ref[...] = jnp.broadcast_to(jnp.minimum(be, float(ne - 1)), be_ref.shape).astype(jnp.int32)
    nu_ref[...] = jnp.broadcast_to(pad_end[ne - 1:ne, :] / blk, nu_ref.shape).astype(jnp.int32)


def _route_place(e, rank, cnt, tile, n_blocks):
    n = e.shape[1]
    nbp = -(-n_blocks // LANES) * LANES
    col = pl.BlockSpec((SUBLANES, tile), lambda i: (0, i))
    return pl.pallas_call(
        _route_place_kernel,
        grid=(n // tile,),
        in_specs=[col, col, pl.BlockSpec((N_EXPERTS, LANES), lambda i: (0, 0))],
        out_specs=[col, pl.BlockSpec((SUBLANES, nbp), lambda i: (0, 0)),
                   pl.BlockSpec((SUBLANES, LANES), lambda i: (0, 0))],
        out_shape=[jax.ShapeDtypeStruct((SUBLANES, n), jnp.int32),
                   jax.ShapeDtypeStruct((SUBLANES, nbp), jnp.int32),
                   jax.ShapeDtypeStruct((SUBLANES, LANES), jnp.int32)],
        compiler_params=_params("arbitrary"),
        name="route_place",
    )(e, rank, cnt)


def _moe_kernel(be_ref, nu_ref, xb_ref, wg_ref, wu_ref, wd_ref, yb_ref, wgb, wub, wdb):
    i = pl.program_id(0)
    new_expert = jnp.logical_or(i == 0, be_ref[i] != be_ref[jnp.maximum(i - 1, 0)])

    @pl.when(new_expert)
    def _():
        wgb[...] = _bf(wg_ref[0])
        wub[...] = _bf(wu_ref[0])
        wdb[...] = _bf(wd_ref[0])

    @pl.when(i < nu_ref[0])
    def _():
        x = _bf(xb_ref[...])
        gt = _dot(x, wgb[...])
        h = gt * _sigmoid(gt) * _dot(x, wub[...])
        yb_ref[...] = _dot(_bf(h), wdb[...])

    @pl.when(i >= nu_ref[0])
    def _():
        yb_ref[...] = jnp.zeros_like(yb_ref)


def _moe(block_expert, n_used, xb, wg, wu, wd):
    rows, d = xb.shape
    blk = MOE_BLOCK
    nb = rows // blk
    grid_spec = pltpu.PrefetchScalarGridSpec(
        num_scalar_prefetch=2,
        grid=(nb,),
        in_specs=[pl.BlockSpec((blk, d), lambda i, be, nu: (i, 0)),
                  pl.BlockSpec((1, d, D_EXPERT), lambda i, be, nu: (be[i], 0, 0)),
                  pl.BlockSpec((1, d, D_EXPERT), lambda i, be, nu: (be[i], 0, 0)),
                  pl.BlockSpec((1, D_EXPERT, d), lambda i, be, nu: (be[i], 0, 0))],
        out_specs=pl.BlockSpec((blk, d), lambda i, be, nu: (i, 0)),
        scratch_shapes=[pltpu.VMEM((d, D_EXPERT), BF16), pltpu.VMEM((d, D_EXPERT), BF16),
                        pltpu.VMEM((D_EXPERT, d), BF16)])
    return pl.pallas_call(
        _moe_kernel,
        grid_spec=grid_spec,
        out_shape=jax.ShapeDtypeStruct((rows, d), F32),
        compiler_params=_params("arbitrary"),
        name="moe_experts",
    )(block_expert, n_used, xb, wg, wu, wd)


def _dispatch_kernel(dest_ref, x_ref, buf_in_hbm, buf_hbm, sem, *, tile, n):
    del buf_in_hbm
    base = pl.program_id(0) * tile

    def row_copy(j, k):
        return pltpu.make_async_copy(x_ref.at[pl.ds(j, 1)],
                                     buf_hbm.at[pl.ds(dest_ref[k * n + base + j], 1)], sem)

    def start(j, carry):
        for k in range(TOP_K):
            row_copy(j, k).start(priority=k)
        return carry

    def wait(j, carry):
        for k in range(TOP_K):
            row_copy(j, k).wait()
        return carry

    lax.fori_loop(0, tile, start, 0)
    lax.fori_loop(0, tile, wait, 0)


def _dispatch(dest_flat, x, rows, tile):
    n, d = x.shape
    any_spec = pl.BlockSpec(memory_space=pl.ANY)
    grid_spec = pltpu.PrefetchScalarGridSpec(
        num_scalar_prefetch=1,
        grid=(n // tile,),
        in_specs=[pl.BlockSpec((tile, d), lambda i, dest: (i, 0)), any_spec],
        out_specs=any_spec,
        scratch_shapes=[pltpu.SemaphoreType.DMA(())])
    return pl.pallas_call(
        functools.partial(_dispatch_kernel, tile=tile, n=n),
        grid_spec=grid_spec,
        out_shape=jax.ShapeDtypeStruct((rows, d), x.dtype),
        input_output_aliases={2: 0},
        compiler_params=_params("arbitrary"),
        name="moe_dispatch",
    )(dest_flat, x, jnp.zeros((rows, d), x.dtype))


def _combine_ln2_kernel(dest_ref, x_ref, gate_ref, yb_hbm, g_ref, b_ref, o_ref, ob_ref, buf, sems,
                        *, tile, n, alpha):
    i = pl.program_id(0)
    steps = pl.num_programs(0)

    def row_copy(step, slot, j, k):
        t = step * tile + j
        return pltpu.make_async_copy(yb_hbm.at[pl.ds(dest_ref[k * n + t], 1)],
                                     buf.at[slot, k, pl.ds(j, 1)], sems.at[slot])

    def start_tile(step, slot):
        def body(j, carry):
            for k in range(TOP_K):
                row_copy(step, slot, j, k).start(priority=k)
            return carry
        lax.fori_loop(0, tile, body, 0)

    def wait_tile(step, slot):
        def body(j, carry):
            for k in range(TOP_K):
                row_copy(step, slot, j, k).wait()
            return carry
        lax.fori_loop(0, tile, body, 0)

    slot = lax.rem(i, 2)

    @pl.when(i == 0)
    def _():
        start_tile(0, 0)

    for s in range(2):
        @pl.when(slot == s)
        def _():
            @pl.when(i + 1 < steps)
            def _():
                start_tile(i + 1, 1 - s)
            wait_tile(i, s)

    gate = gate_ref[...]
    ffn = gate[:, 0:1] * buf[slot, 0] + gate[:, 1:2] * buf[slot, 1]
    y = _layer_norm(alpha * x_ref[...] + ffn, g_ref[...], b_ref[...])
    o_ref[...] = y
    ob_ref[...] = _bf(y)


def _combine_ln2(dest_flat, x, gate, yb, g, b, tile, alpha):
    n, d = x.shape
    row = pl.BlockSpec((tile, d), lambda i, dest: (i, 0))
    par = pl.BlockSpec((1, d), lambda i, dest: (0, 0))
    grid_spec = pltpu.PrefetchScalarGridSpec(
        num_scalar_prefetch=1,
        grid=(n // tile,),
        in_specs=[row, pl.BlockSpec((tile, TOP_K), lambda i, dest: (i, 0)),
                  pl.BlockSpec(memory_space=pl.ANY), par, par],
        out_specs=[row, row],
        scratch_shapes=[pltpu.VMEM((2, TOP_K, tile, d), F32), pltpu.SemaphoreType.DMA((2,))])
    return pl.pallas_call(
        functools.partial(_combine_ln2_kernel, tile=tile, n=n, alpha=alpha),
        grid_spec=grid_spec,
        out_shape=[jax.ShapeDtypeStruct((n, d), F32), jax.ShapeDtypeStruct((n, d), BF16)],
        compiler_params=_params("arbitrary"),
        name="moe_combine_ln2",
    )(dest_flat, x, gate, yb, g, b)


def _route(lgt, bias_col, tile):
    n = lgt.shape[1]
    n_blocks = -(-(n * TOP_K) // MOE_BLOCK) + N_EXPERTS
    e, g, rank, cnt = _route_select(lgt, bias_col, tile)
    dest, be, nu = _route_place(e, rank, cnt, tile, n_blocks)
    return g[:TOP_K].T, dest[:TOP_K].reshape(-1), be[0, :n_blocks], nu[0, :1], n_blocks


def _proj_tn(width):
    blocks = width // LANES
    best = max(d for d in range(1, blocks + 1) if blocks % d == 0 and d * LANES <= 1024)
    return best * LANES


def _tiles(B, S):
    n = B * S
    return dict(
        tm_proj=min(1024, n),
        ts_prep=min(512, S),
        rwkv_chunks=min(8, S // RWKV_CHUNK),
        tq=min(256, S),
        tm_merge=min(512, n),
        t_route=min(512, n),
        t_dispatch=min(512, n),
        t_combine=min(256, n),
    )


def _pad_rows(a, before, total):
    return jnp.zeros((total, a.shape[1]), a.dtype).at[before:before + a.shape[0]].set(a)


def _block_diag_ones(dim):
    i = jnp.arange(dim) // HEAD
    return (i[:, None] == i[None, :]).astype(BF16)


def kernel(x, w_in, rwkv_mu, rwkv_w0, rwkv_w_up, rwkv_a0, rwkv_a_up, rwkv_g_up, rwkv_k_k, rwkv_k_a,
           rwkv_r_k, rwkv_gn_g, rwkv_gn_b, fox_f_bias, ssm_conv_w, ssm_conv_b, ssm_dt_bias, ssm_a_log,
           ssm_d, ssm_norm_g, gate_bias, w_br_rwkv, w_br_fox, w_br_ssm, w_out, ln1_g, ln1_b, router_w,
           router_bias, exp_w_gate, exp_w_up, exp_w_down, ln2_g, ln2_b):
    B, S, D = x.shape
    depth = w_in.shape[0]
    n = B * S
    t = _tiles(B, S)
    alpha = (2 * depth) ** 0.25

    row1 = lambda a: a.reshape(1, -1).astype(F32)
    bd512 = _block_diag_ones(RWKV_DIM)
    expand = (jnp.arange(LANES)[:, None] == DT_OFF + jnp.arange(SSM_DIM)[None, :] // HEAD).astype(BF16)
    rw = router_w.astype(F32).T
    rwh = rw.astype(BF16)
    rwl = (rw - rwh.astype(F32)).astype(BF16)
    rb_col = router_bias.astype(F32).reshape(N_EXPERTS, 1)

    xf = x.reshape(n, D).astype(F32)
    xb16 = xf.astype(BF16)
    for l in range(depth):
        w = w_in[l]
        w_small = jnp.concatenate([w[:, OFF_SSM - FOX_HEADS:OFF_SSM], w[:, OFF_GATE - SSM_HEADS:OFF_GATE]], axis=1)
        w_small_l = jnp.zeros((D, LANES), F32).at[:, :w_small.shape[1]].set(w_small).astype(BF16)
        w_small_t = jnp.zeros((SMALL_ROWS, D), F32).at[:w_small.shape[1]].set(w_small.T).astype(BF16)
        mm = lambda lo, hi: _matmul(xb16, w[:, lo:hi].astype(BF16), t["tm_proj"], _proj_tn(hi - lo), BF16)
        p_rwkv = mm(0, RWKV_COLS)
        p_fox = mm(OFF_FOX, OFF_FOX + 3 * FOX_DIM)
        p_z = mm(OFF_SSM, OFF_SSM + SSM_DIM)
        p_xbc = mm(OFF_SSM + SSM_DIM, OFF_SSM + SSM_DIM + SSM_CONV_DIM)
        p_gate = mm(OFF_GATE, OFF_GATE + 3 * D)
        ps = _matmul(xb16, w_small_l, t["tm_proj"], LANES)
        pst = _matmul_nt(w_small_t, xb16, t["tm_proj"])

        wup = _pad_rows(rwkv_w_up[l], 0, DECAY_LORA + ICLR_LORA).astype(BF16)
        aup = _pad_rows(rwkv_a_up[l], DECAY_LORA, DECAY_LORA + ICLR_LORA).astype(BF16)
        r, lw, k, v, kn, b, g = _rwkv_prep(
            p_rwkv, B, S, t["ts_prep"], row1(rwkv_mu[l]), row1(rwkv_w0[l]), wup, row1(rwkv_a0[l]), aup,
            rwkv_g_up[l].astype(BF16), row1(rwkv_k_k[l]), row1(rwkv_k_a[l]), bd512)
        o_r = _rwkv_chunk(r, lw, k, v, kn, b, g, row1(rwkv_gn_g[l]), row1(rwkv_gn_b[l]),
                          row1(rwkv_r_k[l]), B, S, t["rwkv_chunks"])

        fb_lane = jnp.zeros((1, LANES), F32).at[0, :FOX_HEADS].set(fox_f_bias[l].astype(F32))
        fb_col = jnp.zeros((SMALL_ROWS, 1), F32).at[:FOX_HEADS, 0].set(fox_f_bias[l].astype(F32))
        cc, cr = _fox_c(ps, pst, fb_lane, fb_col, B, S)
        o_f = _fox_attn(p_fox, cc, cr, B, S, t["tq"])

        dtb = ssm_dt_bias[l].astype(F32)
        a_neg = -jnp.exp(ssm_a_log[l].astype(F32))
        dtbl = jnp.zeros((1, LANES), F32).at[0, DT_OFF:DT_OFF + SSM_HEADS].set(dtb)
        dtbc = jnp.zeros((SMALL_ROWS, 1), F32).at[DT_OFF:DT_OFF + SSM_HEADS, 0].set(dtb)
        ac = jnp.zeros((SMALL_ROWS, 1), F32).at[DT_OFF:DT_OFF + SSM_HEADS, 0].set(a_neg)
        al = jnp.repeat(a_neg, HEAD).reshape(1, SSM_DIM)
        dl = jnp.repeat(ssm_d[l].astype(F32), HEAD).reshape(1, SSM_DIM)
        o_m = _mamba(p_z, p_xbc, ps, pst, ssm_conv_w[l].astype(F32), row1(ssm_conv_b[l]), dtbl, dtbc,
                     al, ac, dl, row1(ssm_norm_g[l]), expand, B, S)

        x1, logits = _merge(
            o_r, o_f, o_m, p_gate, xf, w_br_rwkv[l].astype(BF16), w_br_fox[l].astype(BF16),
            w_br_ssm[l].astype(BF16), w_out[l].astype(BF16), row1(gate_bias[l]), row1(ln1_g[l]),
            row1(ln1_b[l]), rwh, rwl, t["tm_merge"], alpha)

        gate, dest, block_expert, n_used, n_blocks = _route(logits, rb_col, t["t_route"])
        xb = _dispatch(dest, x1, n_blocks * MOE_BLOCK, t["t_dispatch"])
        yb = _moe(block_expert, n_used, xb, exp_w_gate[l], exp_w_up[l], exp_w_down[l])
        xf, xb16 = _combine_ln2(dest, x1, gate, yb, row1(ln2_g[l]), row1(ln2_b[l]), t["t_combine"], alpha)
    return xf.reshape(B, S, D).astype(x.dtype)
```

```python
import functools
import math

import jax
import jax.numpy as jnp
from jax import lax
from jax.experimental import pallas as pl
from jax.experimental.pallas import tpu as pltpu

F32 = jnp.float32
BF16 = jnp.bfloat16

LANES = 128
SUBLANES = 8
VMEM_LIMIT = 48 * 1024 * 1024

D_MODEL = 1024
HEAD = 64
PAIR = LANES // HEAD
RWKV_DIM = 512
DECAY_LORA = 64
ICLR_LORA = 64
GATE_LORA = 128
RWKV_COLS = 3 * RWKV_DIM + DECAY_LORA + ICLR_LORA + GATE_LORA
RWKV_GN_EPS = 64e-5
FOX_DIM = 512
FOX_HEADS = 8
SSM_DIM = 1024
SSM_HEADS = 16
SSM_GROUPS = 2
SSM_STATE = 128
CONV_WIDTH = 4
SSM_CONV_DIM = SSM_DIM + 2 * SSM_GROUPS * SSM_STATE
NORM_EPS = 1e-5
N_EXPERTS = 32
N_EXPERT_GROUPS = 4
EXPERTS_PER_GROUP = N_EXPERTS // N_EXPERT_GROUPS
TOP_K = 2
D_EXPERT = 512
LN_EPS = 1e-5
LOG2E = math.log2(math.e)
OFF_FOX = RWKV_COLS
OFF_SSM = OFF_FOX + 3 * FOX_DIM + FOX_HEADS
OFF_GATE = OFF_SSM + SSM_DIM + SSM_CONV_DIM + SSM_HEADS

SMALL_ROWS = 32
DT_OFF = 8
FOX_WIDTH = 512
RWKV_CHUNK = 64
RWKV_GROUP = 8
SSD_CHUNK = 128
MOE_BLOCK = 256
DMA_UNROLL = 8


def _bf(x):
    return x.astype(BF16)


def _dot(a, b):
    return jnp.dot(a, b, preferred_element_type=F32)


def _dot_nt(a, b):
    return lax.dot_general(a, b, (((1,), (1,)), ((), ())), preferred_element_type=F32)


def _dot_tn(a, b):
    return lax.dot_general(a, b, (((0,), (0,)), ((), ())), preferred_element_type=F32)


def _split(x, parts):
    out = []
    rem = x
    for _ in range(parts):
        hi = _bf(rem)
        out.append(hi)
        rem = rem - hi.astype(F32)
    return out


def _dot_sel_rhs(m01, x, parts=3):
    acc = None
    for part in _split(x, parts):
        t = _dot(m01, part)
        acc = t if acc is None else acc + t
    return acc


def _dot_sel_lhs(x, m01, parts=3):
    acc = None
    for part in _split(x, parts):
        t = _dot(part, m01)
        acc = t if acc is None else acc + t
    return acc


def _sigmoid(x):
    return 1.0 / (1.0 + jnp.exp(-x))


def _softplus(x):
    return jnp.maximum(x, 0.0) + jnp.log(1.0 + jnp.exp(-jnp.abs(x)))


def _log_sigmoid(x):
    return jnp.minimum(x, 0.0) - jnp.log(1.0 + jnp.exp(-jnp.abs(x)))


def _iota2(shape, dim):
    return lax.broadcasted_iota(jnp.int32, shape, dim)


def _params(*sem):
    return pltpu.CompilerParams(dimension_semantics=sem, vmem_limit_bytes=VMEM_LIMIT)


def _mm_kernel(x_ref, w_ref, o_ref):
    o_ref[...] = _dot(x_ref[...], w_ref[...]).astype(o_ref.dtype)


def _matmul(x, w, tm, tn, out_dtype=F32):
    n, k = x.shape
    nc = w.shape[1]
    return pl.pallas_call(
        _mm_kernel,
        grid=(n // tm, nc // tn),
        in_specs=[pl.BlockSpec((tm, k), lambda i, j: (i, 0)),
                  pl.BlockSpec((k, tn), lambda i, j: (0, j))],
        out_specs=pl.BlockSpec((tm, tn), lambda i, j: (i, j)),
        out_shape=jax.ShapeDtypeStruct((n, nc), out_dtype),
        compiler_params=_params("parallel", "parallel"),
        name="in_proj",
    )(x, w)


def _small_proj_kernel(x_ref, w_ref, wt_ref, o_ref, ot_ref):
    x = x_ref[...]
    o_ref[...] = _dot(x, w_ref[...])
    ot_ref[...] = _dot_nt(wt_ref[...], x)


def _small_proj(x, w, wt, tm):
    n, k = x.shape
    r = wt.shape[0]
    return pl.pallas_call(
        _small_proj_kernel,
        grid=(n // tm,),
        in_specs=[pl.BlockSpec((tm, k), lambda i: (i, 0)),
                  pl.BlockSpec((k, LANES), lambda i: (0, 0)),
                  pl.BlockSpec((r, k), lambda i: (0, 0))],
        out_specs=[pl.BlockSpec((tm, LANES), lambda i: (i, 0)),
                   pl.BlockSpec((r, tm), lambda i: (0, i))],
        out_shape=[jax.ShapeDtypeStruct((n, LANES), F32), jax.ShapeDtypeStruct((r, n), F32)],
        compiler_params=_params("parallel"),
        name="in_proj_small",
    )(x, w, wt)


def _rwkv_prep_kernel(p_ref, mu_ref, w0_ref, wup_ref, a0_ref, aup_ref, gup_ref, kk_ref, ka_ref,
                      bd_ref, r_o, lw_o, k_o, v_o, kn_o, b_o, g_o, carry_ref):
    @pl.when(pl.program_id(1) == 0)
    def _():
        carry_ref[...] = jnp.zeros_like(carry_ref)

    p = p_ref[...].astype(F32)
    ts = p.shape[0]
    row = _iota2((ts, 1), 0)
    prev = jnp.where(row == 0, carry_ref[...], pltpu.roll(p, 1, axis=0))
    carry_ref[...] = p[ts - 1:ts, :]
    p = p + mu_ref[...] * (prev - p)

    R = RWKV_DIM
    r = p[:, 0:R]
    k = p[:, R:2 * R]
    v = p[:, 2 * R:3 * R]
    lora = p[:, 3 * R:3 * R + DECAY_LORA + ICLR_LORA]
    gd = p[:, 3 * R + DECAY_LORA + ICLR_LORA:]
    w_pre = w0_ref[...] + _dot(_bf(jnp.tanh(lora)), wup_ref[...])
    lw = -jnp.exp(_log_sigmoid(w_pre) - 0.5)
    a = _sigmoid(a0_ref[...] + _dot(_bf(lora), aup_ref[...]))
    g = _dot(_bf(_sigmoid(gd)), gup_ref[...])
    kk = k * kk_ref[...]
    ss = _dot_sel_lhs(kk * kk, bd_ref[...], parts=2)
    kn = kk / jnp.maximum(jnp.sqrt(ss), 1e-12)
    r_o[...] = _bf(r)
    lw_o[...] = lw
    k_o[...] = _bf(k * (1.0 + (a - 1.0) * ka_ref[...]))
    v_o[...] = _bf(v)
    kn_o[...] = _bf(kn)
    b_o[...] = _bf(kn * a)
    g_o[...] = _bf(g)


def _rwkv_prep(p, B, S, ts, mu, w0, wup, a0, aup, gup, k_k, k_a, bd):
    n = B * S
    ns = S // ts
    row_spec = lambda w: pl.BlockSpec((ts, w), lambda b, s: (b * ns + s, 0))
    full = lambda a: pl.BlockSpec(a.shape, lambda b, s: (0,) * a.ndim)
    out = lambda dt: jax.ShapeDtypeStruct((n, RWKV_DIM), dt)
    return pl.pallas_call(
        _rwkv_prep_kernel,
        grid=(B, ns),
        in_specs=[row_spec(RWKV_COLS)] + [full(a) for a in (mu, w0, wup, a0, aup, gup, k_k, k_a, bd)],
        out_specs=[row_spec(RWKV_DIM)] * 7,
        out_shape=[out(BF16), out(F32)] + [out(BF16)] * 5,
        scratch_shapes=[pltpu.VMEM((1, RWKV_COLS), F32)],
        compiler_params=_params("parallel", "arbitrary"),
        name="rwkv_prep",
    )(p, mu, w0, wup, a0, aup, gup, k_k, k_a, bd)


def _rwkv_chunk_kernel(r_ref, lw_ref, k_ref, v_ref, kn_ref, b_ref, g_ref, gng_ref, gnb_ref, rk_ref,
                       o_ref, s_ref, *, n_chunks):
    C = RWKV_CHUNK

    @pl.when(pl.program_id(2) == 0)
    def _():
        s_ref[...] = jnp.zeros_like(s_ref)

    lane = _iota2((1, LANES), 1)
    m0 = (lane < HEAD).astype(F32)
    m1 = 1.0 - m0
    ri = _iota2((LANES, LANES), 0)
    ci = _iota2((LANES, LANES), 1)
    same = (ri >= HEAD) == (ci >= HEAD)
    smask = jnp.where(same & (ri > ci), 1.0, 0.0)
    imask = jnp.where(same & (ri >= ci), 1.0, 0.0)
    eye = jnp.where(ri == ci, 1.0, 0.0)
    bd_mean = _bf(jnp.where(same, 1.0 / HEAD, 0.0))
    bd_one = _bf(jnp.where(same, 1.0, 0.0))
    tri = _bf(jnp.where(_iota2((C, C), 0) >= _iota2((C, C), 1), 1.0, 0.0))

    def stack(x):
        return jnp.concatenate([x * m0, x * m1], axis=0)

    gng = gng_ref[...]
    gnb = gnb_ref[...]
    rk = rk_ref[...]

    def group(chunks):
        ld = lambda ref, c: ref[c * C:(c + 1) * C, :]
        r = [ld(r_ref, c).astype(F32) for c in chunks]
        lw = [ld(lw_ref, c) for c in chunks]
        k = [ld(k_ref, c).astype(F32) for c in chunks]
        v = [ld(v_ref, c).astype(F32) for c in chunks]
        kn = [ld(kn_ref, c).astype(F32) for c in chunks]
        b = [ld(b_ref, c).astype(F32) for c in chunks]
        idx = range(len(chunks))

        cs = [_dot_sel_rhs(tri, lw[i]) for i in idx]
        Rs, As, Ks, Bs, Rt, At, Be, Ke, Vs, g_end = ([] for _ in range(10))
        for i in idx:
            sh = cs[i][C // 2 - 1:C // 2, :]
            cs_end = cs[i][C - 1:C, :]
            e_pos = jnp.exp(cs[i] - sh)
            e_neg = jnp.exp(sh - cs[i])
            e_prev = jnp.exp(cs[i] - lw[i] - sh)
            e_end = jnp.exp(cs_end - cs[i])
            Rs.append(_bf(stack(r[i] * e_pos)))
            As.append(_bf(stack(-kn[i] * e_prev)))
            Ks.append(_bf(stack(k[i] * e_neg)))
            Bs.append(_bf(stack(b[i] * e_neg)))
            Rt.append(stack(r[i] * jnp.exp(cs[i])))
            At.append(_bf(stack(-kn[i] * jnp.exp(cs[i] - lw[i]))))
            Be.append(_bf(stack(b[i] * e_end)))
            Ke.append(_bf(stack(k[i] * e_end)))
            Vs.append(_bf(stack(v[i])))
            g_end.append(jnp.exp(cs_end))

        lanes2 = lambda x, y: jnp.concatenate([x, y], axis=1)
        L = LANES
        BK = [jnp.concatenate([Bs[i], Ks[i]], axis=0) for i in idx]
        GA = [_dot_nt(As[i], BK[i]) for i in idx]
        GR = [_dot_nt(Rs[i], BK[i]) for i in idx]
        Aab = [GA[i][:, :L] * smask for i in idx]
        Aak = [_bf(GA[i][:, L:] * smask) for i in idx]
        RBb = [_bf(GR[i][:, :L] * imask) for i in idx]
        RKb = [_bf(GR[i][:, L:] * imask) for i in idx]
        W1b = [_bf(_dot(Aak[i], Vs[i])) for i in idx]

        T = [eye + Aab[i] for i in idx]
        Pb = [_bf(Aab[i]) for i in idx]
        Pb = [_bf(_dot(Pb[i], Pb[i])) for i in idx]
        for _ in range(int(math.log2(C)) - 2):
            PT = [_dot(Pb[i], lanes2(Pb[i], _bf(T[i]))) for i in idx]
            Pb = [_bf(PT[i][:, :L]) for i in idx]
            T = [T[i] + PT[i][:, L:] for i in idx]
        T = [T[i] + _dot(Pb[i], _bf(T[i])) for i in idx]
        Tb = [_bf(T[i]) for i in idx]

        TU = [_dot(Tb[i], lanes2(At[i], W1b[i])) for i in idx]
        TUb = [_bf(TU[i]) for i in idx]
        TAb = [TUb[i][:, :L] for i in idx]
        U0b = [TUb[i][:, L:] for i in idx]
        RTU = [_dot(RBb[i], TUb[i]) for i in idx]
        Q = [_bf(Rt[i] + RTU[i][:, :L]) for i in idx]
        Y0 = [_dot(RKb[i], Vs[i]) + RTU[i][:, L:] for i in idx]
        Mt = [_bf(_dot_tn(TAb[i], Be[i])) for i in idx]
        Nt = [_dot_tn(Vs[i], Ke[i]) + _dot_tn(U0b[i], Be[i]) for i in idx]

        Y = []
        S0 = s_ref[...]
        for i in idx:
            S0b = _bf(S0)
            Y.append(_dot_nt(Q[i], S0b) + Y0[i])
            S0 = S0 * g_end[i] + _dot(S0b, Mt[i]) + Nt[i]
        s_ref[...] = S0

        rows = slice(chunks[0] * C, (chunks[-1] + 1) * C)
        cat = lambda xs: jnp.concatenate(xs, axis=0)
        y = cat([Y[i][:C, :] + Y[i][C:, :] for i in idx])
        r_all, k_all, v_all = cat(r), cat(k), cat(v)
        mean = _dot_sel_lhs(y, bd_mean, parts=2)
        yc = y - mean
        var = _dot_sel_lhs(yc * yc, bd_mean, parts=2)
        yn = yc * lax.rsqrt(var + RWKV_GN_EPS) * gng + gnb
        bonus = _dot_sel_lhs(r_all * k_all * rk, bd_one, parts=2) * v_all
        o_ref[rows, :] = _bf((yn + bonus) * g_ref[rows, :].astype(F32))

    for first in range(0, n_chunks, RWKV_GROUP):
        group(list(range(first, min(first + RWKV_GROUP, n_chunks))))


def _rwkv_chunk(r, lw, k, v, kn, b, g, gn_g, gn_b, r_k, B, S, n_chunks):
    n = B * S
    rows = n_chunks * RWKV_CHUNK
    ns = S // rows
    npair = RWKV_DIM // LANES
    blk = pl.BlockSpec((rows, LANES), lambda bb, p, s: (bb * ns + s, p))
    par = pl.BlockSpec((1, LANES), lambda bb, p, s: (0, p))
    return pl.pallas_call(
        functools.partial(_rwkv_chunk_kernel, n_chunks=n_chunks),
        grid=(B, npair, ns),
        in_specs=[blk] * 7 + [par] * 3,
        out_specs=blk,
        out_shape=jax.ShapeDtypeStruct((n, RWKV_DIM), BF16),
        scratch_shapes=[pltpu.VMEM((LANES, LANES), F32)],
        compiler_params=_params("parallel", "parallel", "arbitrary"),
        name="rwkv_chunk",
    )(r, lw, k, v, kn, b, g, gn_g, gn_b, r_k)


def _fox_c_kernel(ps_ref, pst_ref, bl_ref, bc_ref, cc_ref, cr_ref, *, seq):
    L = LANES
    ri = _iota2((L, L), 0)
    ci = _iota2((L, L), 1)
    tril = _bf(jnp.where(ri >= ci, 1.0, 0.0))
    triu = _bf(jnp.where(ri <= ci, 1.0, 0.0))
    carry = jnp.zeros((1, L), F32)
    carry_c = jnp.zeros((SMALL_ROWS, 1), F32)
    for i in range(seq // L):
        rows = slice(i * L, (i + 1) * L)
        cs = _dot_sel_rhs(tril, _log_sigmoid(ps_ref[rows, :] + bl_ref[...])) + carry
        cc_ref[rows, :] = cs
        carry = cs[L - 1:L, :]
        cst = _dot_sel_lhs(_log_sigmoid(pst_ref[:, rows] + bc_ref[...]), triu) + carry_c
        cr_ref[:, rows] = cst
        carry_c = cst[:, L - 1:L]


def _fox_c(ps, pst, bias_lane, bias_col, B, S):
    n = B * S
    return pl.pallas_call(
        functools.partial(_fox_c_kernel, seq=S),
        grid=(B,),
        in_specs=[pl.BlockSpec((S, LANES), lambda b: (b, 0)),
                  pl.BlockSpec((SMALL_ROWS, S), lambda b: (0, b)),
                  pl.BlockSpec((1, LANES), lambda b: (0, 0)),
                  pl.BlockSpec((SMALL_ROWS, 1), lambda b: (0, 0))],
        out_specs=[pl.BlockSpec((S, LANES), lambda b: (b, 0)),
                   pl.BlockSpec((SMALL_ROWS, S), lambda b: (0, b))],
        out_shape=[jax.ShapeDtypeStruct((n, LANES), F32),
                   jax.ShapeDtypeStruct((SMALL_ROWS, n), F32)],
        compiler_params=_params("parallel"),
        name="fox_cumgate",
    )(ps, pst, bias_lane, bias_col)


def _transpose_blocks(x):
    n = x.shape[0] // LANES
    return jnp.concatenate([x[i * LANES:(i + 1) * LANES, :].T for i in range(n)], axis=1)


def _fox_attn_kernel(q_ref, k_ref, v_ref, cc_ref, cr_ref, o_ref, vt_ref, ck_ref, s_scr, p_scr, *, tq, seq):
    nblk = q_ref.shape[1] // LANES
    heads = nblk * PAIR
    head0 = pl.program_id(1) * heads
    qi = pl.program_id(2)

    @pl.when(qi == 0)
    def _():
        for i in range(seq // LANES):
            rows = slice(i * LANES, (i + 1) * LANES)
            for c in range(nblk):
                cols = slice(c * LANES, (c + 1) * LANES)
                vt_ref[cols, rows] = _bf(v_ref[rows, cols].astype(F32).T)
        ri = _iota2((LANES, LANES), 0)
        for h in range(heads):
            sel = _bf(jnp.where(ri == head0 + h, 1.0, 0.0))
            ck_ref[h] = _dot_sel_lhs(cc_ref[...], sel) * LOG2E

    in0 = _iota2((LANES, 1), 0) < HEAD
    qts = []
    for c in range(nblk):
        qt = _transpose_blocks(q_ref[:, c * LANES:(c + 1) * LANES].astype(F32) * (HEAD ** -0.5 * LOG2E))
        qts += [_bf(jnp.where(in0, qt, 0.0)), _bf(jnp.where(in0, 0.0, qt))]
    q0 = pl.multiple_of(qi * tq, tq)
    cqs = tuple(cr_ref[pl.ds(head0 + h, 1), pl.ds(q0, tq)] * LOG2E for h in range(heads))
    key_minus_query = _iota2((tq, tq), 0) - _iota2((tq, tq), 1)
    last_block = seq // tq - 1

    s_scr[...] = jnp.full(s_scr.shape, -jnp.inf, F32)
    p_scr[...] = jnp.zeros(p_scr.shape, BF16)

    def step(j, carry, diagonal=False, scores=True, softmax=True):
        pv0 = pl.multiple_of(jnp.clip(j - 2, 0, last_block) * tq, tq)
        if scores:
            qk0 = pl.multiple_of(j * tq, tq)
            kbs = [k_ref[pl.ds(qk0, tq), c * LANES:(c + 1) * LANES] for c in range(nblk)]
        new = []
        for h in range(heads):
            m, l, alpha, acc = carry[h]
            vt = vt_ref[h * HEAD:(h + 1) * HEAD, pl.ds(pv0, tq)]
            acc = alpha * acc + _dot(vt, p_scr[h])
            if softmax:
                s = s_scr[h]
                m_new = jnp.maximum(m, jnp.max(s, axis=0, keepdims=True))
                p = jnp.exp2(s - m_new)
                alpha = jnp.exp2(m - m_new)
                l = alpha * l + jnp.sum(p, axis=0, keepdims=True)
                m = m_new
                p_scr[h] = _bf(p)
            if scores:
                ck = ck_ref[h, pl.ds(qk0, tq), :]
                s_next = _dot(kbs[h // PAIR], qts[h]) + cqs[h] - jnp.concatenate([ck] * (tq // LANES), axis=1)
                s_scr[h] = jnp.where(key_minus_query <= 0, s_next, -jnp.inf) if diagonal else s_next
            new.append((m, l, alpha, acc))
        return tuple(new)

    init = tuple((jnp.full((1, tq), -1e30, F32), jnp.zeros((1, tq), F32), jnp.ones((1, tq), F32),
                  jnp.zeros((HEAD, tq), F32)) for _ in range(heads))
    carry = lax.fori_loop(0, qi, step, init)
    carry = step(qi, carry, diagonal=True)
    carry = step(qi + 1, carry, scores=False)
    carry = step(qi + 2, carry, scores=False, softmax=False)
    ot = jnp.concatenate([acc / l for (_, l, _, acc) in carry], axis=0)
    o_ref[...] = jnp.concatenate(
        [jnp.concatenate([ot[c * LANES:(c + 1) * LANES, i * LANES:(i + 1) * LANES].T for c in range(nblk)], axis=1)
         for i in range(tq // LANES)], axis=0).astype(o_ref.dtype)


def _fox_attn(qkv, cc, cr, B, S, tq):
    n = B * S
    nq = S // tq
    w = FOX_WIDTH
    nw = FOX_DIM // w
    heads = w // HEAD
    return pl.pallas_call(
        functools.partial(_fox_attn_kernel, tq=tq, seq=S),
        grid=(B, nw, nq),
        in_specs=[pl.BlockSpec((tq, w), lambda b, p, i: (b * nq + i, p)),
                  pl.BlockSpec((S, w), lambda b, p, i: (b, nw + p)),
                  pl.BlockSpec((S, w), lambda b, p, i: (b, 2 * nw + p)),
                  pl.BlockSpec((S, LANES), lambda b, p, i: (b, 0)),
                  pl.BlockSpec((SMALL_ROWS, S), lambda b, p, i: (0, b))],
        out_specs=pl.BlockSpec((tq, w), lambda b, p, i: (b * nq + i, p)),
        out_shape=jax.ShapeDtypeStruct((n, FOX_DIM), BF16),
        scratch_shapes=[pltpu.VMEM((w, S), BF16), pltpu.VMEM((heads, S, LANES), F32),
                        pltpu.VMEM((heads, tq, tq), F32), pltpu.VMEM((heads, tq, tq), BF16)],
        compiler_params=_params("parallel", "parallel", "arbitrary"),
        name="fox_attn",
    )(qkv, qkv, qkv, cc, cr)


def _mamba_kernel(z_ref, xbc_ref, ps_ref, pst_ref, cw_ref, cb_ref, dtbl_ref, dtbc_ref, al_ref, ac_ref,
                  dl_ref, ng_ref, ex_ref, o_ref, halo_ref, st_ref):
    L = SSD_CHUNK

    @pl.when(pl.program_id(1) == 0)
    def _():
        halo_ref[...] = jnp.zeros_like(halo_ref)
        st_ref[...] = jnp.zeros_like(st_ref)

    xbc = xbc_ref[...].astype(F32)
    halo = halo_ref[...]
    row8 = _iota2((SUBLANES, 1), 0)
    acc = cb_ref[...] + cw_ref[CONV_WIDTH - 1:CONV_WIDTH, :] * xbc
    for sft in range(1, CONV_WIDTH):
        rolled = pltpu.roll(xbc, sft, axis=0)
        top = jnp.where(row8 < sft, pltpu.roll(halo, sft, axis=0), rolled[:SUBLANES, :])
        shifted = jnp.concatenate([top, rolled[SUBLANES:, :]], axis=0)
        acc = acc + cw_ref[CONV_WIDTH - 1 - sft:CONV_WIDTH - sft, :] * shifted
    halo_ref[...] = xbc[L - SUBLANES:, :]
    act = acc * _sigmoid(acc)
    xs = act[:, :SSM_DIM]
    Bm = act[:, SSM_DIM:SSM_DIM + SSM_GROUPS * SSM_STATE]
    Cm = act[:, SSM_DIM + SSM_GROUPS * SSM_STATE:]

    ri = _iota2((L, L), 0)
    ci = _iota2((L, L), 1)
    lower = ri >= ci
    tril = _bf(jnp.where(lower, 1.0, 0.0))
    triu = _bf(jnp.where(ri <= ci, 1.0, 0.0))
    lane = _iota2((1, LANES), 1)
    in0 = lane < HEAD

    dt_full = _dot_sel_lhs(_softplus(ps_ref[...] + dtbl_ref[...]), ex_ref[...])
    cs = _dot_sel_rhs(tril, dt_full * al_ref[...])
    cs_end = cs[L - 1:L, :]
    ecs = jnp.exp(cs)
    X = xs * dt_full
    Xe = X * jnp.exp(cs_end - cs)
    csr = _dot_sel_lhs(_softplus(pst_ref[...] + dtbc_ref[...]) * ac_ref[...], triu)

    heads_per_group = SSM_HEADS // SSM_GROUPS
    pairs_per_group = heads_per_group // PAIR
    ys = []
    for g in range(SSM_GROUPS):
        Bg = Bm[:, g * SSM_STATE:(g + 1) * SSM_STATE]
        Cg = _bf(Cm[:, g * SSM_STATE:(g + 1) * SSM_STATE])
        CB = _dot_nt(Cg, _bf(Bg))
        BgT = _bf(Bg.T)
        for pp in range(pairs_per_group):
            p = g * pairs_per_group + pp
            cols = slice(p * LANES, (p + 1) * LANES)
            Xp = X[:, cols]
            ydiag = None
            for hh in range(PAIR):
                h = PAIR * p + hh
                decay = jnp.exp(cs[:, h * HEAD:h * HEAD + 1] - csr[DT_OFF + h:DT_OFF + h + 1, :])
                Mh = _bf(CB * jnp.where(lower, decay, 0.0))
                Xh = _bf(jnp.where(in0, Xp, 0.0) if hh == 0 else jnp.where(in0, 0.0, Xp))
                t = _dot(Mh, Xh)
                ydiag = t if ydiag is None else ydiag + t
            st = st_ref[p]
            yoff = _dot(Cg, _bf(st)) * ecs[:, cols]
            st_ref[p] = st * ecs[L - 1:L, cols] + _dot(BgT, _bf(Xe[:, cols]))
            ys.append(ydiag + yoff + dl_ref[:, cols] * xs[:, cols])
    z = z_ref[...].astype(F32)
    y = jnp.concatenate(ys, axis=1) * (z * _sigmoid(z))
    gw = SSM_DIM // SSM_GROUPS
    outs = []
    for g in range(SSM_GROUPS):
        yg = y[:, g * gw:(g + 1) * gw]
        ms = jnp.mean(yg * yg, axis=-1, keepdims=True)
        outs.append(yg * lax.rsqrt(ms + NORM_EPS))
    o_ref[...] = _bf(jnp.concatenate(outs, axis=1) * ng_ref[...])


def _mamba(z, xbc, ps, pst, cw, cb, dtbl, dtbc, al, ac, dl, ng, ex, B, S):
    n = B * S
    L = SSD_CHUNK
    nc = S // L
    row = lambda w: pl.BlockSpec((L, w), lambda b, c: (b * nc + c, 0))
    full = lambda a: pl.BlockSpec(a.shape, lambda b, c: (0,) * a.ndim)
    return pl.pallas_call(
        _mamba_kernel,
        grid=(B, nc),
        in_specs=[row(SSM_DIM), row(SSM_CONV_DIM), row(LANES),
                  pl.BlockSpec((SMALL_ROWS, L), lambda b, c: (0, b * nc + c))]
                 + [full(a) for a in (cw, cb, dtbl, dtbc, al, ac, dl, ng, ex)],
        out_specs=row(SSM_DIM),
        out_shape=jax.ShapeDtypeStruct((n, SSM_DIM), BF16),
        scratch_shapes=[pltpu.VMEM((SUBLANES, SSM_CONV_DIM), F32),
                        pltpu.VMEM((SSM_HEADS // PAIR, SSM_STATE, LANES), F32)],
        compiler_params=_params("parallel", "arbitrary"),
        name="mamba_ssd",
    )(z, xbc, ps, pst, cw, cb, dtbl, dtbc, al, ac, dl, ng, ex)


def _layer_norm(h, g, b):
    mu = jnp.mean(h, axis=-1, keepdims=True)
    hc = h - mu
    var = jnp.mean(hc * hc, axis=-1, keepdims=True)
    return hc * lax.rsqrt(var + LN_EPS) * g + b


def _merge_kernel(or_ref, of_ref, om_ref, pg_ref, x_ref, wr_ref, wf_ref, wm_ref, wo_ref, gb_ref,
                  lg_ref, lb_ref, rwh_ref, rwl_ref, x1_ref, lgt_ref, *, alpha):
    D = D_MODEL
    gate = lambda i: _sigmoid(pg_ref[:, i * D:(i + 1) * D].astype(F32) + gb_ref[:, i * D:(i + 1) * D])
    merged = (gate(0) * _dot(or_ref[...], wr_ref[...])
              + gate(1) * _dot(of_ref[...], wf_ref[...])
              + gate(2) * _dot(om_ref[...], wm_ref[...]))
    mix = _dot(_bf(merged), wo_ref[...])
    x1 = _layer_norm(alpha * x_ref[...] + mix, lg_ref[...], lb_ref[...])
    x1_ref[...] = x1
    xh = _bf(x1)
    xl = _bf(x1 - xh.astype(F32))
    lgt_ref[...] = (_dot_nt(rwh_ref[...], xh) + _dot_nt(rwh_ref[...], xl) + _dot_nt(rwl_ref[...], xh))


def _merge(o_r, o_f, o_m, pg, x, wr, wf, wm, wo, gb, lg, lb, rwh, rwl, tm, alpha):
    n = x.shape[0]
    row = lambda w: pl.BlockSpec((tm, w), lambda i: (i, 0))
    full = lambda a: pl.BlockSpec(a.shape, lambda i: (0,) * a.ndim)
    return pl.pallas_call(
        functools.partial(_merge_kernel, alpha=alpha),
        grid=(n // tm,),
        in_specs=[row(RWKV_DIM), row(FOX_DIM), row(SSM_DIM), row(3 * D_MODEL), row(D_MODEL)]
                 + [full(a) for a in (wr, wf, wm, wo, gb, lg, lb, rwh, rwl)],
        out_specs=[row(D_MODEL), pl.BlockSpec((N_EXPERTS, tm), lambda i: (0, i))],
        out_shape=[jax.ShapeDtypeStruct((n, D_MODEL), F32),
                   jax.ShapeDtypeStruct((N_EXPERTS, n), F32)],
        compiler_params=_params("parallel"),
        name="merge_ln1_router",
    )(o_r, o_f, o_m, pg, x, wr, wf, wm, wo, gb, lg, lb, rwh, rwl)


def _rows(*vals):
    t = vals[0].shape[1]
    sub = _iota2((SUBLANES, t), 0)
    out = jnp.zeros((SUBLANES, t), vals[0].dtype)
    for i, v in enumerate(vals):
        out = jnp.where(sub == i, v, out)
    return out


def _route_select_kernel(lgt_ref, bias_ref, e_ref, g_ref, rank_ref, cnt_ref, carry_ref):
    @pl.when(pl.program_id(0) == 0)
    def _():
        carry_ref[...] = jnp.zeros_like(carry_ref)

    t = lgt_ref.shape[1]
    aff = _sigmoid(lgt_ref[...])
    sel = aff + bias_ref[...]
    gsz = EXPERTS_PER_GROUP
    sub = _iota2((gsz, t), 0).astype(F32)
    best = e1 = e2 = None
    for g in range(N_EXPERT_GROUPS):
        s = sel[g * gsz:(g + 1) * gsz, :]
        m1 = jnp.max(s, axis=0, keepdims=True)
        i1 = jnp.min(jnp.where(s == m1, sub, float(gsz)), axis=0, keepdims=True)
        s2 = jnp.where(sub == i1, -jnp.inf, s)
        m2 = jnp.max(s2, axis=0, keepdims=True)
        i2 = jnp.min(jnp.where(s2 == m2, sub, float(gsz)), axis=0, keepdims=True)
        score = m1 + m2
        if g == 0:
            best, e1, e2 = score, i1, i2
        else:
            better = score > best
            best = jnp.where(better, score, best)
            e1 = jnp.where(better, i1 + float(g * gsz), e1)
            e2 = jnp.where(better, i2 + float(g * gsz), e2)
    row = _iota2((N_EXPERTS, t), 0).astype(F32)
    oh1 = jnp.where(row == e1, 1.0, 0.0)
    oh2 = jnp.where(row == e2, 1.0, 0.0)
    w1 = jnp.sum(oh1 * aff, axis=0, keepdims=True)
    w2 = jnp.sum(oh2 * aff, axis=0, keepdims=True)
    den = w1 + w2
    cnt = oh1 + oh2
    before = _bf(jnp.where(_iota2((t, t), 0) < _iota2((t, t), 1), 1.0, 0.0))
    prefix = _dot(_bf(cnt), before) + carry_ref[...]
    r1 = jnp.sum(oh1 * prefix, axis=0, keepdims=True)
    r2 = jnp.sum(oh2 * prefix, axis=0, keepdims=True)
    carry_ref[...] = carry_ref[...] + jnp.sum(cnt, axis=1, keepdims=True)
    e_ref[...] = _rows(e1, e2)
    g_ref[...] = _rows(w1 / den, w2 / den)
    rank_ref[...] = _rows(r1, r2)
    cnt_ref[...] = jnp.broadcast_to(carry_ref[...], cnt_ref.shape)


def _route_select(lgt, bias_col, tile):
    n = lgt.shape[1]
    col = lambda r: pl.BlockSpec((r, tile), lambda i: (0, i))
    out = jax.ShapeDtypeStruct((SUBLANES, n), F32)
    return pl.pallas_call(
        _route_select_kernel,
        grid=(n // tile,),
        in_specs=[col(N_EXPERTS), pl.BlockSpec((N_EXPERTS, 1), lambda i: (0, 0))],
        out_specs=[col(SUBLANES)] * 3 + [pl.BlockSpec((N_EXPERTS, LANES), lambda i: (0, 0))],
        out_shape=[out] * 3 + [jax.ShapeDtypeStruct((N_EXPERTS, LANES), F32)],
        scratch_shapes=[pltpu.VMEM((N_EXPERTS, 1), F32)],
        compiler_params=_params("arbitrary"),
        name="route_select",
    )(lgt, bias_col)


def _route_place_kernel(e_ref, rank_ref, cnt_ref, dest_ref, be_ref, nu_ref):
    t = e_ref.shape[1]
    blk = float(MOE_BLOCK)
    counts = cnt_ref[...]
    padded = jnp.floor((counts + (blk - 1.0)) / blk) * blk
    ne = N_EXPERTS
    lower = _bf(jnp.where(_iota2((ne, ne), 0) > _iota2((ne, ne), 1), 1.0, 0.0))
    pad_start = _dot_sel_rhs(lower, padded)
    pad_end = pad_start + padded
    row = _iota2((ne, t), 0).astype(F32)
    ps_t = jnp.concatenate([pad_start] * (t // LANES), axis=1)
    e = e_ref[...]
    rank = rank_ref[...]
    dests = []
    for k in range(TOP_K):
        oh = jnp.where(row == e[k:k + 1, :], 1.0, 0.0)
        dests.append(jnp.sum(oh * ps_t, axis=0, keepdims=True) + rank[k:k + 1, :])
    dest_ref[...] = _rows(*dests).astype(jnp.int32)
    nbp = be_ref.shape[1]
    blk_start = _iota2((ne, nbp), 1).astype(F32) * blk
    pe_t = jnp.concatenate([pad_end] * (nbp // LANES), axis=1)
    be = jnp.sum(jnp.where(pe_t <= blk_start, 1.0, 0.0), axis=0, keepdims=True)
    be_ref[...] = jnp.broadcast_to(jnp.minimum(be, float(ne - 1)), be_ref.shape).astype(jnp.int32)
    nu_ref[...] = jnp.broadcast_to(pad_end[ne - 1:ne, :] / blk, nu_ref.shape).astype(jnp.int32)


def _route_place(e, rank, cnt, tile, n_blocks):
    n = e.shape[1]
    nbp = -(-n_blocks // LANES) * LANES
    col = pl.BlockSpec((SUBLANES, tile), lambda i: (0, i))
    return pl.pallas_call(
        _route_place_kernel,
        grid=(n // tile,),
        in_specs=[col, col, pl.BlockSpec((N_EXPERTS, LANES), lambda i: (0, 0))],
        out_specs=[col, pl.BlockSpec((SUBLANES, nbp), lambda i: (0, 0)),
                   pl.BlockSpec((SUBLANES, LANES), lambda i: (0, 0))],
        out_shape=[jax.ShapeDtypeStruct((SUBLANES, n), jnp.int32),
                   jax.ShapeDtypeStruct((SUBLANES, nbp), jnp.int32),
                   jax.ShapeDtypeStruct((SUBLANES, LANES), jnp.int32)],
        compiler_params=_params("arbitrary"),
        name="route_place",
    )(e, rank, cnt)


def _moe_kernel(be_ref, nu_ref, xb_ref, wg_ref, wu_ref, wd_ref, yb_ref, wgb, wub, wdb):
    i = pl.program_id(0)
    new_expert = jnp.logical_or(i == 0, be_ref[i] != be_ref[jnp.maximum(i - 1, 0)])

    @pl.when(new_expert)
    def _():
        wgb[...] = _bf(wg_ref[0])
        wub[...] = _bf(wu_ref[0])
        wdb[...] = _bf(wd_ref[0])

    @pl.when(i < nu_ref[0])
    def _():
        x = _bf(xb_ref[...])
        gt = _dot(x, wgb[...])
        h = gt * _sigmoid(gt) * _dot(x, wub[...])
        yb_ref[...] = _dot(_bf(h), wdb[...])

    @pl.when(i >= nu_ref[0])
    def _():
        yb_ref[...] = jnp.zeros_like(yb_ref)


def _moe(block_expert, n_used, xb, wg, wu, wd):
    rows, d = xb.shape
    blk = MOE_BLOCK
    nb = rows // blk
    grid_spec = pltpu.PrefetchScalarGridSpec(
        num_scalar_prefetch=2,
        grid=(nb,),
        in_specs=[pl.BlockSpec((blk, d), lambda i, be, nu: (i, 0)),
                  pl.BlockSpec((1, d, D_EXPERT), lambda i, be, nu: (be[i], 0, 0)),
                  pl.BlockSpec((1, d, D_EXPERT), lambda i, be, nu: (be[i], 0, 0)),
                  pl.BlockSpec((1, D_EXPERT, d), lambda i, be, nu: (be[i], 0, 0))],
        out_specs=pl.BlockSpec((blk, d), lambda i, be, nu: (i, 0)),
        scratch_shapes=[pltpu.VMEM((d, D_EXPERT), BF16), pltpu.VMEM((d, D_EXPERT), BF16),
                        pltpu.VMEM((D_EXPERT, d), BF16)])
    return pl.pallas_call(
        _moe_kernel,
        grid_spec=grid_spec,
        out_shape=jax.ShapeDtypeStruct((rows, d), F32),
        compiler_params=_params("arbitrary"),
        name="moe_experts",
    )(block_expert, n_used, xb, wg, wu, wd)


def _dispatch_kernel(dest_ref, x_ref, buf_in_hbm, buf_hbm, sem, *, tile, n):
    del buf_in_hbm
    base = pl.program_id(0) * tile

    def row_copy(j, k):
        return pltpu.make_async_copy(x_ref.at[pl.ds(j, 1)],
                                     buf_hbm.at[pl.ds(dest_ref[k * n + base + j], 1)], sem)

    def start(j, carry):
        for k in range(TOP_K):
            row_copy(j, k).start(priority=k)
        return carry

    def wait(j, carry):
        for k in range(TOP_K):
            row_copy(j, k).wait()
        return carry

    lax.fori_loop(0, tile, start, 0, unroll=DMA_UNROLL)
    lax.fori_loop(0, tile, wait, 0, unroll=DMA_UNROLL)


def _dispatch(dest_flat, x, rows, tile):
    n, d = x.shape
    any_spec = pl.BlockSpec(memory_space=pl.ANY)
    grid_spec = pltpu.PrefetchScalarGridSpec(
        num_scalar_prefetch=1,
        grid=(n // tile,),
        in_specs=[pl.BlockSpec((tile, d), lambda i, dest: (i, 0)), any_spec],
        out_specs=any_spec,
        scratch_shapes=[pltpu.SemaphoreType.DMA(())])
    return pl.pallas_call(
        functools.partial(_dispatch_kernel, tile=tile, n=n),
        grid_spec=grid_spec,
        out_shape=jax.ShapeDtypeStruct((rows, d), x.dtype),
        input_output_aliases={2: 0},
        compiler_params=_params("arbitrary"),
        name="moe_dispatch",
    )(dest_flat, x, jnp.zeros((rows, d), x.dtype))


def _combine_ln2_kernel(dest_ref, x_ref, gate_ref, yb_hbm, g_ref, b_ref, o_ref, ob_ref, buf, sems,
                        *, tile, n, alpha):
    i = pl.program_id(0)
    steps = pl.num_programs(0)

    def row_copy(step, slot, j, k):
        t = step * tile + j
        return pltpu.make_async_copy(yb_hbm.at[pl.ds(dest_ref[k * n + t], 1)],
                                     buf.at[slot, k, pl.ds(j, 1)], sems.at[slot])

    def start_tile(step, slot):
        def body(j, carry):
            for k in range(TOP_K):
                row_copy(step, slot, j, k).start(priority=k)
            return carry
        lax.fori_loop(0, tile, body, 0, unroll=DMA_UNROLL)

    def wait_tile(step, slot):
        def body(j, carry):
            for k in range(TOP_K):
                row_copy(step, slot, j, k).wait()
            return carry
        lax.fori_loop(0, tile, body, 0, unroll=DMA_UNROLL)

    slot = lax.rem(i, 2)

    @pl.when(i == 0)
    def _():
        start_tile(0, 0)

    for s in range(2):
        @pl.when(slot == s)
        def _():
            @pl.when(i + 1 < steps)
            def _():
                start_tile(i + 1, 1 - s)
            wait_tile(i, s)

    gate = gate_ref[...]
    ffn = gate[:, 0:1] * buf[slot, 0] + gate[:, 1:2] * buf[slot, 1]
    y = _layer_norm(alpha * x_ref[...] + ffn, g_ref[...], b_ref[...])
    o_ref[...] = y
    ob_ref[...] = _bf(y)


def _combine_ln2(dest_flat, x, gate, yb, g, b, tile, alpha):
    n, d = x.shape
    row = pl.BlockSpec((tile, d), lambda i, dest: (i, 0))
    par = pl.BlockSpec((1, d), lambda i, dest: (0, 0))
    grid_spec = pltpu.PrefetchScalarGridSpec(
        num_scalar_prefetch=1,
        grid=(n // tile,),
        in_specs=[row, pl.BlockSpec((tile, TOP_K), lambda i, dest: (i, 0)),
                  pl.BlockSpec(memory_space=pl.ANY), par, par],
        out_specs=[row, row],
        scratch_shapes=[pltpu.VMEM((2, TOP_K, tile, d), F32), pltpu.SemaphoreType.DMA((2,))])
    return pl.pallas_call(
        functools.partial(_combine_ln2_kernel, tile=tile, n=n, alpha=alpha),
        grid_spec=grid_spec,
        out_shape=[jax.ShapeDtypeStruct((n, d), F32), jax.ShapeDtypeStruct((n, d), BF16)],
        compiler_params=_params("arbitrary"),
        name="moe_combine_ln2",
    )(dest_flat, x, gate, yb, g, b)


def _route(lgt, bias_col, tile):
    n = lgt.shape[1]
    n_blocks = -(-(n * TOP_K) // MOE_BLOCK) + N_EXPERTS
    e, g, rank, cnt = _route_select(lgt, bias_col, tile)
    dest, be, nu = _route_place(e, rank, cnt, tile, n_blocks)
    return g[:TOP_K].T, dest[:TOP_K].reshape(-1), be[0, :n_blocks], nu[0, :1], n_blocks


def _proj_tn(width):
    blocks = width // LANES
    best = max(d for d in range(1, blocks + 1) if blocks % d == 0 and d * LANES <= 1024)
    return best * LANES


def _tiles(B, S):
    n = B * S
    return dict(
        tm_proj=min(1024, n),
        ts_prep=min(512, S),
        rwkv_chunks=min(8, S // RWKV_CHUNK),
        tq=min(256, S),
        tm_merge=min(512, n),
        t_route=min(512, n),
        t_dispatch=min(512, n),
        t_combine=min(256, n),
    )


def _pad_rows(a, before, total):
    return jnp.zeros((total, a.shape[1]), a.dtype).at[before:before + a.shape[0]].set(a)


def _block_diag_ones(dim):
    i = jnp.arange(dim) // HEAD
    return (i[:, None] == i[None, :]).astype(BF16)


def kernel(x, w_in, rwkv_mu, rwkv_w0, rwkv_w_up, rwkv_a0, rwkv_a_up, rwkv_g_up, rwkv_k_k, rwkv_k_a,
           rwkv_r_k, rwkv_gn_g, rwkv_gn_b, fox_f_bias, ssm_conv_w, ssm_conv_b, ssm_dt_bias, ssm_a_log,
           ssm_d, ssm_norm_g, gate_bias, w_br_rwkv, w_br_fox, w_br_ssm, w_out, ln1_g, ln1_b, router_w,
           router_bias, exp_w_gate, exp_w_up, exp_w_down, ln2_g, ln2_b):
    B, S, D = x.shape
    depth = w_in.shape[0]
    n = B * S
    t = _tiles(B, S)
    alpha = (2 * depth) ** 0.25

    row1 = lambda a: a.reshape(1, -1).astype(F32)
    bd512 = _block_diag_ones(RWKV_DIM)
    expand = (jnp.arange(LANES)[:, None] == DT_OFF + jnp.arange(SSM_DIM)[None, :] // HEAD).astype(BF16)
    rw = router_w.astype(F32).T
    rwh = rw.astype(BF16)
    rwl = (rw - rwh.astype(F32)).astype(BF16)
    rb_col = router_bias.astype(F32).reshape(N_EXPERTS, 1)

    xf = x.reshape(n, D).astype(F32)
    xb16 = xf.astype(BF16)
    for l in range(depth):
        w = w_in[l]
        w_small = jnp.concatenate([w[:, OFF_SSM - FOX_HEADS:OFF_SSM], w[:, OFF_GATE - SSM_HEADS:OFF_GATE]], axis=1)
        w_small_l = jnp.zeros((D, LANES), F32).at[:, :w_small.shape[1]].set(w_small).astype(BF16)
        w_small_t = jnp.zeros((SMALL_ROWS, D), F32).at[:w_small.shape[1]].set(w_small.T).astype(BF16)
        mm = lambda lo, hi: _matmul(xb16, w[:, lo:hi].astype(BF16), t["tm_proj"], _proj_tn(hi - lo), BF16)
        p_rwkv = mm(0, RWKV_COLS)
        p_fox = mm(OFF_FOX, OFF_FOX + 3 * FOX_DIM)
        p_z = mm(OFF_SSM, OFF_SSM + SSM_DIM)
        p_xbc = mm(OFF_SSM + SSM_DIM, OFF_SSM + SSM_DIM + SSM_CONV_DIM)
        p_gate = mm(OFF_GATE, OFF_GATE + 3 * D)
        ps, pst = _small_proj(xb16, w_small_l, w_small_t, t["tm_proj"])

        wup = _pad_rows(rwkv_w_up[l], 0, DECAY_LORA + ICLR_LORA).astype(BF16)
        aup = _pad_rows(rwkv_a_up[l], DECAY_LORA, DECAY_LORA + ICLR_LORA).astype(BF16)
        r, lw, k, v, kn, b, g = _rwkv_prep(
            p_rwkv, B, S, t["ts_prep"], row1(rwkv_mu[l]), row1(rwkv_w0[l]), wup, row1(rwkv_a0[l]), aup,
            rwkv_g_up[l].astype(BF16), row1(rwkv_k_k[l]), row1(rwkv_k_a[l]), bd512)
        o_r = _rwkv_chunk(r, lw, k, v, kn, b, g, row1(rwkv_gn_g[l]), row1(rwkv_gn_b[l]),
                          row1(rwkv_r_k[l]), B, S, t["rwkv_chunks"])

        fb_lane = jnp.zeros((1, LANES), F32).at[0, :FOX_HEADS].set(fox_f_bias[l].astype(F32))
        fb_col = jnp.zeros((SMALL_ROWS, 1), F32).at[:FOX_HEADS, 0].set(fox_f_bias[l].astype(F32))
        cc, cr = _fox_c(ps, pst, fb_lane, fb_col, B, S)
        o_f = _fox_attn(p_fox, cc, cr, B, S, t["tq"])

        dtb = ssm_dt_bias[l].astype(F32)
        a_neg = -jnp.exp(ssm_a_log[l].astype(F32))
        dtbl = jnp.zeros((1, LANES), F32).at[0, DT_OFF:DT_OFF + SSM_HEADS].set(dtb)
        dtbc = jnp.zeros((SMALL_ROWS, 1), F32).at[DT_OFF:DT_OFF + SSM_HEADS, 0].set(dtb)
        ac = jnp.zeros((SMALL_ROWS, 1), F32).at[DT_OFF:DT_OFF + SSM_HEADS, 0].set(a_neg)
        al = jnp.repeat(a_neg, HEAD).reshape(1, SSM_DIM)
        dl = jnp.repeat(ssm_d[l].astype(F32), HEAD).reshape(1, SSM_DIM)
        o_m = _mamba(p_z, p_xbc, ps, pst, ssm_conv_w[l].astype(F32), row1(ssm_conv_b[l]), dtbl, dtbc,
                     al, ac, dl, row1(ssm_norm_g[l]), expand, B, S)

        x1, logits = _merge(
            o_r, o_f, o_m, p_gate, xf, w_br_rwkv[l].astype(BF16), w_br_fox[l].astype(BF16),
            w_br_ssm[l].astype(BF16), w_out[l].astype(BF16), row1(gate_bias[l]), row1(ln1_g[l]),
            row1(ln1_b[l]), rwh, rwl, t["tm_merge"], alpha)

        gate, dest, block_expert, n_used, n_blocks = _route(logits, rb_col, t["t_route"])
        xb = _dispatch(dest, x1, n_blocks * MOE_BLOCK, t["t_dispatch"])
        yb = _moe(block_expert, n_used, xb, exp_w_gate[l], exp_w_up[l], exp_w_down[l])
        xf, xb16 = _combine_ln2(dest, x1, gate, yb, row1(ln2_g[l]), row1(ln2_b[l]), t["t_combine"], alpha)
    return xf.reshape(B, S, D).astype(x.dtype)
```

```python
import functools
import math

import jax
import jax.numpy as jnp
from jax import lax
from jax.experimental import pallas as pl
from jax.experimental.pallas import tpu as pltpu

F32 = jnp.float32
BF16 = jnp.bfloat16

LANES = 128
SUBLANES = 8
VMEM_LIMIT = 48 * 1024 * 1024

D_MODEL = 1024
HEAD = 64
PAIR = LANES // HEAD
RWKV_DIM = 512
DECAY_LORA = 64
ICLR_LORA = 64
GATE_LORA = 128
RWKV_COLS = 3 * RWKV_DIM + DECAY_LORA + ICLR_LORA + GATE_LORA
RWKV_GN_EPS = 64e-5
FOX_DIM = 512
FOX_HEADS = 8
SSM_DIM = 1024
SSM_HEADS = 16
SSM_GROUPS = 2
SSM_STATE = 128
CONV_WIDTH = 4
SSM_CONV_DIM = SSM_DIM + 2 * SSM_GROUPS * SSM_STATE
NORM_EPS = 1e-5
N_EXPERTS = 32
N_EXPERT_GROUPS = 4
EXPERTS_PER_GROUP = N_EXPERTS // N_EXPERT_GROUPS
TOP_K = 2
D_EXPERT = 512
LN_EPS = 1e-5
LOG2E = math.log2(math.e)
OFF_FOX = RWKV_COLS
OFF_SSM = OFF_FOX + 3 * FOX_DIM + FOX_HEADS
OFF_GATE = OFF_SSM + SSM_DIM + SSM_CONV_DIM + SSM_HEADS

SMALL_ROWS = 32
DT_OFF = 8
FOX_WIDTH = 512
RWKV_CHUNK = 64
RWKV_GROUP = 8
SSD_CHUNK = 128
MOE_BLOCK = 256
DMA_UNROLL = 16


def _bf(x):
    return x.astype(BF16)


def _dot(a, b):
    return jnp.dot(a, b, preferred_element_type=F32)


def _dot_nt(a, b):
    return lax.dot_general(a, b, (((1,), (1,)), ((), ())), preferred_element_type=F32)


def _dot_tn(a, b):
    return lax.dot_general(a, b, (((0,), (0,)), ((), ())), preferred_element_type=F32)


def _split(x, parts):
    out = []
    rem = x
    for _ in range(parts):
        hi = _bf(rem)
        out.append(hi)
        rem = rem - hi.astype(F32)
    return out


def _dot_sel_rhs(m01, x, parts=3):
    acc = None
    for part in _split(x, parts):
        t = _dot(m01, part)
        acc = t if acc is None else acc + t
    return acc


def _dot_sel_lhs(x, m01, parts=3):
    acc = None
    for part in _split(x, parts):
        t = _dot(part, m01)
        acc = t if acc is None else acc + t
    return acc


def _sigmoid(x):
    return 1.0 / (1.0 + jnp.exp(-x))


def _softplus(x):
    return jnp.maximum(x, 0.0) + jnp.log(1.0 + jnp.exp(-jnp.abs(x)))


def _log_sigmoid(x):
    return jnp.minimum(x, 0.0) - jnp.log(1.0 + jnp.exp(-jnp.abs(x)))


def _iota2(shape, dim):
    return lax.broadcasted_iota(jnp.int32, shape, dim)


def _params(*sem):
    return pltpu.CompilerParams(dimension_semantics=sem, vmem_limit_bytes=VMEM_LIMIT)


def _mm_kernel(x_ref, w_ref, o_ref):
    o_ref[...] = _dot(x_ref[...], w_ref[...]).astype(o_ref.dtype)


def _matmul(x, w, tm, tn, out_dtype=F32):
    n, k = x.shape
    nc = w.shape[1]
    return pl.pallas_call(
        _mm_kernel,
        grid=(n // tm, nc // tn),
        in_specs=[pl.BlockSpec((tm, k), lambda i, j: (i, 0)),
                  pl.BlockSpec((k, tn), lambda i, j: (0, j))],
        out_specs=pl.BlockSpec((tm, tn), lambda i, j: (i, j)),
        out_shape=jax.ShapeDtypeStruct((n, nc), out_dtype),
        compiler_params=_params("parallel", "parallel"),
        name="in_proj",
    )(x, w)


def _small_proj_kernel(x_ref, w_ref, wt_ref, o_ref, ot_ref):
    x = x_ref[...]
    o_ref[...] = _dot(x, w_ref[...])
    ot_ref[...] = _dot_nt(wt_ref[...], x)


def _small_proj(x, w, wt, tm):
    n, k = x.shape
    r = wt.shape[0]
    return pl.pallas_call(
        _small_proj_kernel,
        grid=(n // tm,),
        in_specs=[pl.BlockSpec((tm, k), lambda i: (i, 0)),
                  pl.BlockSpec((k, LANES), lambda i: (0, 0)),
                  pl.BlockSpec((r, k), lambda i: (0, 0))],
        out_specs=[pl.BlockSpec((tm, LANES), lambda i: (i, 0)),
                   pl.BlockSpec((r, tm), lambda i: (0, i))],
        out_shape=[jax.ShapeDtypeStruct((n, LANES), F32), jax.ShapeDtypeStruct((r, n), F32)],
        compiler_params=_params("parallel"),
        name="in_proj_small",
    )(x, w, wt)


def _rwkv_prep_kernel(p_ref, mu_ref, w0_ref, wup_ref, a0_ref, aup_ref, gup_ref, kk_ref, ka_ref,
                      bd_ref, r_o, lw_o, k_o, v_o, kn_o, b_o, g_o, carry_ref):
    @pl.when(pl.program_id(1) == 0)
    def _():
        carry_ref[...] = jnp.zeros_like(carry_ref)

    p = p_ref[...].astype(F32)
    ts = p.shape[0]
    row = _iota2((ts, 1), 0)
    prev = jnp.where(row == 0, carry_ref[...], pltpu.roll(p, 1, axis=0))
    carry_ref[...] = p[ts - 1:ts, :]
    p = p + mu_ref[...] * (prev - p)

    R = RWKV_DIM
    r = p[:, 0:R]
    k = p[:, R:2 * R]
    v = p[:, 2 * R:3 * R]
    lora = p[:, 3 * R:3 * R + DECAY_LORA + ICLR_LORA]
    gd = p[:, 3 * R + DECAY_LORA + ICLR_LORA:]
    w_pre = w0_ref[...] + _dot(_bf(jnp.tanh(lora)), wup_ref[...])
    lw = -jnp.exp(_log_sigmoid(w_pre) - 0.5)
    a = _sigmoid(a0_ref[...] + _dot(_bf(lora), aup_ref[...]))
    g = _dot(_bf(_sigmoid(gd)), gup_ref[...])
    kk = k * kk_ref[...]
    ss = _dot_sel_lhs(kk * kk, bd_ref[...], parts=2)
    kn = kk / jnp.maximum(jnp.sqrt(ss), 1e-12)
    r_o[...] = _bf(r)
    lw_o[...] = lw
    k_o[...] = _bf(k * (1.0 + (a - 1.0) * ka_ref[...]))
    v_o[...] = _bf(v)
    kn_o[...] = _bf(kn)
    b_o[...] = _bf(kn * a)
    g_o[...] = _bf(g)


def _rwkv_prep(p, B, S, ts, mu, w0, wup, a0, aup, gup, k_k, k_a, bd):
    n = B * S
    ns = S // ts
    row_spec = lambda w: pl.BlockSpec((ts, w), lambda b, s: (b * ns + s, 0))
    full = lambda a: pl.BlockSpec(a.shape, lambda b, s: (0,) * a.ndim)
    out = lambda dt: jax.ShapeDtypeStruct((n, RWKV_DIM), dt)
    return pl.pallas_call(
        _rwkv_prep_kernel,
        grid=(B, ns),
        in_specs=[row_spec(RWKV_COLS)] + [full(a) for a in (mu, w0, wup, a0, aup, gup, k_k, k_a, bd)],
        out_specs=[row_spec(RWKV_DIM)] * 7,
        out_shape=[out(BF16), out(F32)] + [out(BF16)] * 5,
        scratch_shapes=[pltpu.VMEM((1, RWKV_COLS), F32)],
        compiler_params=_params("parallel", "arbitrary"),
        name="rwkv_prep",
    )(p, mu, w0, wup, a0, aup, gup, k_k, k_a, bd)


def _rwkv_chunk_kernel(r_ref, lw_ref, k_ref, v_ref, kn_ref, b_ref, g_ref, gng_ref, gnb_ref, rk_ref,
                       o_ref, s_ref, *, n_chunks):
    C = RWKV_CHUNK

    @pl.when(pl.program_id(2) == 0)
    def _():
        s_ref[...] = jnp.zeros_like(s_ref)

    lane = _iota2((1, LANES), 1)
    m0 = (lane < HEAD).astype(F32)
    m1 = 1.0 - m0
    ri = _iota2((LANES, LANES), 0)
    ci = _iota2((LANES, LANES), 1)
    same = (ri >= HEAD) == (ci >= HEAD)
    smask = jnp.where(same & (ri > ci), 1.0, 0.0)
    imask = jnp.where(same & (ri >= ci), 1.0, 0.0)
    eye = jnp.where(ri == ci, 1.0, 0.0)
    bd_mean = _bf(jnp.where(same, 1.0 / HEAD, 0.0))
    bd_one = _bf(jnp.where(same, 1.0, 0.0))
    tri = _bf(jnp.where(_iota2((C, C), 0) >= _iota2((C, C), 1), 1.0, 0.0))

    def stack(x):
        return jnp.concatenate([x * m0, x * m1], axis=0)

    gng = gng_ref[...]
    gnb = gnb_ref[...]
    rk = rk_ref[...]

    def group(chunks):
        ld = lambda ref, c: ref[c * C:(c + 1) * C, :]
        r = [ld(r_ref, c).astype(F32) for c in chunks]
        lw = [ld(lw_ref, c) for c in chunks]
        k = [ld(k_ref, c).astype(F32) for c in chunks]
        v = [ld(v_ref, c).astype(F32) for c in chunks]
        kn = [ld(kn_ref, c).astype(F32) for c in chunks]
        b = [ld(b_ref, c).astype(F32) for c in chunks]
        idx = range(len(chunks))

        cs = [_dot_sel_rhs(tri, lw[i]) for i in idx]
        Rs, As, Ks, Bs, Rt, At, Be, Ke, Vs, g_end = ([] for _ in range(10))
        for i in idx:
            sh = cs[i][C // 2 - 1:C // 2, :]
            cs_end = cs[i][C - 1:C, :]
            e_pos = jnp.exp(cs[i] - sh)
            e_neg = jnp.exp(sh - cs[i])
            e_prev = jnp.exp(cs[i] - lw[i] - sh)
            e_end = jnp.exp(cs_end - cs[i])
            Rs.append(_bf(stack(r[i] * e_pos)))
            As.append(_bf(stack(-kn[i] * e_prev)))
            Ks.append(_bf(stack(k[i] * e_neg)))
            Bs.append(_bf(stack(b[i] * e_neg)))
            Rt.append(stack(r[i] * jnp.exp(cs[i])))
            At.append(_bf(stack(-kn[i] * jnp.exp(cs[i] - lw[i]))))
            Be.append(_bf(stack(b[i] * e_end)))
            Ke.append(_bf(stack(k[i] * e_end)))
            Vs.append(_bf(stack(v[i])))
            g_end.append(jnp.exp(cs_end))

        lanes2 = lambda x, y: jnp.concatenate([x, y], axis=1)
        L = LANES
        BK = [jnp.concatenate([Bs[i], Ks[i]], axis=0) for i in idx]
        GA = [_dot_nt(As[i], BK[i]) for i in idx]
        GR = [_dot_nt(Rs[i], BK[i]) for i in idx]
        Aab = [GA[i][:, :L] * smask for i in idx]
        Aak = [_bf(GA[i][:, L:] * smask) for i in idx]
        RBb = [_bf(GR[i][:, :L] * imask) for i in idx]
        RKb = [_bf(GR[i][:, L:] * imask) for i in idx]
        W1b = [_bf(_dot(Aak[i], Vs[i])) for i in idx]

        T = [eye + Aab[i] for i in idx]
        Pb = [_bf(Aab[i]) for i in idx]
        Pb = [_bf(_dot(Pb[i], Pb[i])) for i in idx]
        for _ in range(int(math.log2(C)) - 2):
            PT = [_dot(Pb[i], lanes2(Pb[i], _bf(T[i]))) for i in idx]
            Pb = [_bf(PT[i][:, :L]) for i in idx]
            T = [T[i] + PT[i][:, L:] for i in idx]
        T = [T[i] + _dot(Pb[i], _bf(T[i])) for i in idx]
        Tb = [_bf(T[i]) for i in idx]

        TU = [_dot(Tb[i], lanes2(At[i], W1b[i])) for i in idx]
        TUb = [_bf(TU[i]) for i in idx]
        TAb = [TUb[i][:, :L] for i in idx]
        U0b = [TUb[i][:, L:] for i in idx]
        RTU = [_dot(RBb[i], TUb[i]) for i in idx]
        Q = [_bf(Rt[i] + RTU[i][:, :L]) for i in idx]
        Y0 = [_dot(RKb[i], Vs[i]) + RTU[i][:, L:] for i in idx]
        Mt = [_bf(_dot_tn(TAb[i], Be[i])) for i in idx]
        Nt = [_dot_tn(Vs[i], Ke[i]) + _dot_tn(U0b[i], Be[i]) for i in idx]

        Y = []
        S0 = s_ref[...]
        for i in idx:
            S0b = _bf(S0)
            Y.append(_dot_nt(Q[i], S0b) + Y0[i])
            S0 = S0 * g_end[i] + _dot(S0b, Mt[i]) + Nt[i]
        s_ref[...] = S0

        rows = slice(chunks[0] * C, (chunks[-1] + 1) * C)
        cat = lambda xs: jnp.concatenate(xs, axis=0)
        y = cat([Y[i][:C, :] + Y[i][C:, :] for i in idx])
        r_all, k_all, v_all = cat(r), cat(k), cat(v)
        mean = _dot_sel_lhs(y, bd_mean, parts=2)
        yc = y - mean
        var = _dot_sel_lhs(yc * yc, bd_mean, parts=2)
        yn = yc * lax.rsqrt(var + RWKV_GN_EPS) * gng + gnb
        bonus = _dot_sel_lhs(r_all * k_all * rk, bd_one, parts=2) * v_all
        o_ref[rows, :] = _bf((yn + bonus) * g_ref[rows, :].astype(F32))

    for first in range(0, n_chunks, RWKV_GROUP):
        group(list(range(first, min(first + RWKV_GROUP, n_chunks))))


def _rwkv_chunk(r, lw, k, v, kn, b, g, gn_g, gn_b, r_k, B, S, n_chunks):
    n = B * S
    rows = n_chunks * RWKV_CHUNK
    ns = S // rows
    npair = RWKV_DIM // LANES
    blk = pl.BlockSpec((rows, LANES), lambda bb, p, s: (bb * ns + s, p))
    par = pl.BlockSpec((1, LANES), lambda bb, p, s: (0, p))
    return pl.pallas_call(
        functools.partial(_rwkv_chunk_kernel, n_chunks=n_chunks),
        grid=(B, npair, ns),
        in_specs=[blk] * 7 + [par] * 3,
        out_specs=blk,
        out_shape=jax.ShapeDtypeStruct((n, RWKV_DIM), BF16),
        scratch_shapes=[pltpu.VMEM((LANES, LANES), F32)],
        compiler_params=_params("parallel", "parallel", "arbitrary"),
        name="rwkv_chunk",
    )(r, lw, k, v, kn, b, g, gn_g, gn_b, r_k)


def _fox_c_kernel(ps_ref, pst_ref, bl_ref, bc_ref, cc_ref, cr_ref, *, seq):
    L = LANES
    ri = _iota2((L, L), 0)
    ci = _iota2((L, L), 1)
    tril = _bf(jnp.where(ri >= ci, 1.0, 0.0))
    triu = _bf(jnp.where(ri <= ci, 1.0, 0.0))
    carry = jnp.zeros((1, L), F32)
    carry_c = jnp.zeros((SMALL_ROWS, 1), F32)
    for i in range(seq // L):
        rows = slice(i * L, (i + 1) * L)
        cs = _dot_sel_rhs(tril, _log_sigmoid(ps_ref[rows, :] + bl_ref[...])) + carry
        cc_ref[rows, :] = cs
        carry = cs[L - 1:L, :]
        cst = _dot_sel_lhs(_log_sigmoid(pst_ref[:, rows] + bc_ref[...]), triu) + carry_c
        cr_ref[:, rows] = cst
        carry_c = cst[:, L - 1:L]


def _fox_c(ps, pst, bias_lane, bias_col, B, S):
    n = B * S
    return pl.pallas_call(
        functools.partial(_fox_c_kernel, seq=S),
        grid=(B,),
        in_specs=[pl.BlockSpec((S, LANES), lambda b: (b, 0)),
                  pl.BlockSpec((SMALL_ROWS, S), lambda b: (0, b)),
                  pl.BlockSpec((1, LANES), lambda b: (0, 0)),
                  pl.BlockSpec((SMALL_ROWS, 1), lambda b: (0, 0))],
        out_specs=[pl.BlockSpec((S, LANES), lambda b: (b, 0)),
                   pl.BlockSpec((SMALL_ROWS, S), lambda b: (0, b))],
        out_shape=[jax.ShapeDtypeStruct((n, LANES), F32),
                   jax.ShapeDtypeStruct((SMALL_ROWS, n), F32)],
        compiler_params=_params("parallel"),
        name="fox_cumgate",
    )(ps, pst, bias_lane, bias_col)


def _transpose_blocks(x):
    n = x.shape[0] // LANES
    return jnp.concatenate([x[i * LANES:(i + 1) * LANES, :].T for i in range(n)], axis=1)


def _fox_attn_kernel(q_ref, k_ref, v_ref, cc_ref, cr_ref, o_ref, vt_ref, ck_ref, s_scr, p_scr, *, tq, seq):
    nblk = q_ref.shape[1] // LANES
    heads = nblk * PAIR
    head0 = pl.program_id(1) * heads
    qi = pl.program_id(2)

    @pl.when(qi == 0)
    def _():
        for i in range(seq // LANES):
            rows = slice(i * LANES, (i + 1) * LANES)
            for c in range(nblk):
                cols = slice(c * LANES, (c + 1) * LANES)
                vt_ref[cols, rows] = _bf(v_ref[rows, cols].astype(F32).T)
        ri = _iota2((LANES, LANES), 0)
        for h in range(heads):
            sel = _bf(jnp.where(ri == head0 + h, 1.0, 0.0))
            ck_ref[h] = _dot_sel_lhs(cc_ref[...], sel) * LOG2E

    in0 = _iota2((LANES, 1), 0) < HEAD
    qts = []
    for c in range(nblk):
        qt = _transpose_blocks(q_ref[:, c * LANES:(c + 1) * LANES].astype(F32) * (HEAD ** -0.5 * LOG2E))
        qts += [_bf(jnp.where(in0, qt, 0.0)), _bf(jnp.where(in0, 0.0, qt))]
    q0 = pl.multiple_of(qi * tq, tq)
    cqs = tuple(cr_ref[pl.ds(head0 + h, 1), pl.ds(q0, tq)] * LOG2E for h in range(heads))
    key_minus_query = _iota2((tq, tq), 0) - _iota2((tq, tq), 1)
    last_block = seq // tq - 1

    s_scr[...] = jnp.full(s_scr.shape, -jnp.inf, F32)
    p_scr[...] = jnp.zeros(p_scr.shape, BF16)

    def step(j, carry, diagonal=False, scores=True, softmax=True):
        pv0 = pl.multiple_of(jnp.clip(j - 2, 0, last_block) * tq, tq)
        if scores:
            qk0 = pl.multiple_of(j * tq, tq)
            kbs = [k_ref[pl.ds(qk0, tq), c * LANES:(c + 1) * LANES] for c in range(nblk)]
        new = []
        for h in range(heads):
            m, l, alpha, acc = carry[h]
            vt = vt_ref[h * HEAD:(h + 1) * HEAD, pl.ds(pv0, tq)]
            acc = alpha * acc + _dot(vt, p_scr[h])
            if softmax:
                s = s_scr[h]
                m_new = jnp.maximum(m, jnp.max(s, axis=0, keepdims=True))
                p = jnp.exp2(s - m_new)
                alpha = jnp.exp2(m - m_new)
                l = alpha * l + jnp.sum(p, axis=0, keepdims=True)
                m = m_new
                p_scr[h] = _bf(p)
            if scores:
                ck = ck_ref[h, pl.ds(qk0, tq), :]
                s_next = _dot(kbs[h // PAIR], qts[h]) + cqs[h] - jnp.concatenate([ck] * (tq // LANES), axis=1)
                s_scr[h] = jnp.where(key_minus_query <= 0, s_next, -jnp.inf) if diagonal else s_next
            new.append((m, l, alpha, acc))
        return tuple(new)

    init = tuple((jnp.full((1, tq), -1e30, F32), jnp.zeros((1, tq), F32), jnp.ones((1, tq), F32),
                  jnp.zeros((HEAD, tq), F32)) for _ in range(heads))
    carry = lax.fori_loop(0, qi, step, init)
    carry = step(qi, carry, diagonal=True)
    carry = step(qi + 1, carry, scores=False)
    carry = step(qi + 2, carry, scores=False, softmax=False)
    ot = jnp.concatenate([acc / l for (_, l, _, acc) in carry], axis=0)
    o_ref[...] = jnp.concatenate(
        [jnp.concatenate([ot[c * LANES:(c + 1) * LANES, i * LANES:(i + 1) * LANES].T for c in range(nblk)], axis=1)
         for i in range(tq // LANES)], axis=0).astype(o_ref.dtype)


def _fox_attn(qkv, cc, cr, B, S, tq):
    n = B * S
    nq = S // tq
    w = FOX_WIDTH
    nw = FOX_DIM // w
    heads = w // HEAD
    return pl.pallas_call(
        functools.partial(_fox_attn_kernel, tq=tq, seq=S),
        grid=(B, nw, nq),
        in_specs=[pl.BlockSpec((tq, w), lambda b, p, i: (b * nq + i, p)),
                  pl.BlockSpec((S, w), lambda b, p, i: (b, nw + p)),
                  pl.BlockSpec((S, w), lambda b, p, i: (b, 2 * nw + p)),
                  pl.BlockSpec((S, LANES), lambda b, p, i: (b, 0)),
                  pl.BlockSpec((SMALL_ROWS, S), lambda b, p, i: (0, b))],
        out_specs=pl.BlockSpec((tq, w), lambda b, p, i: (b * nq + i, p)),
        out_shape=jax.ShapeDtypeStruct((n, FOX_DIM), BF16),
        scratch_shapes=[pltpu.VMEM((w, S), BF16), pltpu.VMEM((heads, S, LANES), F32),
                        pltpu.VMEM((heads, tq, tq), F32), pltpu.VMEM((heads, tq, tq), BF16)],
        compiler_params=_params("parallel", "parallel", "arbitrary"),
        name="fox_attn",
    )(qkv, qkv, qkv, cc, cr)


def _mamba_kernel(z_ref, xbc_ref, ps_ref, pst_ref, cw_ref, cb_ref, dtbl_ref, dtbc_ref, al_ref, ac_ref,
                  dl_ref, ng_ref, ex_ref, o_ref, halo_ref, st_ref):
    L = SSD_CHUNK

    @pl.when(pl.program_id(1) == 0)
    def _():
        halo_ref[...] = jnp.zeros_like(halo_ref)
        st_ref[...] = jnp.zeros_like(st_ref)

    xbc = xbc_ref[...].astype(F32)
    halo = halo_ref[...]
    row8 = _iota2((SUBLANES, 1), 0)
    acc = cb_ref[...] + cw_ref[CONV_WIDTH - 1:CONV_WIDTH, :] * xbc
    for sft in range(1, CONV_WIDTH):
        rolled = pltpu.roll(xbc, sft, axis=0)
        top = jnp.where(row8 < sft, pltpu.roll(halo, sft, axis=0), rolled[:SUBLANES, :])
        shifted = jnp.concatenate([top, rolled[SUBLANES:, :]], axis=0)
        acc = acc + cw_ref[CONV_WIDTH - 1 - sft:CONV_WIDTH - sft, :] * shifted
    halo_ref[...] = xbc[L - SUBLANES:, :]
    act = acc * _sigmoid(acc)
    xs = act[:, :SSM_DIM]
    Bm = act[:, SSM_DIM:SSM_DIM + SSM_GROUPS * SSM_STATE]
    Cm = act[:, SSM_DIM + SSM_GROUPS * SSM_STATE:]

    ri = _iota2((L, L), 0)
    ci = _iota2((L, L), 1)
    lower = ri >= ci
    tril = _bf(jnp.where(lower, 1.0, 0.0))
    triu = _bf(jnp.where(ri <= ci, 1.0, 0.0))
    lane = _iota2((1, LANES), 1)
    in0 = lane < HEAD

    dt_full = _dot_sel_lhs(_softplus(ps_ref[...] + dtbl_ref[...]), ex_ref[...])
    cs = _dot_sel_rhs(tril, dt_full * al_ref[...])
    cs_end = cs[L - 1:L, :]
    ecs = jnp.exp(cs)
    X = xs * dt_full
    Xe = X * jnp.exp(cs_end - cs)
    csr = _dot_sel_lhs(_softplus(pst_ref[...] + dtbc_ref[...]) * ac_ref[...], triu)

    heads_per_group = SSM_HEADS // SSM_GROUPS
    pairs_per_group = heads_per_group // PAIR
    ys = []
    for g in range(SSM_GROUPS):
        Bg = Bm[:, g * SSM_STATE:(g + 1) * SSM_STATE]
        Cg = _bf(Cm[:, g * SSM_STATE:(g + 1) * SSM_STATE])
        CB = _dot_nt(Cg, _bf(Bg))
        BgT = _bf(Bg.T)
        for pp in range(pairs_per_group):
            p = g * pairs_per_group + pp
            cols = slice(p * LANES, (p + 1) * LANES)
            Xp = X[:, cols]
            ydiag = None
            for hh in range(PAIR):
                h = PAIR * p + hh
                decay = jnp.exp(cs[:, h * HEAD:h * HEAD + 1] - csr[DT_OFF + h:DT_OFF + h + 1, :])
                Mh = _bf(CB * jnp.where(lower, decay, 0.0))
                Xh = _bf(jnp.where(in0, Xp, 0.0) if hh == 0 else jnp.where(in0, 0.0, Xp))
                t = _dot(Mh, Xh)
                ydiag = t if ydiag is None else ydiag + t
            st = st_ref[p]
            yoff = _dot(Cg, _bf(st)) * ecs[:, cols]
            st_ref[p] = st * ecs[L - 1:L, cols] + _dot(BgT, _bf(Xe[:, cols]))
            ys.append(ydiag + yoff + dl_ref[:, cols] * xs[:, cols])
    z = z_ref[...].astype(F32)
    y = jnp.concatenate(ys, axis=1) * (z * _sigmoid(z))
    gw = SSM_DIM // SSM_GROUPS
    outs = []
    for g in range(SSM_GROUPS):
        yg = y[:, g * gw:(g + 1) * gw]
        ms = jnp.mean(yg * yg, axis=-1, keepdims=True)
        outs.append(yg * lax.rsqrt(ms + NORM_EPS))
    o_ref[...] = _bf(jnp.concatenate(outs, axis=1) * ng_ref[...])


def _mamba(z, xbc, ps, pst, cw, cb, dtbl, dtbc, al, ac, dl, ng, ex, B, S):
    n = B * S
    L = SSD_CHUNK
    nc = S // L
    row = lambda w: pl.BlockSpec((L, w), lambda b, c: (b * nc + c, 0))
    full = lambda a: pl.BlockSpec(a.shape, lambda b, c: (0,) * a.ndim)
    return pl.pallas_call(
        _mamba_kernel,
        grid=(B, nc),
        in_specs=[row(SSM_DIM), row(SSM_CONV_DIM), row(LANES),
                  pl.BlockSpec((SMALL_ROWS, L), lambda b, c: (0, b * nc + c))]
                 + [full(a) for a in (cw, cb, dtbl, dtbc, al, ac, dl, ng, ex)],
        out_specs=row(SSM_DIM),
        out_shape=jax.ShapeDtypeStruct((n, SSM_DIM), BF16),
        scratch_shapes=[pltpu.VMEM((SUBLANES, SSM_CONV_DIM), F32),
                        pltpu.VMEM((SSM_HEADS // PAIR, SSM_STATE, LANES), F32)],
        compiler_params=_params("parallel", "arbitrary"),
        name="mamba_ssd",
    )(z, xbc, ps, pst, cw, cb, dtbl, dtbc, al, ac, dl, ng, ex)


def _layer_norm(h, g, b):
    mu = jnp.mean(h, axis=-1, keepdims=True)
    hc = h - mu
    var = jnp.mean(hc * hc, axis=-1, keepdims=True)
    return hc * lax.rsqrt(var + LN_EPS) * g + b


def _merge_kernel(or_ref, of_ref, om_ref, pg_ref, x_ref, wr_ref, wf_ref, wm_ref, wo_ref, gb_ref,
                  lg_ref, lb_ref, rwh_ref, rwl_ref, x1_ref, lgt_ref, *, alpha):
    D = D_MODEL
    gate = lambda i: _sigmoid(pg_ref[:, i * D:(i + 1) * D].astype(F32) + gb_ref[:, i * D:(i + 1) * D])
    merged = (gate(0) * _dot(or_ref[...], wr_ref[...])
              + gate(1) * _dot(of_ref[...], wf_ref[...])
              + gate(2) * _dot(om_ref[...], wm_ref[...]))
    mix = _dot(_bf(merged), wo_ref[...])
    x1 = _layer_norm(alpha * x_ref[...] + mix, lg_ref[...], lb_ref[...])
    x1_ref[...] = x1
    xh = _bf(x1)
    xl = _bf(x1 - xh.astype(F32))
    lgt_ref[...] = (_dot_nt(rwh_ref[...], xh) + _dot_nt(rwh_ref[...], xl) + _dot_nt(rwl_ref[...], xh))


def _merge(o_r, o_f, o_m, pg, x, wr, wf, wm, wo, gb, lg, lb, rwh, rwl, tm, alpha):
    n = x.shape[0]
    row = lambda w: pl.BlockSpec((tm, w), lambda i: (i, 0))
    full = lambda a: pl.BlockSpec(a.shape, lambda i: (0,) * a.ndim)
    return pl.pallas_call(
        functools.partial(_merge_kernel, alpha=alpha),
        grid=(n // tm,),
        in_specs=[row(RWKV_DIM), row(FOX_DIM), row(SSM_DIM), row(3 * D_MODEL), row(D_MODEL)]
                 + [full(a) for a in (wr, wf, wm, wo, gb, lg, lb, rwh, rwl)],
        out_specs=[row(D_MODEL), pl.BlockSpec((N_EXPERTS, tm), lambda i: (0, i))],
        out_shape=[jax.ShapeDtypeStruct((n, D_MODEL), F32),
                   jax.ShapeDtypeStruct((N_EXPERTS, n), F32)],
        compiler_params=_params("parallel"),
        name="merge_ln1_router",
    )(o_r, o_f, o_m, pg, x, wr, wf, wm, wo, gb, lg, lb, rwh, rwl)


def _rows(*vals):
    t = vals[0].shape[1]
    sub = _iota2((SUBLANES, t), 0)
    out = jnp.zeros((SUBLANES, t), vals[0].dtype)
    for i, v in enumerate(vals):
        out = jnp.where(sub == i, v, out)
    return out


def _route_select_kernel(lgt_ref, bias_ref, e_ref, g_ref, rank_ref, cnt_ref, carry_ref):
    @pl.when(pl.program_id(0) == 0)
    def _():
        carry_ref[...] = jnp.zeros_like(carry_ref)

    t = lgt_ref.shape[1]
    aff = _sigmoid(lgt_ref[...])
    sel = aff + bias_ref[...]
    gsz = EXPERTS_PER_GROUP
    sub = _iota2((gsz, t), 0).astype(F32)
    best = e1 = e2 = None
    for g in range(N_EXPERT_GROUPS):
        s = sel[g * gsz:(g + 1) * gsz, :]
        m1 = jnp.max(s, axis=0, keepdims=True)
        i1 = jnp.min(jnp.where(s == m1, sub, float(gsz)), axis=0, keepdims=True)
        s2 = jnp.where(sub == i1, -jnp.inf, s)
        m2 = jnp.max(s2, axis=0, keepdims=True)
        i2 = jnp.min(jnp.where(s2 == m2, sub, float(gsz)), axis=0, keepdims=True)
        score = m1 + m2
        if g == 0:
            best, e1, e2 = score, i1, i2
        else:
            better = score > best
            best = jnp.where(better, score, best)
            e1 = jnp.where(better, i1 + float(g * gsz), e1)
            e2 = jnp.where(better, i2 + float(g * gsz), e2)
    row = _iota2((N_EXPERTS, t), 0).astype(F32)
    oh1 = jnp.where(row == e1, 1.0, 0.0)
    oh2 = jnp.where(row == e2, 1.0, 0.0)
    w1 = jnp.sum(oh1 * aff, axis=0, keepdims=True)
    w2 = jnp.sum(oh2 * aff, axis=0, keepdims=True)
    den = w1 + w2
    cnt = oh1 + oh2
    before = _bf(jnp.where(_iota2((t, t), 0) < _iota2((t, t), 1), 1.0, 0.0))
    prefix = _dot(_bf(cnt), before) + carry_ref[...]
    r1 = jnp.sum(oh1 * prefix, axis=0, keepdims=True)
    r2 = jnp.sum(oh2 * prefix, axis=0, keepdims=True)
    carry_ref[...] = carry_ref[...] + jnp.sum(cnt, axis=1, keepdims=True)
    e_ref[...] = _rows(e1, e2)
    g_ref[...] = _rows(w1 / den, w2 / den)
    rank_ref[...] = _rows(r1, r2)
    cnt_ref[...] = jnp.broadcast_to(carry_ref[...], cnt_ref.shape)


def _route_select(lgt, bias_col, tile):
    n = lgt.shape[1]
    col = lambda r: pl.BlockSpec((r, tile), lambda i: (0, i))
    out = jax.ShapeDtypeStruct((SUBLANES, n), F32)
    return pl.pallas_call(
        _route_select_kernel,
        grid=(n // tile,),
        in_specs=[col(N_EXPERTS), pl.BlockSpec((N_EXPERTS, 1), lambda i: (0, 0))],
        out_specs=[col(SUBLANES)] * 3 + [pl.BlockSpec((N_EXPERTS, LANES), lambda i: (0, 0))],
        out_shape=[out] * 3 + [jax.ShapeDtypeStruct((N_EXPERTS, LANES), F32)],
        scratch_shapes=[pltpu.VMEM((N_EXPERTS, 1), F32)],
        compiler_params=_params("arbitrary"),
        name="route_select",
    )(lgt, bias_col)


def _route_place_kernel(e_ref, rank_ref, cnt_ref, dest_ref, be_ref, nu_ref):
    t = e_ref.shape[1]
    blk = float(MOE_BLOCK)
    counts = cnt_ref[...]
    padded = jnp.floor((counts + (blk - 1.0)) / blk) * blk
    ne = N_EXPERTS
    lower = _bf(jnp.where(_iota2((ne, ne), 0) > _iota2((ne, ne), 1), 1.0, 0.0))
    pad_start = _dot_sel_rhs(lower, padded)
    pad_end = pad_start + padded
    row = _iota2((ne, t), 0).astype(F32)
    ps_t = jnp.concatenate([pad_start] * (t // LANES), axis=1)
    e = e_ref[...]
    rank = rank_ref[...]
    dests = []
    for k in range(TOP_K):
        oh = jnp.where(row == e[k:k + 1, :], 1.0, 0.0)
        dests.append(jnp.sum(oh * ps_t, axis=0, keepdims=True) + rank[k:k + 1, :])
    dest_ref[...] = _rows(*dests).astype(jnp.int32)
    nbp = be_ref.shape[1]
    blk_start = _iota2((ne, nbp), 1).astype(F32) * blk
    pe_t = jnp.concatenate([pad_end] * (nbp // LANES), axis=1)
    be = jnp.sum(jnp.where(pe_t <= blk_start, 1.0, 0.0), axis=0, keepdims=True)
    be_ref[...] = jnp.broadcast_to(jnp.minimum(be, float(ne - 1)), be_ref.shape).astype(jnp.int32)
    nu_ref[...] = jnp.broadcast_to(pad_end[ne - 1:ne, :] / blk, nu_ref.shape).astype(jnp.int32)


def _route_place(e, rank, cnt, tile, n_blocks):
    n = e.shape[1]
    nbp = -(-n_blocks // LANES) * LANES
    col = pl.BlockSpec((SUBLANES, tile), lambda i: (0, i))
    return pl.pallas_call(
        _route_place_kernel,
        grid=(n // tile,),
        in_specs=[col, col, pl.BlockSpec((N_EXPERTS, LANES), lambda i: (0, 0))],
        out_specs=[col, pl.BlockSpec((SUBLANES, nbp), lambda i: (0, 0)),
                   pl.BlockSpec((SUBLANES, LANES), lambda i: (0, 0))],
        out_shape=[jax.ShapeDtypeStruct((SUBLANES, n), jnp.int32),
                   jax.ShapeDtypeStruct((SUBLANES, nbp), jnp.int32),
                   jax.ShapeDtypeStruct((SUBLANES, LANES), jnp.int32)],
        compiler_params=_params("arbitrary"),
        name="route_place",
    )(e, rank, cnt)


def _moe_kernel(be_ref, nu_ref, xb_ref, wg_ref, wu_ref, wd_ref, yb_ref, wgb, wub, wdb):
    i = pl.program_id(0)
    new_expert = jnp.logical_or(i == 0, be_ref[i] != be_ref[jnp.maximum(i - 1, 0)])

    @pl.when(new_expert)
    def _():
        wgb[...] = _bf(wg_ref[0])
        wub[...] = _bf(wu_ref[0])
        wdb[...] = _bf(wd_ref[0])

    @pl.when(i < nu_ref[0])
    def _():
        x = _bf(xb_ref[...])
        gt = _dot(x, wgb[...])
        h = gt * _sigmoid(gt) * _dot(x, wub[...])
        yb_ref[...] = _dot(_bf(h), wdb[...])

    @pl.when(i >= nu_ref[0])
    def _():
        yb_ref[...] = jnp.zeros_like(yb_ref)


def _moe(block_expert, n_used, xb, wg, wu, wd):
    rows, d = xb.shape
    blk = MOE_BLOCK
    nb = rows // blk
    grid_spec = pltpu.PrefetchScalarGridSpec(
        num_scalar_prefetch=2,
        grid=(nb,),
        in_specs=[pl.BlockSpec((blk, d), lambda i, be, nu: (i, 0)),
                  pl.BlockSpec((1, d, D_EXPERT), lambda i, be, nu: (be[i], 0, 0)),
                  pl.BlockSpec((1, d, D_EXPERT), lambda i, be, nu: (be[i], 0, 0)),
                  pl.BlockSpec((1, D_EXPERT, d), lambda i, be, nu: (be[i], 0, 0))],
        out_specs=pl.BlockSpec((blk, d), lambda i, be, nu: (i, 0)),
        scratch_shapes=[pltpu.VMEM((d, D_EXPERT), BF16), pltpu.VMEM((d, D_EXPERT), BF16),
                        pltpu.VMEM((D_EXPERT, d), BF16)])
    return pl.pallas_call(
        _moe_kernel,
        grid_spec=grid_spec,
        out_shape=jax.ShapeDtypeStruct((rows, d), F32),
        compiler_params=_params("arbitrary"),
        name="moe_experts",
    )(block_expert, n_used, xb, wg, wu, wd)


def _dispatch_kernel(dest_ref, x_ref, buf_in_hbm, buf_hbm, sem, *, tile, n):
    del buf_in_hbm
    base = pl.program_id(0) * tile

    def row_copy(j, k):
        return pltpu.make_async_copy(x_ref.at[pl.ds(j, 1)],
                                     buf_hbm.at[pl.ds(dest_ref[k * n + base + j], 1)], sem)

    def start(j, carry):
        for k in range(TOP_K):
            row_copy(j, k).start(priority=k)
        return carry

    def wait(j, carry):
        for k in range(TOP_K):
            row_copy(j, k).wait()
        return carry

    lax.fori_loop(0, tile, start, 0, unroll=DMA_UNROLL)
    lax.fori_loop(0, tile, wait, 0, unroll=DMA_UNROLL)


def _dispatch(dest_flat, x, rows, tile):
    n, d = x.shape
    any_spec = pl.BlockSpec(memory_space=pl.ANY)
    grid_spec = pltpu.PrefetchScalarGridSpec(
        num_scalar_prefetch=1,
        grid=(n // tile,),
        in_specs=[pl.BlockSpec((tile, d), lambda i, dest: (i, 0)), any_spec],
        out_specs=any_spec,
        scratch_shapes=[pltpu.SemaphoreType.DMA(())])
    return pl.pallas_call(
        functools.partial(_dispatch_kernel, tile=tile, n=n),
        grid_spec=grid_spec,
        out_shape=jax.ShapeDtypeStruct((rows, d), x.dtype),
        input_output_aliases={2: 0},
        compiler_params=_params("arbitrary"),
        name="moe_dispatch",
    )(dest_flat, x, jnp.zeros((rows, d), x.dtype))


def _combine_ln2_kernel(dest_ref, x_ref, gate_ref, yb_hbm, g_ref, b_ref, o_ref, ob_ref, buf, sems,
                        *, tile, n, alpha):
    i = pl.program_id(0)
    steps = pl.num_programs(0)

    def row_copy(step, slot, j, k):
        t = step * tile + j
        return pltpu.make_async_copy(yb_hbm.at[pl.ds(dest_ref[k * n + t], 1)],
                                     buf.at[slot, k, pl.ds(j, 1)], sems.at[slot])

    def start_tile(step, slot):
        def body(j, carry):
            for k in range(TOP_K):
                row_copy(step, slot, j, k).start(priority=k)
            return carry
        lax.fori_loop(0, tile, body, 0, unroll=DMA_UNROLL)

    def wait_tile(step, slot):
        def body(j, carry):
            for k in range(TOP_K):
                row_copy(step, slot, j, k).wait()
            return carry
        lax.fori_loop(0, tile, body, 0, unroll=DMA_UNROLL)

    slot = lax.rem(i, 2)

    @pl.when(i == 0)
    def _():
        start_tile(0, 0)

    for s in range(2):
        @pl.when(slot == s)
        def _():
            @pl.when(i + 1 < steps)
            def _():
                start_tile(i + 1, 1 - s)
            wait_tile(i, s)

    gate = gate_ref[...]
    ffn = gate[:, 0:1] * buf[slot, 0] + gate[:, 1:2] * buf[slot, 1]
    y = _layer_norm(alpha * x_ref[...] + ffn, g_ref[...], b_ref[...])
    o_ref[...] = y
    ob_ref[...] = _bf(y)


def _combine_ln2(dest_flat, x, gate, yb, g, b, tile, alpha):
    n, d = x.shape
    row = pl.BlockSpec((tile, d), lambda i, dest: (i, 0))
    par = pl.BlockSpec((1, d), lambda i, dest: (0, 0))
    grid_spec = pltpu.PrefetchScalarGridSpec(
        num_scalar_prefetch=1,
        grid=(n // tile,),
        in_specs=[row, pl.BlockSpec((tile, TOP_K), lambda i, dest: (i, 0)),
                  pl.BlockSpec(memory_space=pl.ANY), par, par],
        out_specs=[row, row],
        scratch_shapes=[pltpu.VMEM((2, TOP_K, tile, d), F32), pltpu.SemaphoreType.DMA((2,))])
    return pl.pallas_call(
        functools.partial(_combine_ln2_kernel, tile=tile, n=n, alpha=alpha),
        grid_spec=grid_spec,
        out_shape=[jax.ShapeDtypeStruct((n, d), F32), jax.ShapeDtypeStruct((n, d), BF16)],
        compiler_params=_params("arbitrary"),
        name="moe_combine_ln2",
    )(dest_flat, x, gate, yb, g, b)


def _route(lgt, bias_col, tile):
    n = lgt.shape[1]
    n_blocks = -(-(n * TOP_K) // MOE_BLOCK) + N_EXPERTS
    e, g, rank, cnt = _route_select(lgt, bias_col, tile)
    dest, be, nu = _route_place(e, rank, cnt, tile, n_blocks)
    return g[:TOP_K].T, dest[:TOP_K].reshape(-1), be[0, :n_blocks], nu[0, :1], n_blocks


def _proj_tn(width):
    blocks = width // LANES
    best = max(d for d in range(1, blocks + 1) if blocks % d == 0 and d * LANES <= 1024)
    return best * LANES


def _tiles(B, S):
    n = B * S
    return dict(
        tm_proj=min(1024, n),
        ts_prep=min(512, S),
        rwkv_chunks=min(8, S // RWKV_CHUNK),
        tq=min(256, S),
        tm_merge=min(512, n),
        t_route=min(512, n),
        t_dispatch=min(512, n),
        t_combine=min(256, n),
    )


def _pad_rows(a, before, total):
    return jnp.zeros((total, a.shape[1]), a.dtype).at[before:before + a.shape[0]].set(a)


def _block_diag_ones(dim):
    i = jnp.arange(dim) // HEAD
    return (i[:, None] == i[None, :]).astype(BF16)


def kernel(x, w_in, rwkv_mu, rwkv_w0, rwkv_w_up, rwkv_a0, rwkv_a_up, rwkv_g_up, rwkv_k_k, rwkv_k_a,
           rwkv_r_k, rwkv_gn_g, rwkv_gn_b, fox_f_bias, ssm_conv_w, ssm_conv_b, ssm_dt_bias, ssm_a_log,
           ssm_d, ssm_norm_g, gate_bias, w_br_rwkv, w_br_fox, w_br_ssm, w_out, ln1_g, ln1_b, router_w,
           router_bias, exp_w_gate, exp_w_up, exp_w_down, ln2_g, ln2_b):
    B, S, D = x.shape
    depth = w_in.shape[0]
    n = B * S
    t = _tiles(B, S)
    alpha = (2 * depth) ** 0.25

    row1 = lambda a: a.reshape(1, -1).astype(F32)
    bd512 = _block_diag_ones(RWKV_DIM)
    expand = (jnp.arange(LANES)[:, None] == DT_OFF + jnp.arange(SSM_DIM)[None, :] // HEAD).astype(BF16)
    rw = router_w.astype(F32).T
    rwh = rw.astype(BF16)
    rwl = (rw - rwh.astype(F32)).astype(BF16)
    rb_col = router_bias.astype(F32).reshape(N_EXPERTS, 1)

    xf = x.reshape(n, D).astype(F32)
    xb16 = xf.astype(BF16)
    for l in range(depth):
        w = w_in[l].astype(BF16)
        w_small = jnp.concatenate([w[:, OFF_SSM - FOX_HEADS:OFF_SSM], w[:, OFF_GATE - SSM_HEADS:OFF_GATE]], axis=1)
        w_small_l = jnp.zeros((D, LANES), BF16).at[:, :w_small.shape[1]].set(w_small)
        w_small_t = jnp.zeros((SMALL_ROWS, D), BF16).at[:w_small.shape[1]].set(w_small.T)
        mm = lambda lo, hi: _matmul(xb16, w[:, lo:hi], t["tm_proj"], _proj_tn(hi - lo), BF16)
        p_rwkv = mm(0, RWKV_COLS)
        p_fox = mm(OFF_FOX, OFF_FOX + 3 * FOX_DIM)
        p_z = mm(OFF_SSM, OFF_SSM + SSM_DIM)
        p_xbc = mm(OFF_SSM + SSM_DIM, OFF_SSM + SSM_DIM + SSM_CONV_DIM)
        p_gate = mm(OFF_GATE, OFF_GATE + 3 * D)
        ps, pst = _small_proj(xb16, w_small_l, w_small_t, t["tm_proj"])

        wup = _pad_rows(rwkv_w_up[l], 0, DECAY_LORA + ICLR_LORA).astype(BF16)
        aup = _pad_rows(rwkv_a_up[l], DECAY_LORA, DECAY_LORA + ICLR_LORA).astype(BF16)
        r, lw, k, v, kn, b, g = _rwkv_prep(
            p_rwkv, B, S, t["ts_prep"], row1(rwkv_mu[l]), row1(rwkv_w0[l]), wup, row1(rwkv_a0[l]), aup,
            rwkv_g_up[l].astype(BF16), row1(rwkv_k_k[l]), row1(rwkv_k_a[l]), bd512)
        o_r = _rwkv_chunk(r, lw, k, v, kn, b, g, row1(rwkv_gn_g[l]), row1(rwkv_gn_b[l]),
                          row1(rwkv_r_k[l]), B, S, t["rwkv_chunks"])

        fb_lane = jnp.zeros((1, LANES), F32).at[0, :FOX_HEADS].set(fox_f_bias[l].astype(F32))
        fb_col = jnp.zeros((SMALL_ROWS, 1), F32).at[:FOX_HEADS, 0].set(fox_f_bias[l].astype(F32))
        cc, cr = _fox_c(ps, pst, fb_lane, fb_col, B, S)
        o_f = _fox_attn(p_fox, cc, cr, B, S, t["tq"])

        dtb = ssm_dt_bias[l].astype(F32)
        a_neg = -jnp.exp(ssm_a_log[l].astype(F32))
        dtbl = jnp.zeros((1, LANES), F32).at[0, DT_OFF:DT_OFF + SSM_HEADS].set(dtb)
        dtbc = jnp.zeros((SMALL_ROWS, 1), F32).at[DT_OFF:DT_OFF + SSM_HEADS, 0].set(dtb)
        ac = jnp.zeros((SMALL_ROWS, 1), F32).at[DT_OFF:DT_OFF + SSM_HEADS, 0].set(a_neg)
        al = jnp.repeat(a_neg, HEAD).reshape(1, SSM_DIM)
        dl = jnp.repeat(ssm_d[l].astype(F32), HEAD).reshape(1, SSM_DIM)
        o_m = _mamba(p_z, p_xbc, ps, pst, ssm_conv_w[l].astype(F32), row1(ssm_conv_b[l]), dtbl, dtbc,
                     al, ac, dl, row1(ssm_norm_g[l]), expand, B, S)

        x1, logits = _merge(
            o_r, o_f, o_m, p_gate, xf, w_br_rwkv[l].astype(BF16), w_br_fox[l].astype(BF16),
            w_br_ssm[l].astype(BF16), w_out[l].astype(BF16), row1(gate_bias[l]), row1(ln1_g[l]),
            row1(ln1_b[l]), rwh, rwl, t["tm_merge"], alpha)

        gate, dest, block_expert, n_used, n_blocks = _route(logits, rb_col, t["t_route"])
        xb = _dispatch(dest, x1, n_blocks * MOE_BLOCK, t["t_dispatch"])
        yb = _moe(block_expert, n_used, xb, exp_w_gate[l], exp_w_up[l], exp_w_down[l])
        xf, xb16 = _combine_ln2(dest, x1, gate, yb, row1(ln2_g[l]), row1(ln2_b[l]), t["t_combine"], alpha)
    return xf.reshape(B, S, D).astype(x.dtype)
```

```python
import functools
import math

import jax
import jax.numpy as jnp
from jax import lax
from jax.experimental import pallas as pl
from jax.experimental.pallas import tpu as pltpu

F32 = jnp.float32
BF16 = jnp.bfloat16

LANES = 128
SUBLANES = 8
VMEM_LIMIT = 48 * 1024 * 1024

D_MODEL = 1024
HEAD = 64
PAIR = LANES // HEAD
RWKV_DIM = 512
DECAY_LORA = 64
ICLR_LORA = 64
GATE_LORA = 128
RWKV_COLS = 3 * RWKV_DIM + DECAY_LORA + ICLR_LORA + GATE_LORA
RWKV_GN_EPS = 64e-5
FOX_DIM = 512
FOX_HEADS = 8
SSM_DIM = 1024
SSM_HEADS = 16
SSM_GROUPS = 2
SSM_STATE = 128
CONV_WIDTH = 4
SSM_CONV_DIM = SSM_DIM + 2 * SSM_GROUPS * SSM_STATE
NORM_EPS = 1e-5
N_EXPERTS = 32
N_EXPERT_GROUPS = 4
EXPERTS_PER_GROUP = N_EXPERTS // N_EXPERT_GROUPS
TOP_K = 2
D_EXPERT = 512
LN_EPS = 1e-5
LOG2E = math.log2(math.e)
OFF_FOX = RWKV_COLS
OFF_SSM = OFF_FOX + 3 * FOX_DIM + FOX_HEADS
OFF_GATE = OFF_SSM + SSM_DIM + SSM_CONV_DIM + SSM_HEADS

SMALL_ROWS = 32
DT_OFF = 8
FOX_WIDTH = 512
RWKV_CHUNK = 64
RWKV_GROUP = 8
SSD_CHUNK = 128
MOE_BLOCK = 256
DMA_UNROLL = 16


def _bf(x):
    return x.astype(BF16)


def _dot(a, b):
    return jnp.dot(a, b, preferred_element_type=F32)


def _dot_nt(a, b):
    return lax.dot_general(a, b, (((1,), (1,)), ((), ())), preferred_element_type=F32)


def _dot_tn(a, b):
    return lax.dot_general(a, b, (((0,), (0,)), ((), ())), preferred_element_type=F32)


def _split(x, parts):
    out = []
    rem = x
    for _ in range(parts):
        hi = _bf(rem)
        out.append(hi)
        rem = rem - hi.astype(F32)
    return out


def _dot_sel_rhs(m01, x, parts=3):
    acc = None
    for part in _split(x, parts):
        t = _dot(m01, part)
        acc = t if acc is None else acc + t
    return acc


def _dot_sel_lhs(x, m01, parts=3):
    acc = None
    for part in _split(x, parts):
        t = _dot(part, m01)
        acc = t if acc is None else acc + t
    return acc


def _sigmoid(x):
    return 1.0 / (1.0 + jnp.exp(-x))


def _softplus(x):
    return jnp.maximum(x, 0.0) + jnp.log(1.0 + jnp.exp(-jnp.abs(x)))


def _log_sigmoid(x):
    return jnp.minimum(x, 0.0) - jnp.log(1.0 + jnp.exp(-jnp.abs(x)))


def _iota2(shape, dim):
    return lax.broadcasted_iota(jnp.int32, shape, dim)


def _params(*sem):
    return pltpu.CompilerParams(dimension_semantics=sem, vmem_limit_bytes=VMEM_LIMIT)


def _mm_kernel(x_ref, w_ref, o_ref):
    o_ref[...] = _dot(x_ref[...], w_ref[...]).astype(o_ref.dtype)


def _matmul(x, w, tm, tn, out_dtype=F32):
    n, k = x.shape
    nc = w.shape[1]
    return pl.pallas_call(
        _mm_kernel,
        grid=(n // tm, nc // tn),
        in_specs=[pl.BlockSpec((tm, k), lambda i, j: (i, 0)),
                  pl.BlockSpec((k, tn), lambda i, j: (0, j))],
        out_specs=pl.BlockSpec((tm, tn), lambda i, j: (i, j)),
        out_shape=jax.ShapeDtypeStruct((n, nc), out_dtype),
        compiler_params=_params("parallel", "parallel"),
        name="in_proj",
    )(x, w)


def _small_proj_kernel(x_ref, w_ref, wt_ref, o_ref, ot_ref):
    x = x_ref[...]
    o_ref[...] = _dot(x, w_ref[...])
    ot_ref[...] = _dot_nt(wt_ref[...], x)


def _small_proj(x, w, wt, tm):
    n, k = x.shape
    r = wt.shape[0]
    return pl.pallas_call(
        _small_proj_kernel,
        grid=(n // tm,),
        in_specs=[pl.BlockSpec((tm, k), lambda i: (i, 0)),
                  pl.BlockSpec((k, LANES), lambda i: (0, 0)),
                  pl.BlockSpec((r, k), lambda i: (0, 0))],
        out_specs=[pl.BlockSpec((tm, LANES), lambda i: (i, 0)),
                   pl.BlockSpec((r, tm), lambda i: (0, i))],
        out_shape=[jax.ShapeDtypeStruct((n, LANES), F32), jax.ShapeDtypeStruct((r, n), F32)],
        compiler_params=_params("parallel"),
        name="in_proj_small",
    )(x, w, wt)


def _rwkv_prep_kernel(p_ref, mu_ref, w0_ref, wup_ref, a0_ref, aup_ref, gup_ref, kk_ref, ka_ref,
                      bd_ref, r_o, lw_o, k_o, v_o, kn_o, b_o, g_o, carry_ref):
    @pl.when(pl.program_id(1) == 0)
    def _():
        carry_ref[...] = jnp.zeros_like(carry_ref)

    p = p_ref[...].astype(F32)
    ts = p.shape[0]
    row = _iota2((ts, 1), 0)
    prev = jnp.where(row == 0, carry_ref[...], pltpu.roll(p, 1, axis=0))
    carry_ref[...] = p[ts - 1:ts, :]
    p = p + mu_ref[...] * (prev - p)

    R = RWKV_DIM
    r = p[:, 0:R]
    k = p[:, R:2 * R]
    v = p[:, 2 * R:3 * R]
    lora = p[:, 3 * R:3 * R + DECAY_LORA + ICLR_LORA]
    gd = p[:, 3 * R + DECAY_LORA + ICLR_LORA:]
    w_pre = w0_ref[...] + _dot(_bf(jnp.tanh(lora)), wup_ref[...])
    lw = -jnp.exp(_log_sigmoid(w_pre) - 0.5)
    a = _sigmoid(a0_ref[...] + _dot(_bf(lora), aup_ref[...]))
    g = _dot(_bf(_sigmoid(gd)), gup_ref[...])
    kk = k * kk_ref[...]
    ss = _dot_sel_lhs(kk * kk, bd_ref[...], parts=2)
    kn = kk / jnp.maximum(jnp.sqrt(ss), 1e-12)
    r_o[...] = _bf(r)
    lw_o[...] = lw
    k_o[...] = _bf(k * (1.0 + (a - 1.0) * ka_ref[...]))
    v_o[...] = _bf(v)
    kn_o[...] = _bf(kn)
    b_o[...] = _bf(kn * a)
    g_o[...] = _bf(g)


def _rwkv_prep(p, B, S, ts, mu, w0, wup, a0, aup, gup, k_k, k_a, bd):
    n = B * S
    ns = S // ts
    row_spec = lambda w: pl.BlockSpec((ts, w), lambda b, s: (b * ns + s, 0))
    full = lambda a: pl.BlockSpec(a.shape, lambda b, s: (0,) * a.ndim)
    out = lambda dt: jax.ShapeDtypeStruct((n, RWKV_DIM), dt)
    return pl.pallas_call(
        _rwkv_prep_kernel,
        grid=(B, ns),
        in_specs=[row_spec(RWKV_COLS)] + [full(a) for a in (mu, w0, wup, a0, aup, gup, k_k, k_a, bd)],
        out_specs=[row_spec(RWKV_DIM)] * 7,
        out_shape=[out(BF16), out(F32)] + [out(BF16)] * 5,
        scratch_shapes=[pltpu.VMEM((1, RWKV_COLS), F32)],
        compiler_params=_params("parallel", "arbitrary"),
        name="rwkv_prep",
    )(p, mu, w0, wup, a0, aup, gup, k_k, k_a, bd)


def _rwkv_chunk_kernel(r_ref, lw_ref, k_ref, v_ref, kn_ref, b_ref, g_ref, gng_ref, gnb_ref, rk_ref,
                       o_ref, s_ref, *, n_chunks):
    C = RWKV_CHUNK

    @pl.when(pl.program_id(2) == 0)
    def _():
        s_ref[...] = jnp.zeros_like(s_ref)

    lane = _iota2((1, LANES), 1)
    m0 = (lane < HEAD).astype(F32)
    m1 = 1.0 - m0
    ri = _iota2((LANES, LANES), 0)
    ci = _iota2((LANES, LANES), 1)
    same = (ri >= HEAD) == (ci >= HEAD)
    smask = jnp.where(same & (ri > ci), 1.0, 0.0)
    imask = jnp.where(same & (ri >= ci), 1.0, 0.0)
    eye = jnp.where(ri == ci, 1.0, 0.0)
    bd_mean = _bf(jnp.where(same, 1.0 / HEAD, 0.0))
    bd_one = _bf(jnp.where(same, 1.0, 0.0))
    tri = _bf(jnp.where(_iota2((C, C), 0) >= _iota2((C, C), 1), 1.0, 0.0))

    def stack(x):
        return jnp.concatenate([x * m0, x * m1], axis=0)

    gng = gng_ref[...]
    gnb = gnb_ref[...]
    rk = rk_ref[...]

    def group(chunks):
        ld = lambda ref, c: ref[c * C:(c + 1) * C, :]
        r = [ld(r_ref, c).astype(F32) for c in chunks]
        lw = [ld(lw_ref, c) for c in chunks]
        k = [ld(k_ref, c).astype(F32) for c in chunks]
        v = [ld(v_ref, c).astype(F32) for c in chunks]
        kn = [ld(kn_ref, c).astype(F32) for c in chunks]
        b = [ld(b_ref, c).astype(F32) for c in chunks]
        idx = range(len(chunks))

        cs = [_dot_sel_rhs(tri, lw[i]) for i in idx]
        Rs, As, Ks, Bs, Rt, At, Be, Ke, Vs, g_end = ([] for _ in range(10))
        for i in idx:
            sh = cs[i][C // 2 - 1:C // 2, :]
            cs_end = cs[i][C - 1:C, :]
            e_pos = jnp.exp(cs[i] - sh)
            e_neg = jnp.exp(sh - cs[i])
            e_prev = jnp.exp(cs[i] - lw[i] - sh)
            e_end = jnp.exp(cs_end - cs[i])
            Rs.append(_bf(stack(r[i] * e_pos)))
            As.append(_bf(stack(-kn[i] * e_prev)))
            Ks.append(_bf(stack(k[i] * e_neg)))
            Bs.append(_bf(stack(b[i] * e_neg)))
            Rt.append(stack(r[i] * jnp.exp(cs[i])))
            At.append(_bf(stack(-kn[i] * jnp.exp(cs[i] - lw[i]))))
            Be.append(_bf(stack(b[i] * e_end)))
            Ke.append(_bf(stack(k[i] * e_end)))
            Vs.append(_bf(stack(v[i])))
            g_end.append(jnp.exp(cs_end))

        lanes2 = lambda x, y: jnp.concatenate([x, y], axis=1)
        L = LANES
        BK = [jnp.concatenate([Bs[i], Ks[i]], axis=0) for i in idx]
        GA = [_dot_nt(As[i], BK[i]) for i in idx]
        GR = [_dot_nt(Rs[i], BK[i]) for i in idx]
        Aab = [GA[i][:, :L] * smask for i in idx]
        Aak = [_bf(GA[i][:, L:] * smask) for i in idx]
        RBb = [_bf(GR[i][:, :L] * imask) for i in idx]
        RKb = [_bf(GR[i][:, L:] * imask) for i in idx]
        W1b = [_bf(_dot(Aak[i], Vs[i])) for i in idx]

        T = [eye + Aab[i] for i in idx]
        Pb = [_bf(Aab[i]) for i in idx]
        Pb = [_bf(_dot(Pb[i], Pb[i])) for i in idx]
        for _ in range(int(math.log2(C)) - 2):
            PT = [_dot(Pb[i], lanes2(Pb[i], _bf(T[i]))) for i in idx]
            Pb = [_bf(PT[i][:, :L]) for i in idx]
            T = [T[i] + PT[i][:, L:] for i in idx]
        T = [T[i] + _dot(Pb[i], _bf(T[i])) for i in idx]
        Tb = [_bf(T[i]) for i in idx]

        TU = [_dot(Tb[i], lanes2(At[i], W1b[i])) for i in idx]
        TUb = [_bf(TU[i]) for i in idx]
        TAb = [TUb[i][:, :L] for i in idx]
        U0b = [TUb[i][:, L:] for i in idx]
        RTU = [_dot(RBb[i], TUb[i]) for i in idx]
        Q = [_bf(Rt[i] + RTU[i][:, :L]) for i in idx]
        Y0 = [_dot(RKb[i], Vs[i]) + RTU[i][:, L:] for i in idx]
        Mt = [_bf(_dot_tn(TAb[i], Be[i])) for i in idx]
        Nt = [_dot_tn(Vs[i], Ke[i]) + _dot_tn(U0b[i], Be[i]) for i in idx]

        Y = []
        S0 = s_ref[...]
        for i in idx:
            S0b = _bf(S0)
            Y.append(_dot_nt(Q[i], S0b) + Y0[i])
            S0 = S0 * g_end[i] + _dot(S0b, Mt[i]) + Nt[i]
        s_ref[...] = S0

        rows = slice(chunks[0] * C, (chunks[-1] + 1) * C)
        cat = lambda xs: jnp.concatenate(xs, axis=0)
        y = cat([Y[i][:C, :] + Y[i][C:, :] for i in idx])
        r_all, k_all, v_all = cat(r), cat(k), cat(v)
        mean = _dot_sel_lhs(y, bd_mean, parts=2)
        yc = y - mean
        var = _dot_sel_lhs(yc * yc, bd_mean, parts=2)
        yn = yc * lax.rsqrt(var + RWKV_GN_EPS) * gng + gnb
        bonus = _dot_sel_lhs(r_all * k_all * rk, bd_one, parts=2) * v_all
        o_ref[rows, :] = _bf((yn + bonus) * g_ref[rows, :].astype(F32))

    for first in range(0, n_chunks, RWKV_GROUP):
        group(list(range(first, min(first + RWKV_GROUP, n_chunks))))


def _rwkv_chunk(r, lw, k, v, kn, b, g, gn_g, gn_b, r_k, B, S, n_chunks):
    n = B * S
    rows = n_chunks * RWKV_CHUNK
    ns = S // rows
    npair = RWKV_DIM // LANES
    blk = pl.BlockSpec((rows, LANES), lambda bb, p, s: (bb * ns + s, p))
    par = pl.BlockSpec((1, LANES), lambda bb, p, s: (0, p))
    return pl.pallas_call(
        functools.partial(_rwkv_chunk_kernel, n_chunks=n_chunks),
        grid=(B, npair, ns),
        in_specs=[blk] * 7 + [par] * 3,
        out_specs=blk,
        out_shape=jax.ShapeDtypeStruct((n, RWKV_DIM), BF16),
        scratch_shapes=[pltpu.VMEM((LANES, LANES), F32)],
        compiler_params=_params("parallel", "parallel", "arbitrary"),
        name="rwkv_chunk",
    )(r, lw, k, v, kn, b, g, gn_g, gn_b, r_k)


def _fox_c_kernel(ps_ref, pst_ref, bl_ref, bc_ref, cc_ref, cr_ref, *, seq):
    L = LANES
    ri = _iota2((L, L), 0)
    ci = _iota2((L, L), 1)
    tril = _bf(jnp.where(ri >= ci, 1.0, 0.0))
    triu = _bf(jnp.where(ri <= ci, 1.0, 0.0))
    carry = jnp.zeros((1, L), F32)
    carry_c = jnp.zeros((SMALL_ROWS, 1), F32)
    for i in range(seq // L):
        rows = slice(i * L, (i + 1) * L)
        cs = _dot_sel_rhs(tril, _log_sigmoid(ps_ref[rows, :] + bl_ref[...])) + carry
        cc_ref[rows, :] = cs
        carry = cs[L - 1:L, :]
        cst = _dot_sel_lhs(_log_sigmoid(pst_ref[:, rows] + bc_ref[...]), triu) + carry_c
        cr_ref[:, rows] = cst
        carry_c = cst[:, L - 1:L]


def _fox_c(ps, pst, bias_lane, bias_col, B, S):
    n = B * S
    return pl.pallas_call(
        functools.partial(_fox_c_kernel, seq=S),
        grid=(B,),
        in_specs=[pl.BlockSpec((S, LANES), lambda b: (b, 0)),
                  pl.BlockSpec((SMALL_ROWS, S), lambda b: (0, b)),
                  pl.BlockSpec((1, LANES), lambda b: (0, 0)),
                  pl.BlockSpec((SMALL_ROWS, 1), lambda b: (0, 0))],
        out_specs=[pl.BlockSpec((S, LANES), lambda b: (b, 0)),
                   pl.BlockSpec((SMALL_ROWS, S), lambda b: (0, b))],
        out_shape=[jax.ShapeDtypeStruct((n, LANES), F32),
                   jax.ShapeDtypeStruct((SMALL_ROWS, n), F32)],
        compiler_params=_params("parallel"),
        name="fox_cumgate",
    )(ps, pst, bias_lane, bias_col)


def _transpose_blocks(x):
    n = x.shape[0] // LANES
    return jnp.concatenate([x[i * LANES:(i + 1) * LANES, :].T for i in range(n)], axis=1)


def _fox_attn_kernel(q_ref, k_ref, v_ref, cc_ref, cr_ref, o_ref, vt_ref, ck_ref, s_scr, p_scr, *, tq, seq):
    nblk = q_ref.shape[1] // LANES
    heads = nblk * PAIR
    head0 = pl.program_id(1) * heads
    qi = pl.program_id(2)

    @pl.when(qi == 0)
    def _():
        for i in range(seq // LANES):
            rows = slice(i * LANES, (i + 1) * LANES)
            for c in range(nblk):
                cols = slice(c * LANES, (c + 1) * LANES)
                vt_ref[cols, rows] = _bf(v_ref[rows, cols].astype(F32).T)
        ri = _iota2((LANES, LANES), 0)
        for h in range(heads):
            sel = _bf(jnp.where(ri == head0 + h, 1.0, 0.0))
            ck_ref[h] = _dot_sel_lhs(cc_ref[...], sel) * LOG2E

    in0 = _iota2((LANES, 1), 0) < HEAD
    qts = []
    for c in range(nblk):
        qt = _transpose_blocks(q_ref[:, c * LANES:(c + 1) * LANES].astype(F32) * (HEAD ** -0.5 * LOG2E))
        qts += [_bf(jnp.where(in0, qt, 0.0)), _bf(jnp.where(in0, 0.0, qt))]
    q0 = pl.multiple_of(qi * tq, tq)
    cqs = tuple(cr_ref[pl.ds(head0 + h, 1), pl.ds(q0, tq)] * LOG2E for h in range(heads))
    key_minus_query = _iota2((tq, tq), 0) - _iota2((tq, tq), 1)
    last_block = seq // tq - 1

    s_scr[...] = jnp.full(s_scr.shape, -jnp.inf, F32)
    p_scr[...] = jnp.zeros(p_scr.shape, BF16)

    def step(j, carry, diagonal=False, scores=True, softmax=True):
        pv0 = pl.multiple_of(jnp.clip(j - 2, 0, last_block) * tq, tq)
        if scores:
            qk0 = pl.multiple_of(j * tq, tq)
            kbs = [k_ref[pl.ds(qk0, tq), c * LANES:(c + 1) * LANES] for c in range(nblk)]
        new = []
        for h in range(heads):
            m, l, alpha, acc = carry[h]
            vt = vt_ref[h * HEAD:(h + 1) * HEAD, pl.ds(pv0, tq)]
            acc = alpha * acc + _dot(vt, p_scr[h])
            if softmax:
                s = s_scr[h]
                m_new = jnp.maximum(m, jnp.max(s, axis=0, keepdims=True))
                p = jnp.exp2(s - m_new)
                alpha = jnp.exp2(m - m_new)
                l = alpha * l + jnp.sum(p, axis=0, keepdims=True)
                m = m_new
                p_scr[h] = _bf(p)
            if scores:
                ck = ck_ref[h, pl.ds(qk0, tq), :]
                s_next = _dot(kbs[h // PAIR], qts[h]) + cqs[h] - jnp.concatenate([ck] * (tq // LANES), axis=1)
                s_scr[h] = jnp.where(key_minus_query <= 0, s_next, -jnp.inf) if diagonal else s_next
            new.append((m, l, alpha, acc))
        return tuple(new)

    init = tuple((jnp.full((1, tq), -1e30, F32), jnp.zeros((1, tq), F32), jnp.ones((1, tq), F32),
                  jnp.zeros((HEAD, tq), F32)) for _ in range(heads))
    carry = lax.fori_loop(0, qi, step, init)
    carry = step(qi, carry, diagonal=True)
    carry = step(qi + 1, carry, scores=False)
    carry = step(qi + 2, carry, scores=False, softmax=False)
    ot = jnp.concatenate([acc / l for (_, l, _, acc) in carry], axis=0)
    o_ref[...] = jnp.concatenate(
        [jnp.concatenate([ot[c * LANES:(c + 1) * LANES, i * LANES:(i + 1) * LANES].T for c in range(nblk)], axis=1)
         for i in range(tq // LANES)], axis=0).astype(o_ref.dtype)


def _fox_attn(qkv, cc, cr, B, S, tq):
    n = B * S
    nq = S // tq
    w = FOX_WIDTH
    nw = FOX_DIM // w
    heads = w // HEAD
    return pl.pallas_call(
        functools.partial(_fox_attn_kernel, tq=tq, seq=S),
        grid=(B, nw, nq),
        in_specs=[pl.BlockSpec((tq, w), lambda b, p, i: (b * nq + i, p)),
                  pl.BlockSpec((S, w), lambda b, p, i: (b, nw + p)),
                  pl.BlockSpec((S, w), lambda b, p, i: (b, 2 * nw + p)),
                  pl.BlockSpec((S, LANES), lambda b, p, i: (b, 0)),
                  pl.BlockSpec((SMALL_ROWS, S), lambda b, p, i: (0, b))],
        out_specs=pl.BlockSpec((tq, w), lambda b, p, i: (b * nq + i, p)),
        out_shape=jax.ShapeDtypeStruct((n, FOX_DIM), BF16),
        scratch_shapes=[pltpu.VMEM((w, S), BF16), pltpu.VMEM((heads, S, LANES), F32),
                        pltpu.VMEM((heads, tq, tq), F32), pltpu.VMEM((heads, tq, tq), BF16)],
        compiler_params=_params("parallel", "parallel", "arbitrary"),
        name="fox_attn",
    )(qkv, qkv, qkv, cc, cr)


def _mamba_kernel(z_ref, xbc_ref, ps_ref, pst_ref, cw_ref, cb_ref, dtbl_ref, dtbc_ref, al_ref, ac_ref,
                  dl_ref, ng_ref, ex_ref, o_ref, halo_ref, st_ref):
    L = SSD_CHUNK

    @pl.when(pl.program_id(1) == 0)
    def _():
        halo_ref[...] = jnp.zeros_like(halo_ref)
        st_ref[...] = jnp.zeros_like(st_ref)

    xbc = xbc_ref[...].astype(F32)
    halo = halo_ref[...]
    row8 = _iota2((SUBLANES, 1), 0)
    acc = cb_ref[...] + cw_ref[CONV_WIDTH - 1:CONV_WIDTH, :] * xbc
    for sft in range(1, CONV_WIDTH):
        rolled = pltpu.roll(xbc, sft, axis=0)
        top = jnp.where(row8 < sft, pltpu.roll(halo, sft, axis=0), rolled[:SUBLANES, :])
        shifted = jnp.concatenate([top, rolled[SUBLANES:, :]], axis=0)
        acc = acc + cw_ref[CONV_WIDTH - 1 - sft:CONV_WIDTH - sft, :] * shifted
    halo_ref[...] = xbc[L - SUBLANES:, :]
    act = acc * _sigmoid(acc)
    xs = act[:, :SSM_DIM]
    Bm = act[:, SSM_DIM:SSM_DIM + SSM_GROUPS * SSM_STATE]
    Cm = act[:, SSM_DIM + SSM_GROUPS * SSM_STATE:]

    ri = _iota2((L, L), 0)
    ci = _iota2((L, L), 1)
    lower = ri >= ci
    tril = _bf(jnp.where(lower, 1.0, 0.0))
    triu = _bf(jnp.where(ri <= ci, 1.0, 0.0))
    lane = _iota2((1, LANES), 1)
    in0 = lane < HEAD

    dt_full = _dot_sel_lhs(_softplus(ps_ref[...] + dtbl_ref[...]), ex_ref[...])
    cs = _dot_sel_rhs(tril, dt_full * al_ref[...])
    cs_end = cs[L - 1:L, :]
    ecs = jnp.exp(cs)
    X = xs * dt_full
    Xe = X * jnp.exp(cs_end - cs)
    csr = _dot_sel_lhs(_softplus(pst_ref[...] + dtbc_ref[...]) * ac_ref[...], triu)

    heads_per_group = SSM_HEADS // SSM_GROUPS
    pairs_per_group = heads_per_group // PAIR
    ys = []
    for g in range(SSM_GROUPS):
        Bg = Bm[:, g * SSM_STATE:(g + 1) * SSM_STATE]
        Cg = _bf(Cm[:, g * SSM_STATE:(g + 1) * SSM_STATE])
        CB = _dot_nt(Cg, _bf(Bg))
        BgT = _bf(Bg.T)
        for pp in range(pairs_per_group):
            p = g * pairs_per_group + pp
            cols = slice(p * LANES, (p + 1) * LANES)
            Xp = X[:, cols]
            ydiag = None
            for hh in range(PAIR):
                h = PAIR * p + hh
                decay = jnp.exp(cs[:, h * HEAD:h * HEAD + 1] - csr[DT_OFF + h:DT_OFF + h + 1, :])
                Mh = _bf(CB * jnp.where(lower, decay, 0.0))
                Xh = _bf(jnp.where(in0, Xp, 0.0) if hh == 0 else jnp.where(in0, 0.0, Xp))
                t = _dot(Mh, Xh)
                ydiag = t if ydiag is None else ydiag + t
            st = st_ref[p]
            yoff = _dot(Cg, _bf(st)) * ecs[:, cols]
            st_ref[p] = st * ecs[L - 1:L, cols] + _dot(BgT, _bf(Xe[:, cols]))
            ys.append(ydiag + yoff + dl_ref[:, cols] * xs[:, cols])
    z = z_ref[...].astype(F32)
    y = jnp.concatenate(ys, axis=1) * (z * _sigmoid(z))
    gw = SSM_DIM // SSM_GROUPS
    outs = []
    for g in range(SSM_GROUPS):
        yg = y[:, g * gw:(g + 1) * gw]
        ms = jnp.mean(yg * yg, axis=-1, keepdims=True)
        outs.append(yg * lax.rsqrt(ms + NORM_EPS))
    o_ref[...] = _bf(jnp.concatenate(outs, axis=1) * ng_ref[...])


def _mamba(z, xbc, ps, pst, cw, cb, dtbl, dtbc, al, ac, dl, ng, ex, B, S):
    n = B * S
    L = SSD_CHUNK
    nc = S // L
    row = lambda w: pl.BlockSpec((L, w), lambda b, c: (b * nc + c, 0))
    full = lambda a: pl.BlockSpec(a.shape, lambda b, c: (0,) * a.ndim)
    return pl.pallas_call(
        _mamba_kernel,
        grid=(B, nc),
        in_specs=[row(SSM_DIM), row(SSM_CONV_DIM), row(LANES),
                  pl.BlockSpec((SMALL_ROWS, L), lambda b, c: (0, b * nc + c))]
                 + [full(a) for a in (cw, cb, dtbl, dtbc, al, ac, dl, ng, ex)],
        out_specs=row(SSM_DIM),
        out_shape=jax.ShapeDtypeStruct((n, SSM_DIM), BF16),
        scratch_shapes=[pltpu.VMEM((SUBLANES, SSM_CONV_DIM), F32),
                        pltpu.VMEM((SSM_HEADS // PAIR, SSM_STATE, LANES), F32)],
        compiler_params=_params("parallel", "arbitrary"),
        name="mamba_ssd",
    )(z, xbc, ps, pst, cw, cb, dtbl, dtbc, al, ac, dl, ng, ex)


def _layer_norm(h, g, b):
    mu = jnp.mean(h, axis=-1, keepdims=True)
    hc = h - mu
    var = jnp.mean(hc * hc, axis=-1, keepdims=True)
    return hc * lax.rsqrt(var + LN_EPS) * g + b


def _merge_kernel(or_ref, of_ref, om_ref, pg_ref, x_ref, wr_ref, wf_ref, wm_ref, wo_ref, gb_ref,
                  lg_ref, lb_ref, rwh_ref, rwl_ref, x1_ref, lgt_ref, *, alpha):
    D = D_MODEL
    gate = lambda i: _sigmoid(pg_ref[:, i * D:(i + 1) * D].astype(F32) + gb_ref[:, i * D:(i + 1) * D])
    merged = (gate(0) * _dot(or_ref[...], wr_ref[...])
              + gate(1) * _dot(of_ref[...], wf_ref[...])
              + gate(2) * _dot(om_ref[...], wm_ref[...]))
    mix = _dot(_bf(merged), wo_ref[...])
    x1 = _layer_norm(alpha * x_ref[...] + mix, lg_ref[...], lb_ref[...])
    x1_ref[...] = x1
    xh = _bf(x1)
    xl = _bf(x1 - xh.astype(F32))
    lgt_ref[...] = (_dot_nt(rwh_ref[...], xh) + _dot_nt(rwh_ref[...], xl) + _dot_nt(rwl_ref[...], xh))


def _merge(o_r, o_f, o_m, pg, x, wr, wf, wm, wo, gb, lg, lb, rwh, rwl, tm, alpha):
    n = x.shape[0]
    row = lambda w: pl.BlockSpec((tm, w), lambda i: (i, 0))
    full = lambda a: pl.BlockSpec(a.shape, lambda i: (0,) * a.ndim)
    return pl.pallas_call(
        functools.partial(_merge_kernel, alpha=alpha),
        grid=(n // tm,),
        in_specs=[row(RWKV_DIM), row(FOX_DIM), row(SSM_DIM), row(3 * D_MODEL), row(D_MODEL)]
                 + [full(a) for a in (wr, wf, wm, wo, gb, lg, lb, rwh, rwl)],
        out_specs=[row(D_MODEL), pl.BlockSpec((N_EXPERTS, tm), lambda i: (0, i))],
        out_shape=[jax.ShapeDtypeStruct((n, D_MODEL), F32),
                   jax.ShapeDtypeStruct((N_EXPERTS, n), F32)],
        compiler_params=_params("parallel"),
        name="merge_ln1_router",
    )(o_r, o_f, o_m, pg, x, wr, wf, wm, wo, gb, lg, lb, rwh, rwl)


def _rows(*vals):
    t = vals[0].shape[1]
    sub = _iota2((SUBLANES, t), 0)
    out = jnp.zeros((SUBLANES, t), vals[0].dtype)
    for i, v in enumerate(vals):
        out = jnp.where(sub == i, v, out)
    return out


def _route_select_kernel(lgt_ref, bias_ref, e_ref, g_ref, rank_ref, cnt_ref, carry_ref):
    @pl.when(pl.program_id(0) == 0)
    def _():
        carry_ref[...] = jnp.zeros_like(carry_ref)

    t = lgt_ref.shape[1]
    aff = _sigmoid(lgt_ref[...])
    sel = aff + bias_ref[...]
    gsz = EXPERTS_PER_GROUP
    sub = _iota2((gsz, t), 0).astype(F32)
    best = e1 = e2 = None
    for g in range(N_EXPERT_GROUPS):
        s = sel[g * gsz:(g + 1) * gsz, :]
        m1 = jnp.max(s, axis=0, keepdims=True)
        i1 = jnp.min(jnp.where(s == m1, sub, float(gsz)), axis=0, keepdims=True)
        s2 = jnp.where(sub == i1, -jnp.inf, s)
        m2 = jnp.max(s2, axis=0, keepdims=True)
        i2 = jnp.min(jnp.where(s2 == m2, sub, float(gsz)), axis=0, keepdims=True)
        score = m1 + m2
        if g == 0:
            best, e1, e2 = score, i1, i2
        else:
            better = score > best
            best = jnp.where(better, score, best)
            e1 = jnp.where(better, i1 + float(g * gsz), e1)
            e2 = jnp.where(better, i2 + float(g * gsz), e2)
    row = _iota2((N_EXPERTS, t), 0).astype(F32)
    oh1 = jnp.where(row == e1, 1.0, 0.0)
    oh2 = jnp.where(row == e2, 1.0, 0.0)
    w1 = jnp.sum(oh1 * aff, axis=0, keepdims=True)
    w2 = jnp.sum(oh2 * aff, axis=0, keepdims=True)
    den = w1 + w2
    cnt = oh1 + oh2
    before = _bf(jnp.where(_iota2((t, t), 0) < _iota2((t, t), 1), 1.0, 0.0))
    prefix = _dot(_bf(cnt), before) + carry_ref[...]
    r1 = jnp.sum(oh1 * prefix, axis=0, keepdims=True)
    r2 = jnp.sum(oh2 * prefix, axis=0, keepdims=True)
    carry_ref[...] = carry_ref[...] + jnp.sum(cnt, axis=1, keepdims=True)
    e_ref[...] = _rows(e1, e2)
    g_ref[...] = _rows(w1 / den, w2 / den)
    rank_ref[...] = _rows(r1, r2)
    cnt_ref[...] = jnp.broadcast_to(carry_ref[...], cnt_ref.shape)


def _route_select(lgt, bias_col, tile):
    n = lgt.shape[1]
    col = lambda r: pl.BlockSpec((r, tile), lambda i: (0, i))
    out = jax.ShapeDtypeStruct((SUBLANES, n), F32)
    return pl.pallas_call(
        _route_select_kernel,
        grid=(n // tile,),
        in_specs=[col(N_EXPERTS), pl.BlockSpec((N_EXPERTS, 1), lambda i: (0, 0))],
        out_specs=[col(SUBLANES)] * 3 + [pl.BlockSpec((N_EXPERTS, LANES), lambda i: (0, 0))],
        out_shape=[out] * 3 + [jax.ShapeDtypeStruct((N_EXPERTS, LANES), F32)],
        scratch_shapes=[pltpu.VMEM((N_EXPERTS, 1), F32)],
        compiler_params=_params("arbitrary"),
        name="route_select",
    )(lgt, bias_col)


def _route_place_kernel(e_ref, rank_ref, cnt_ref, dest_ref, be_ref, nu_ref):
    t = e_ref.shape[1]
    blk = float(MOE_BLOCK)
    counts = cnt_ref[...]
    padded = jnp.floor((counts + (blk - 1.0)) / blk) * blk
    ne = N_EXPERTS
    lower = _bf(jnp.where(_iota2((ne, ne), 0) > _iota2((ne, ne), 1), 1.0, 0.0))
    pad_start = _dot_sel_rhs(lower, padded)
    pad_end = pad_start + padded
    row = _iota2((ne, t), 0).astype(F32)
    ps_t = jnp.concatenate([pad_start] * (t // LANES), axis=1)
    e = e_ref[...]
    rank = rank_ref[...]
    dests = []
    for k in range(TOP_K):
        oh = jnp.where(row == e[k:k + 1, :], 1.0, 0.0)
        dests.append(jnp.sum(oh * ps_t, axis=0, keepdims=True) + rank[k:k + 1, :])
    dest_ref[...] = _rows(*dests).astype(jnp.int32)
    nbp = be_ref.shape[1]
    blk_start = _iota2((ne, nbp), 1).astype(F32) * blk
    pe_t = jnp.concatenate([pad_end] * (nbp // LANES), axis=1)
    be = jnp.sum(jnp.where(pe_t <= blk_start, 1.0, 0.0), axis=0, keepdims=True)
    be_ref[...] = jnp.broadcast_to(jnp.minimum(be, float(ne - 1)), be_ref.shape).astype(jnp.int32)
    nu_ref[...] = jnp.broadcast_to(pad_end[ne - 1:ne, :] / blk, nu_ref.shape).astype(jnp.int32)


def _route_place(e, rank, cnt, tile, n_blocks):
    n = e.shape[1]
    nbp = -(-n_blocks // LANES) * LANES
    col = pl.BlockSpec((SUBLANES, tile), lambda i: (0, i))
    return pl.pallas_call(
        _route_place_kernel,
        grid=(n // tile,),
        in_specs=[col, col, pl.BlockSpec((N_EXPERTS, LANES), lambda i: (0, 0))],
        out_specs=[col, pl.BlockSpec((SUBLANES, nbp), lambda i: (0, 0)),
                   pl.BlockSpec((SUBLANES, LANES), lambda i: (0, 0))],
        out_shape=[jax.ShapeDtypeStruct((SUBLANES, n), jnp.int32),
                   jax.ShapeDtypeStruct((SUBLANES, nbp), jnp.int32),
                   jax.ShapeDtypeStruct((SUBLANES, LANES), jnp.int32)],
        compiler_params=_params("arbitrary"),
        name="route_place",
    )(e, rank, cnt)


def _moe_kernel(be_ref, nu_ref, xb_ref, wg_ref, wu_ref, wd_ref, yb_ref, wgb, wub, wdb):
    i = pl.program_id(0)
    new_expert = jnp.logical_or(i == 0, be_ref[i] != be_ref[jnp.maximum(i - 1, 0)])

    @pl.when(new_expert)
    def _():
        wgb[...] = _bf(wg_ref[0])
        wub[...] = _bf(wu_ref[0])
        wdb[...] = _bf(wd_ref[0])

    @pl.when(i < nu_ref[0])
    def _():
        x = _bf(xb_ref[...])
        gt = _dot(x, wgb[...])
        h = gt * _sigmoid(gt) * _dot(x, wub[...])
        yb_ref[...] = _dot(_bf(h), wdb[...])

    @pl.when(i >= nu_ref[0])
    def _():
        yb_ref[...] = jnp.zeros_like(yb_ref)


def _moe(block_expert, n_used, xb, wg, wu, wd):
    rows, d = xb.shape
    blk = MOE_BLOCK
    nb = rows // blk
    grid_spec = pltpu.PrefetchScalarGridSpec(
        num_scalar_prefetch=2,
        grid=(nb,),
        in_specs=[pl.BlockSpec((blk, d), lambda i, be, nu: (i, 0)),
                  pl.BlockSpec((1, d, D_EXPERT), lambda i, be, nu: (be[i], 0, 0)),
                  pl.BlockSpec((1, d, D_EXPERT), lambda i, be, nu: (be[i], 0, 0)),
                  pl.BlockSpec((1, D_EXPERT, d), lambda i, be, nu: (be[i], 0, 0))],
        out_specs=pl.BlockSpec((blk, d), lambda i, be, nu: (i, 0)),
        scratch_shapes=[pltpu.VMEM((d, D_EXPERT), BF16), pltpu.VMEM((d, D_EXPERT), BF16),
                        pltpu.VMEM((D_EXPERT, d), BF16)])
    return pl.pallas_call(
        _moe_kernel,
        grid_spec=grid_spec,
        out_shape=jax.ShapeDtypeStruct((rows, d), F32),
        compiler_params=_params("arbitrary"),
        name="moe_experts",
    )(block_expert, n_used, xb, wg, wu, wd)


def _dispatch_kernel(dest_ref, x_ref, buf_in_hbm, buf_hbm, sem, *, tile, n):
    del buf_in_hbm
    base = pl.program_id(0) * tile

    def row_copy(j, k):
        return pltpu.make_async_copy(x_ref.at[pl.ds(j, 1)],
                                     buf_hbm.at[pl.ds(dest_ref[k * n + base + j], 1)], sem)

    def start(j, carry):
        for k in range(TOP_K):
            row_copy(j, k).start(priority=k)
        return carry

    def wait(j, carry):
        for k in range(TOP_K):
            row_copy(j, k).wait()
        return carry

    lax.fori_loop(0, tile, start, 0, unroll=DMA_UNROLL)
    lax.fori_loop(0, tile, wait, 0, unroll=DMA_UNROLL)


def _dispatch(dest_flat, x, buf, tile):
    n, d = x.shape
    any_spec = pl.BlockSpec(memory_space=pl.ANY)
    grid_spec = pltpu.PrefetchScalarGridSpec(
        num_scalar_prefetch=1,
        grid=(n // tile,),
        in_specs=[pl.BlockSpec((tile, d), lambda i, dest: (i, 0)), any_spec],
        out_specs=any_spec,
        scratch_shapes=[pltpu.SemaphoreType.DMA(())])
    return pl.pallas_call(
        functools.partial(_dispatch_kernel, tile=tile, n=n),
        grid_spec=grid_spec,
        out_shape=jax.ShapeDtypeStruct(buf.shape, buf.dtype),
        input_output_aliases={2: 0},
        compiler_params=_params("arbitrary"),
        name="moe_dispatch",
    )(dest_flat, x, buf)


def _combine_ln2_kernel(dest_ref, x_ref, gate_ref, yb_hbm, g_ref, b_ref, o_ref, ob_ref, buf, sems,
                        *, tile, n, alpha):
    i = pl.program_id(0)
    steps = pl.num_programs(0)

    def row_copy(step, slot, j, k):
        t = step * tile + j
        return pltpu.make_async_copy(yb_hbm.at[pl.ds(dest_ref[k * n + t], 1)],
                                     buf.at[slot, k, pl.ds(j, 1)], sems.at[slot])

    def start_tile(step, slot):
        def body(j, carry):
            for k in range(TOP_K):
                row_copy(step, slot, j, k).start(priority=k)
            return carry
        lax.fori_loop(0, tile, body, 0, unroll=DMA_UNROLL)

    def wait_tile(step, slot):
        def body(j, carry):
            for k in range(TOP_K):
                row_copy(step, slot, j, k).wait()
            return carry
        lax.fori_loop(0, tile, body, 0, unroll=DMA_UNROLL)

    slot = lax.rem(i, 2)

    @pl.when(i == 0)
    def _():
        start_tile(0, 0)

    for s in range(2):
        @pl.when(slot == s)
        def _():
            @pl.when(i + 1 < steps)
            def _():
                start_tile(i + 1, 1 - s)
            wait_tile(i, s)

    gate = gate_ref[...]
    ffn = gate[:, 0:1] * buf[slot, 0] + gate[:, 1:2] * buf[slot, 1]
    y = _layer_norm(alpha * x_ref[...] + ffn, g_ref[...], b_ref[...])
    o_ref[...] = y
    ob_ref[...] = _bf(y)


def _combine_ln2(dest_flat, x, gate, yb, g, b, tile, alpha):
    n, d = x.shape
    row = pl.BlockSpec((tile, d), lambda i, dest: (i, 0))
    par = pl.BlockSpec((1, d), lambda i, dest: (0, 0))
    grid_spec = pltpu.PrefetchScalarGridSpec(
        num_scalar_prefetch=1,
        grid=(n // tile,),
        in_specs=[row, pl.BlockSpec((tile, TOP_K), lambda i, dest: (i, 0)),
                  pl.BlockSpec(memory_space=pl.ANY), par, par],
        out_specs=[row, row],
        scratch_shapes=[pltpu.VMEM((2, TOP_K, tile, d), F32), pltpu.SemaphoreType.DMA((2,))])
    return pl.pallas_call(
        functools.partial(_combine_ln2_kernel, tile=tile, n=n, alpha=alpha),
        grid_spec=grid_spec,
        out_shape=[jax.ShapeDtypeStruct((n, d), F32), jax.ShapeDtypeStruct((n, d), BF16)],
        compiler_params=_params("arbitrary"),
        name="moe_combine_ln2",
    )(dest_flat, x, gate, yb, g, b)


def _route(lgt, bias_col, tile):
    n = lgt.shape[1]
    n_blocks = -(-(n * TOP_K) // MOE_BLOCK) + N_EXPERTS
    e, g, rank, cnt = _route_select(lgt, bias_col, tile)
    dest, be, nu = _route_place(e, rank, cnt, tile, n_blocks)
    return g[:TOP_K].T, dest[:TOP_K].reshape(-1), be[0, :n_blocks], nu[0, :1], n_blocks


def _proj_tn(width):
    blocks = width // LANES
    best = max(d for d in range(1, blocks + 1) if blocks % d == 0 and d * LANES <= 1024)
    return best * LANES


def _tiles(B, S):
    n = B * S
    return dict(
        tm_proj=min(1024, n),
        ts_prep=min(512, S),
        rwkv_chunks=min(8, S // RWKV_CHUNK),
        tq=min(256, S),
        tm_merge=min(512, n),
        t_route=min(512, n),
        t_dispatch=min(512, n),
        t_combine=min(256, n),
    )


def _pad_rows(a, before, total):
    return jnp.zeros((total, a.shape[1]), a.dtype).at[before:before + a.shape[0]].set(a)


def _block_diag_ones(dim):
    i = jnp.arange(dim) // HEAD
    return (i[:, None] == i[None, :]).astype(BF16)


def kernel(x, w_in, rwkv_mu, rwkv_w0, rwkv_w_up, rwkv_a0, rwkv_a_up, rwkv_g_up, rwkv_k_k, rwkv_k_a,
           rwkv_r_k, rwkv_gn_g, rwkv_gn_b, fox_f_bias, ssm_conv_w, ssm_conv_b, ssm_dt_bias, ssm_a_log,
           ssm_d, ssm_norm_g, gate_bias, w_br_rwkv, w_br_fox, w_br_ssm, w_out, ln1_g, ln1_b, router_w,
           router_bias, exp_w_gate, exp_w_up, exp_w_down, ln2_g, ln2_b):
    B, S, D = x.shape
    depth = w_in.shape[0]
    n = B * S
    t = _tiles(B, S)
    alpha = (2 * depth) ** 0.25

    row1 = lambda a: a.reshape(1, -1).astype(F32)
    bd512 = _block_diag_ones(RWKV_DIM)
    expand = (jnp.arange(LANES)[:, None] == DT_OFF + jnp.arange(SSM_DIM)[None, :] // HEAD).astype(BF16)
    rw = router_w.astype(F32).T
    rwh = rw.astype(BF16)
    rwl = (rw - rwh.astype(F32)).astype(BF16)
    rb_col = router_bias.astype(F32).reshape(N_EXPERTS, 1)

    xf = x.reshape(n, D).astype(F32)
    xb16 = xf.astype(BF16)
    xb = None
    for l in range(depth):
        w = w_in[l].astype(BF16)
        w_small = jnp.concatenate([w[:, OFF_SSM - FOX_HEADS:OFF_SSM], w[:, OFF_GATE - SSM_HEADS:OFF_GATE]], axis=1)
        w_small_l = jnp.zeros((D, LANES), BF16).at[:, :w_small.shape[1]].set(w_small)
        w_small_t = jnp.zeros((SMALL_ROWS, D), BF16).at[:w_small.shape[1]].set(w_small.T)
        mm = lambda lo, hi: _matmul(xb16, w[:, lo:hi], t["tm_proj"], _proj_tn(hi - lo), BF16)
        p_rwkv = mm(0, RWKV_COLS)
        p_fox = mm(OFF_FOX, OFF_FOX + 3 * FOX_DIM)
        p_z = mm(OFF_SSM, OFF_SSM + SSM_DIM)
        p_xbc = mm(OFF_SSM + SSM_DIM, OFF_SSM + SSM_DIM + SSM_CONV_DIM)
        p_gate = mm(OFF_GATE, OFF_GATE + 3 * D)
        ps, pst = _small_proj(xb16, w_small_l, w_small_t, t["tm_proj"])

        wup = _pad_rows(rwkv_w_up[l], 0, DECAY_LORA + ICLR_LORA).astype(BF16)
        aup = _pad_rows(rwkv_a_up[l], DECAY_LORA, DECAY_LORA + ICLR_LORA).astype(BF16)
        r, lw, k, v, kn, b, g = _rwkv_prep(
            p_rwkv, B, S, t["ts_prep"], row1(rwkv_mu[l]), row1(rwkv_w0[l]), wup, row1(rwkv_a0[l]), aup,
            rwkv_g_up[l].astype(BF16), row1(rwkv_k_k[l]), row1(rwkv_k_a[l]), bd512)
        o_r = _rwkv_chunk(r, lw, k, v, kn, b, g, row1(rwkv_gn_g[l]), row1(rwkv_gn_b[l]),
                          row1(rwkv_r_k[l]), B, S, t["rwkv_chunks"])

        fb_lane = jnp.zeros((1, LANES), F32).at[0, :FOX_HEADS].set(fox_f_bias[l].astype(F32))
        fb_col = jnp.zeros((SMALL_ROWS, 1), F32).at[:FOX_HEADS, 0].set(fox_f_bias[l].astype(F32))
        cc, cr = _fox_c(ps, pst, fb_lane, fb_col, B, S)
        o_f = _fox_attn(p_fox, cc, cr, B, S, t["tq"])

        dtb = ssm_dt_bias[l].astype(F32)
        a_neg = -jnp.exp(ssm_a_log[l].astype(F32))
        dtbl = jnp.zeros((1, LANES), F32).at[0, DT_OFF:DT_OFF + SSM_HEADS].set(dtb)
        dtbc = jnp.zeros((SMALL_ROWS, 1), F32).at[DT_OFF:DT_OFF + SSM_HEADS, 0].set(dtb)
        ac = jnp.zeros((SMALL_ROWS, 1), F32).at[DT_OFF:DT_OFF + SSM_HEADS, 0].set(a_neg)
        al = jnp.repeat(a_neg, HEAD).reshape(1, SSM_DIM)
        dl = jnp.repeat(ssm_d[l].astype(F32), HEAD).reshape(1, SSM_DIM)
        o_m = _mamba(p_z, p_xbc, ps, pst, ssm_conv_w[l].astype(F32), row1(ssm_conv_b[l]), dtbl, dtbc,
                     al, ac, dl, row1(ssm_norm_g[l]), expand, B, S)

        x1, logits = _merge(
            o_r, o_f, o_m, p_gate, xf, w_br_rwkv[l].astype(BF16), w_br_fox[l].astype(BF16),
            w_br_ssm[l].astype(BF16), w_out[l].astype(BF16), row1(gate_bias[l]), row1(ln1_g[l]),
            row1(ln1_b[l]), rwh, rwl, t["tm_merge"], alpha)

        gate, dest, block_expert, n_used, n_blocks = _route(logits, rb_col, t["t_route"])
        if xb is None:
            xb = jnp.zeros((n_blocks * MOE_BLOCK, D), F32)
        xb = _dispatch(dest, x1, xb, t["t_dispatch"])
        yb = _moe(block_expert, n_used, xb, exp_w_gate[l], exp_w_up[l], exp_w_down[l])
        xf, xb16 = _combine_ln2(dest, x1, gate, yb, row1(ln2_g[l]), row1(ln2_b[l]), t["t_combine"], alpha)
    return xf.reshape(B, S, D).astype(x.dtype)
```
